```python
import jax
import jax.numpy as jnp
from jax import lax
import numpy as np


D_MODEL = 1024
BATCH = 8
SEQ = 4096
DEPTH = 4

GRID_W = 64
CTX_LEN = 256
EPS = 1e-6
F_MIN = 1e-6
A_HEAD_DIM = 128
A_HEADS = D_MODEL // (2 * A_HEAD_DIM)
A_WIDTH = A_HEADS * A_HEAD_DIM
HGRN_CHUNK = 64
B_GROUPS = 4
B_WIDTH = D_MODEL - A_WIDTH
B_GROUP_DIM = B_WIDTH // B_GROUPS
B_CHUNK = 128
EV_IN = 5 * A_WIDTH + 2 * B_WIDTH
C_WIDTH = D_MODEL // 4
C_GROUPS = 4
C_GROUP_DIM = C_WIDTH // C_GROUPS
MLA_V_DIM = 128
MLA_HEADS = (D_MODEL - C_WIDTH) // MLA_V_DIM
MLA_NOPE = 128
MLA_ROPE = 64
MLA_QK = MLA_NOPE + MLA_ROPE
Q_LORA = 384
KV_LORA = 256
OD_IN = C_WIDTH + Q_LORA + KV_LORA + MLA_ROPE
ROPE_THETA = 10000.0
ATTN_BLOCK = 128
D_FF = 2816
CONV_W = 3

kernel_name = 'hybrid_flow_backbone'


def rms_norm(x, w):
    xf = x.astype(jnp.float32)
    y = xf * lax.rsqrt(jnp.mean(xf * xf, axis=-1, keepdims=True) + EPS)
    return y.astype(x.dtype) * w


def modulate(x, norm_w, shift, scale):
    return rms_norm(x, norm_w) * (1 + scale) + shift


def to_heads(a, n_heads):
    bsz, t_len, width = a.shape
    return a.reshape(bsz, t_len, n_heads, width // n_heads).transpose(0, 2, 1, 3)


def flip_t(a):
    return a[:, :, ::-1]


def gla_chunked(q, k, v, log_f, s0):
    bsz, nh, t_len, _ = q.shape
    dv = v.shape[-1]
    n_chunks = t_len // HGRN_CHUNK

    def to_chunks(a):
        return a.reshape(bsz, nh, n_chunks, HGRN_CHUNK, a.shape[-1]).transpose(2, 0, 1, 3, 4)

    prefix_mask = jnp.tril(jnp.ones((HGRN_CHUNK, HGRN_CHUNK), dtype=bool))[:, :, None]

    def step(state, inp):
        qc, kc, vc, gc = inp
        b = jnp.cumsum(gc, axis=2)
        diff = b[:, :, :, None, :] - b[:, :, None, :, :]
        decay = jnp.where(prefix_mask, jnp.exp(jnp.minimum(diff, 0.0)), 0.0)
        scores = jnp.einsum('bhtk,bhsk,bhtsk->bhts', qc, kc, decay)
        out = (jnp.einsum('bhtk,bhkv->bhtv', qc * jnp.exp(b), state)
               + jnp.einsum('bhts,bhsv->bhtv', scores, vc))
        b_last = b[:, :, -1:, :]
        new_state = (jnp.exp(b_last[:, :, 0, :])[..., None] * state
                     + jnp.einsum('bhsk,bhsv->bhkv', kc * jnp.exp(b_last - b), vc))
        return new_state, out

    s_fin, o = lax.scan(step, s0, (to_chunks(q), to_chunks(k), to_chunks(v), to_chunks(log_f)))
    return o.transpose(1, 2, 0, 3, 4).reshape(bsz, nh, t_len, dv), s_fin


def hgrn2_features(p, lb):
    f32 = jnp.float32
    q = to_heads(jax.nn.silu(p[..., :A_WIDTH]).astype(f32), A_HEADS)
    v = to_heads(p[..., A_WIDTH:2 * A_WIDTH].astype(f32), A_HEADS)
    out = []
    for d in range(2):
        z = p[..., (2 + d) * A_WIDTH:(3 + d) * A_WIDTH].astype(f32)
        f = jnp.maximum(lb[d] + (1.0 - lb[d]) * jax.nn.sigmoid(z), F_MIN)
        log_f = jnp.log(f)
        k = 1.0 - f
        out.append((q, to_heads(k, A_HEADS), v, to_heads(log_f, A_HEADS)))
    return out


def hgrn2_readout(o, g, onorm_w):
    bsz, _, t_len, _ = o.shape
    o = rms_norm(o.transpose(0, 2, 1, 3), onorm_w).reshape(bsz, t_len, A_WIDTH)
    return (o * jax.nn.silu(g.astype(jnp.float32))).astype(g.dtype)


def hgrn2_mixer(pl, pc, lb, onorm_w, with_ctx):
    fwd_l, bwd_l = hgrn2_features(pl, lb)
    fwd_c, bwd_c = hgrn2_features(pc, lb)
    s0 = jnp.zeros((pl.shape[0], A_HEADS, A_HEAD_DIM, A_HEAD_DIM), jnp.float32)
    oc_f, sc_f = gla_chunked(*fwd_c, s0)
    ol_f, _ = gla_chunked(*fwd_l, sc_f)
    oc_b, sc_b = gla_chunked(*[flip_t(a) for a in bwd_c], s0)
    ol_b, _ = gla_chunked(*[flip_t(a) for a in bwd_l], sc_b)
    g_sl = slice(4 * A_WIDTH, 5 * A_WIDTH)
    yl = hgrn2_readout(ol_f + flip_t(ol_b), pl[..., g_sl], onorm_w)
    yc = None
    if with_ctx:
        yc = hgrn2_readout(oc_f + flip_t(oc_b), pc[..., g_sl], onorm_w)
    return yl, yc


def chunk_mlp(u_pre, v_pre, vnorm_w, ws, bs):
    bsz, t_len, _ = u_pre.shape
    n = t_len // B_CHUNK
    u = jax.nn.gelu(u_pre)
    v = rms_norm(jax.nn.gelu(v_pre).reshape(bsz, n, B_CHUNK, B_GROUPS, B_GROUP_DIM), vnorm_w)
    sv = jnp.einsum('gts,bnsgc->bntgc', ws, v) + bs.T[:, :, None]
    return u * sv.reshape(bsz, t_len, B_WIDTH)


def even_mixer(hl, hc, w_in, lb, onorm_w, vnorm_w, ws, bs, w_out, with_ctx):
    pl = hl @ w_in
    pc = hc @ w_in
    a_sl = slice(0, 5 * A_WIDTH)
    u_sl = slice(5 * A_WIDTH, 5 * A_WIDTH + B_WIDTH)
    v_sl = slice(5 * A_WIDTH + B_WIDTH, EV_IN)
    al, ac = hgrn2_mixer(pl[..., a_sl], pc[..., a_sl], lb, onorm_w, with_ctx)
    yl = jnp.concatenate([al, chunk_mlp(pl[..., u_sl], pl[..., v_sl], vnorm_w, ws, bs)], -1) @ w_out
    yc = None
    if with_ctx:
        yc = jnp.concatenate([ac, chunk_mlp(pc[..., u_sl], pc[..., v_sl], vnorm_w, ws, bs)], -1) @ w_out
    return yl, yc


def fourier_mix(p):
    bsz, t_len, _ = p.shape
    xg = p.reshape(bsz, t_len, C_GROUPS, C_GROUP_DIM).astype(jnp.float32)
    y = jnp.fft.fft2(xg, axes=(1, 3), norm='ortho').real
    return y.reshape(bsz, t_len, C_WIDTH).astype(p.dtype)


def rope_2d(x, cos, sin):
    xr = x.reshape(*x.shape[:-1], 2, 2, MLA_ROPE // 4)
    x1, x2 = xr[..., 0, :], xr[..., 1, :]
    return jnp.stack([x1 * cos - x2 * sin, x2 * cos + x1 * sin], axis=-2).reshape(x.shape)


def mla_project(p, qa_w, w_qb, kva_w, w_kvb, qn_w, kn_w, cos, sin):
    bsz, t_len, _ = p.shape
    q_lat = rms_norm(p[..., :Q_LORA], qa_w)
    kv_lat = rms_norm(p[..., Q_LORA:Q_LORA + KV_LORA], kva_w)
    k_pe = rms_norm(p[..., Q_LORA + KV_LORA:], kn_w[MLA_NOPE:])
    q = (q_lat @ w_qb).reshape(bsz, t_len, MLA_HEADS, MLA_QK)
    kv = (kv_lat @ w_kvb).reshape(bsz, t_len, MLA_HEADS, MLA_NOPE + MLA_V_DIM)
    q_nope = rms_norm(q[..., :MLA_NOPE], qn_w[:MLA_NOPE])
    q_pe = rms_norm(q[..., MLA_NOPE:], qn_w[MLA_NOPE:])
    k_nope = rms_norm(kv[..., :MLA_NOPE], kn_w[:MLA_NOPE])
    if cos is not None:
        q_pe = rope_2d(q_pe, cos[:, None], sin[:, None])
        k_pe = rope_2d(k_pe, cos, sin)
    k_pe = jnp.broadcast_to(k_pe[:, :, None, :], (bsz, t_len, MLA_HEADS, MLA_ROPE))
    q = jnp.concatenate([q_nope, q_pe], -1)
    k = jnp.concatenate([k_nope, k_pe], -1)
    return q, k, kv[..., MLA_NOPE:]


def attend(q, k, v):
    bsz, tq, nh, dq = q.shape
    nb = tq // ATTN_BLOCK
    qb = q.reshape(bsz, nb, ATTN_BLOCK, nh, dq).transpose(1, 0, 2, 3, 4)
    scale = dq ** -0.5

    def one_block(qblk):
        s = jnp.einsum('bqhd,bkhd->bhqk', qblk, k).astype(jnp.float32) * scale
        p = jax.nn.softmax(s, axis=-1).astype(v.dtype)
        return jnp.einsum('bhqk,bkhd->bqhd', p, v)

    o = lax.map(one_block, qb)
    return o.transpose(1, 0, 2, 3, 4).reshape(bsz, tq, nh * v.shape[-1])


def odd_mixer(hl, hc, w_in, qa_w, w_qb, kva_w, w_kvb, qn_w, kn_w, w_out, cos, sin, with_ctx):
    pl = hl @ w_in
    pc = hc @ w_in
    ql, kl, vl = mla_project(pl[..., C_WIDTH:], qa_w, w_qb, kva_w, w_kvb, qn_w, kn_w, cos, sin)
    qc, kc, vc = mla_project(pc[..., C_WIDTH:], qa_w, w_qb, kva_w, w_kvb, qn_w, kn_w, None, None)
    k_all = jnp.concatenate([kc, kl], axis=1)
    v_all = jnp.concatenate([vc, vl], axis=1)
    yl = jnp.concatenate([fourier_mix(pl[..., :C_WIDTH]), attend(ql, k_all, v_all)], -1) @ w_out
    yc = None
    if with_ctx:
        yc = jnp.concatenate([fourier_mix(pc[..., :C_WIDTH]), attend(qc, kc, vc)], -1) @ w_out
    return yl, yc


def conv_ffn(h, w_up, conv_w, conv_b, w_down):
    t_len = h.shape[1]
    up = h @ w_up
    gate, val = up[..., :D_FF], up[..., D_FF:]
    pad = CONV_W // 2
    gp = jnp.pad(gate, ((0, 0), (pad, pad), (0, 0)))
    gate_c = conv_b
    for j in range(CONV_W):
        gate_c = gate_c + gp[:, j:j + t_len] * conv_w[j]
    return (jax.nn.silu(gate_c) * val) @ w_down


def setup_inputs(seed: int = 0) -> dict:
    key = jax.random.key(seed)
    ks = jax.random.split(key, 27)
    f32 = jnp.float32
    n_ev = (DEPTH + 1) // 2
    n_od = DEPTH // 2

    def normal(k, shape, scale):
        return jax.random.normal(k, shape, f32) * scale

    def gain(k, shape):
        return 1.0 + 0.02 * jax.random.normal(k, shape, f32)

    return {
        'x': normal(ks[0], (BATCH, SEQ, D_MODEL), 1.0),
        'c': normal(ks[1], (BATCH, D_MODEL), 1.0),
        'ctx': normal(ks[2], (BATCH, CTX_LEN, D_MODEL), 1.0),
        'c_ctx': normal(ks[3], (D_MODEL,), 1.0),
        'ada_w': normal(ks[4], (DEPTH, D_MODEL, 6 * D_MODEL), 0.5 * D_MODEL ** -0.5),
        'ada_b': normal(ks[5], (DEPTH, 6 * D_MODEL), 0.02),
        'norm_mix_w': gain(ks[6], (DEPTH, D_MODEL)),
        'norm_ffn_w': gain(ks[7], (DEPTH, D_MODEL)),
        'ev_w_in': normal(ks[8], (n_ev, D_MODEL, EV_IN), D_MODEL ** -0.5),
        'ev_lb_logits': normal(ks[9], (n_ev, 2, A_WIDTH), 1.0),
        'ev_onorm_w': gain(ks[10], (n_ev, A_HEAD_DIM)),
        'ev_vnorm_w': gain(ks[11], (n_ev, B_GROUPS, B_GROUP_DIM)),
        'ev_ws': normal(ks[12], (n_ev, B_GROUPS, B_CHUNK, B_CHUNK), B_CHUNK ** -0.5),
        'ev_bs': 1.0 + normal(ks[13], (n_ev, B_GROUPS, B_CHUNK), 0.1),
        'ev_w_out': normal(ks[14], (n_ev, D_MODEL, D_MODEL), D_MODEL ** -0.5),
        'od_w_in': normal(ks[15], (n_od, D_MODEL, OD_IN), D_MODEL ** -0.5),
        'od_qa_norm_w': gain(ks[16], (n_od, Q_LORA)),
        'od_w_qb': normal(ks[17], (n_od, Q_LORA, MLA_HEADS * MLA_QK), Q_LORA ** -0.5),
        'od_kva_norm_w': gain(ks[18], (n_od, KV_LORA)),
        'od_w_kvb': normal(ks[19], (n_od, KV_LORA, MLA_HEADS * (MLA_NOPE + MLA_V_DIM)), KV_LORA ** -0.5),
        'od_q_norm_w': gain(ks[20], (n_od, MLA_QK)),
        'od_k_norm_w': gain(ks[21], (n_od, MLA_QK)),
        'od_w_out': normal(ks[22], (n_od, D_MODEL, D_MODEL), D_MODEL ** -0.5),
        'ffn_w_up': normal(ks[23], (DEPTH, D_MODEL, 2 * D_FF), D_MODEL ** -0.5),
        'ffn_conv_w': normal(ks[24], (DEPTH, CONV_W, D_FF), CONV_W ** -0.5),
        'ffn_conv_b': normal(ks[25], (DEPTH, D_FF), 0.02),
        'ffn_w_down': normal(ks[26], (DEPTH, D_FF, D_MODEL), D_FF ** -0.5),
    }


def reference(x, c, ctx, c_ctx, ada_w, ada_b, norm_mix_w, norm_ffn_w, ev_w_in, ev_lb_logits,
              ev_onorm_w, ev_vnorm_w, ev_ws, ev_bs, ev_w_out, od_w_in, od_qa_norm_w, od_w_qb,
              od_kva_norm_w, od_w_kvb, od_q_norm_w, od_k_norm_w, od_w_out, ffn_w_up, ffn_conv_w,
              ffn_conv_b, ffn_w_down):
    f32 = jnp.float32
    n_lat = x.shape[1]
    ROWS = n_lat // GRID_W
    row = jnp.repeat(jnp.arange(ROWS), GRID_W)
    col = jnp.tile(jnp.arange(GRID_W), ROWS)
    r_axis = MLA_ROPE // 2
    inv_freq = ROPE_THETA ** (-jnp.arange(0, r_axis, 2, dtype=f32) / r_axis)
    ang = jnp.stack([row, col], axis=-1).astype(f32)[:, :, None] * inv_freq
    cos = jnp.cos(ang).astype(x.dtype)
    sin = jnp.sin(ang).astype(x.dtype)
    lb_p = jax.nn.softmax(ev_lb_logits.astype(f32), axis=0)
    lbs = jnp.cumsum(lb_p, axis=0) - lb_p[0]

    silu_c = jax.nn.silu(c)
    silu_cc = jax.nn.silu(c_ctx)
    xl, xc = x, ctx
    for l in range(DEPTH):
        with_ctx = l < DEPTH - 1
        mod_l = (silu_c @ ada_w[l] + ada_b[l])[:, None, :]
        mod_c = (silu_cc @ ada_w[l] + ada_b[l])[None, None, :]
        sh1, sc1, g1, sh2, sc2, g2 = jnp.split(mod_l, 6, axis=-1)
        csh1, csc1, cg1, csh2, csc2, cg2 = jnp.split(mod_c, 6, axis=-1)
        hl = modulate(xl, norm_mix_w[l], sh1, sc1)
        hc = modulate(xc, norm_mix_w[l], csh1, csc1)
        if l % 2 == 0:
            e = l // 2
            yl, yc = even_mixer(hl, hc, ev_w_in[e], lbs[e], ev_onorm_w[e], ev_vnorm_w[e],
                                ev_ws[e], ev_bs[e], ev_w_out[e], with_ctx)
        else:
            o = l // 2
            yl, yc = odd_mixer(hl, hc, od_w_in[o], od_qa_norm_w[o], od_w_qb[o], od_kva_norm_w[o],
                               od_w_kvb[o], od_q_norm_w[o], od_k_norm_w[o], od_w_out[o],
                               cos, sin, with_ctx)
        xl = xl + g1 * yl
        xl = xl + g2 * conv_ffn(modulate(xl, norm_ffn_w[l], sh2, sc2), ffn_w_up[l],
                                ffn_conv_w[l], ffn_conv_b[l], ffn_w_down[l])
        if with_ctx:
            xc = xc + cg1 * yc
            xc = xc + cg2 * conv_ffn(modulate(xc, norm_ffn_w[l], csh2, csc2), ffn_w_up[l],
                                     ffn_conv_w[l], ffn_conv_b[l], ffn_w_down[l])
    return xl
```

```python
import functools

import numpy as np
import jax
import jax.numpy as jnp
from jax import lax
from jax.experimental import pallas as pl
from jax.experimental.pallas import tpu as pltpu

F32 = jnp.float32
BF16 = jnp.bfloat16

EPS = 1e-6
F_MIN = 1e-6
GRID_W = 64
ROPE_THETA = 10000.0
HGRN_HEAD_DIM = 128
HGRN_CHUNK = 64
MLP_CHUNK = 128
MLA_NOPE = 128
MLA_ROPE = 64
MLA_V_DIM = 128
FOURIER_GROUPS = 4
CONV_W = 3

TM = 256
HALO = 8
ATTN_KC = 512
V7X_VMEM_LIMIT = 56 * 1024 * 1024


def _cparams(sem, vmem=None):
    return pltpu.CompilerParams(dimension_semantics=sem, vmem_limit_bytes=vmem)


def _dot(a, b):
    return jnp.dot(a, b, preferred_element_type=F32)


def _dot_nt(a, b):
    return lax.dot_general(a, b, (((1,), (1,)), ((), ())), preferred_element_type=F32)


def _dot_tn(a, b):
    return lax.dot_general(a, b, (((0,), (0,)), ((), ())), preferred_element_type=F32)


def _silu(x):
    return x * jax.nn.sigmoid(x)


def _gelu_tanh(x):
    return 0.5 * x * (1.0 + jnp.tanh(0.7978845608028654 * (x + 0.044715 * (x * x * x))))


def _rms(x, n=None):
    n = x.shape[-1] if n is None else n
    return x * lax.rsqrt(jnp.sum(x * x, axis=-1, keepdims=True) * (1.0 / n) + EPS)


def _modulate(x, nw, shift, scale):
    return (_rms(x) * nw) * (1.0 + scale) + shift


def _mod_spec(d):
    return pl.BlockSpec((None, 6, d), lambda b, i: (2 * b + jnp.minimum(i, 1), 0, 0))


def _const_spec(shape):
    nd = len(shape)
    return pl.BlockSpec(shape, lambda *_: (0,) * nd)


def _ada_kernel(c_ref, w_ref, b_ref, o_ref):
    s = _silu(c_ref[...]).astype(BF16)
    o_ref[...] = _dot(s, w_ref[...].astype(BF16)) + b_ref[...]


def _ada_tables(cvec, ada_w, ada_b):
    depth, d, n = ada_w.shape
    rows = cvec.shape[0]
    tn = 1536
    return pl.pallas_call(
        _ada_kernel,
        grid=(depth, n // tn),
        in_specs=[
            pl.BlockSpec((rows, d), lambda l, j: (0, 0)),
            pl.BlockSpec((None, d, tn), lambda l, j: (l, 0, j)),
            pl.BlockSpec((None, 1, tn), lambda l, j: (l, 0, j)),
        ],
        out_specs=pl.BlockSpec((None, rows, tn), lambda l, j: (l, 0, j)),
        out_shape=jax.ShapeDtypeStruct((depth, rows, n), F32),
        compiler_params=_cparams(("parallel", "parallel"), V7X_VMEM_LIMIT),
        name="ada_tables",
    )(cvec, ada_w, ada_b.reshape(depth, 1, n))


def _even_in_kernel(x_ref, mod_ref, nw_ref, w_ref, o_ref):
    h = _modulate(x_ref[...], nw_ref[...], mod_ref[0:1, :], mod_ref[1:2, :])
    o_ref[...] = _dot(h.astype(BF16), w_ref[...])


def _even_in(x, mod, nw, w_in):
    b, tt, d = x.shape
    n = w_in.shape[1]
    return pl.pallas_call(
        _even_in_kernel,
        grid=(b, tt // TM),
        in_specs=[
            pl.BlockSpec((None, TM, d), lambda b, i: (b, i, 0)),
            _mod_spec(d),
            _const_spec((1, d)),
            _const_spec((d, n)),
        ],
        out_specs=pl.BlockSpec((None, TM, n), lambda b, i: (b, i, 0)),
        out_shape=jax.ShapeDtypeStruct((b, tt, n), F32),
        compiler_params=_cparams(("parallel", "parallel"), V7X_VMEM_LIMIT),
        name="even_in",
    )(x, mod, nw, w_in)


def _hgrn_consts(reverse):
    c = HGRN_CHUNK
    idx = np.arange(c)
    if not reverse:
        tm = (idx[None, :] <= idx[:, None]).astype(np.float32)
    else:
        tm = (idx[None, :] >= idx[:, None]).astype(np.float32)
    blocks, masks = [tm], []
    for w in (32, 16, 8, 4):
        grp = idx // (2 * w)
        if not reverse:
            mid = grp * 2 * w + w - 1
            qrow = idx > mid
        else:
            mid = grp * 2 * w + w
            qrow = idx < mid
        blocks.append(np.where(qrow[:, None], tm[idx] - tm[mid], tm[mid] - tm[idx]))
        masks.append((grp[:, None] == grp[None, :]) & qrow[:, None] & (~qrow)[None, :])
    blk = idx // 4
    anchor = 4 * blk if not reverse else 4 * blk + 3
    blocks.append(tm[idx] - tm[anchor])
    blocks.append(tm[anchor] - tm[idx])
    causal = (idx[None, :] <= idx[:, None]) if not reverse else (idx[None, :] >= idx[:, None])
    masks.append((blk[:, None] == blk[None, :]) & causal)
    last = c - 1 if not reverse else 0
    blocks.append(tm[last][None, :] - tm[idx])
    return np.concatenate(blocks, 0).astype(np.float32), np.stack(masks).astype(np.float32)


def _hgrn_kernel(*refs, reverse, heads, readout):
    if readout:
        pq_ref, pv_ref, pz_ref, lb_ref, mall_ref, masks_ref, pg_ref, of_ref, onw_ref, o_ref, st_ref = refs
    else:
        pq_ref, pv_ref, pz_ref, lb_ref, mall_ref, masks_ref, o_ref, st_ref = refs
    c, hd = HGRN_CHUNK, HGRN_HEAD_DIM
    width = heads * hd

    @pl.when(pl.program_id(1) == 0)
    def _():
        st_ref[...] = jnp.zeros_like(st_ref)

    lb = lb_ref[...]
    f = jnp.maximum(lb + (1.0 - lb) * jax.nn.sigmoid(pz_ref[...]), F_MIN)
    g = jnp.log(f)
    kk = 1.0 - f
    g1 = g.astype(BF16)
    r1 = g - g1.astype(F32)
    g2 = r1.astype(BF16)
    g3 = (r1 - g2.astype(F32)).astype(BF16)
    ex3 = _dot(mall_ref[...], jnp.concatenate([g1, g2, g3], axis=1))
    e = jnp.exp(ex3[:, :width] + ex3[:, width:2 * width] + ex3[:, 2 * width:])
    q = _silu(pq_ref[...])
    v = pv_ref[...]
    outs = []
    for h in range(heads):
        sl = slice(h * hd, (h + 1) * hd)
        qh, kh, eh = q[:, sl], kk[:, sl], e[:, sl]
        a = None
        for lv in range(4):
            ew = eh[c * (lv + 1):c * (lv + 2)]
            t = _dot_nt((qh * ew).astype(BF16), (kh * ew).astype(BF16)) * masks_ref[lv]
            a = t if a is None else a + t
        a = a + _dot_nt((qh * eh[5 * c:6 * c]).astype(BF16), (kh * eh[6 * c:7 * c]).astype(BF16)) * masks_ref[4]
        vb = v[:, sl].astype(BF16)
        st = st_ref[h]
        qhat = (qh * eh[0:c]).astype(BF16)
        outs.append(_dot(a.astype(BF16), vb) + _dot_nt(qhat, st.astype(BF16)))
        khat = (kh * eh[7 * c:8 * c]).astype(BF16)
        e_last = eh[c - 1:c] if not reverse else eh[0:1]
        st_ref[h] = st * e_last + _dot_tn(vb, khat)
    o = jnp.concatenate(outs, axis=1)
    if readout:
        o = o + of_ref[...]
        gate = _silu(pg_ref[...])
        onw = onw_ref[...]
        ys = [_rms(o[:, h * hd:(h + 1) * hd]) * onw for h in range(heads)]
        o_ref[...] = (jnp.concatenate(ys, axis=1) * gate).astype(o_ref.dtype)
    else:
        o_ref[...] = o


def _hgrn_scan(p, lb, reverse, ctx_len, width, readout_args=None):
    b, tt, _ = p.shape
    c = HGRN_CHUNK
    heads = width // HGRN_HEAD_DIM
    nc, ncc = tt // c, ctx_len // c
    mall, masks = _hgrn_consts(reverse)
    mall = jnp.asarray(mall, BF16)
    masks = jnp.asarray(masks, F32)

    if not reverse:
        def chunk(j):
            return j
    else:
        def chunk(j):
            return jnp.where(j < ncc, ncc - 1 - j, nc - 1 - (j - ncc))

    def col(k):
        return pl.BlockSpec((None, c, width), lambda b, j: (b, chunk(j), k))

    in_specs = [col(0), col(1), col(3 if reverse else 2), _const_spec((1, width)),
                _const_spec(mall.shape), _const_spec(masks.shape)]
    args = [p, p, p, lb.reshape(1, width), mall, masks]
    readout = readout_args is not None
    if readout:
        o_fwd, onw = readout_args
        in_specs += [col(4), col(0), _const_spec((1, HGRN_HEAD_DIM))]
        args += [p, o_fwd, onw.reshape(1, HGRN_HEAD_DIM)]
    return pl.pallas_call(
        functools.partial(_hgrn_kernel, reverse=reverse, heads=heads, readout=readout),
        grid=(b, nc),
        in_specs=in_specs,
        out_specs=col(0),
        out_shape=jax.ShapeDtypeStruct((b, tt, width), BF16 if readout else F32),
        scratch_shapes=[pltpu.VMEM((heads, HGRN_HEAD_DIM, HGRN_HEAD_DIM), F32)],
        compiler_params=_cparams(("parallel", "arbitrary")),
        name="hgrn_bwd_readout" if readout else "hgrn_fwd",
    )(*args)


def _even_out_kernel(ya_ref, pu_ref, pv_ref, x_ref, mod_ref, vnw_ref, ws_ref, bsb_ref, wa_ref, wb_ref, o_ref, *, groups):
    gd = pu_ref.shape[1] // groups
    ms = []
    for n in range(TM // MLP_CHUNK):
        rows = slice(n * MLP_CHUNK, (n + 1) * MLP_CHUNK)
        u = _gelu_tanh(pu_ref[rows, :])
        v = _gelu_tanh(pv_ref[rows, :])
        parts = []
        for g in range(groups):
            cs = slice(g * gd, (g + 1) * gd)
            vg = (_rms(v[:, cs]) * vnw_ref[:, cs]).astype(BF16)
            sv = _dot(ws_ref[g], vg) + bsb_ref[:, cs]
            parts.append(u[:, cs] * sv)
        ms.append(jnp.concatenate(parts, axis=1))
    m = jnp.concatenate(ms, axis=0).astype(BF16)
    y = _dot(ya_ref[...], wa_ref[...]) + _dot(m, wb_ref[...])
    o_ref[...] = x_ref[...] + mod_ref[2:3, :] * y


def _even_out(ya, p, x, mod, vnw, ws, bsb, w_out, a_width):
    b, tt, d = x.shape
    bw = d - a_width
    groups = ws.shape[0]
    ucol = 5 * a_width // bw
    return pl.pallas_call(
        functools.partial(_even_out_kernel, groups=groups),
        grid=(b, tt // TM),
        in_specs=[
            pl.BlockSpec((None, TM, a_width), lambda b, i: (b, i, 0)),
            pl.BlockSpec((None, TM, bw), lambda b, i: (b, i, ucol)),
            pl.BlockSpec((None, TM, bw), lambda b, i: (b, i, ucol + 1)),
            pl.BlockSpec((None, TM, d), lambda b, i: (b, i, 0)),
            _mod_spec(d),
            _const_spec((1, bw)),
            _const_spec(ws.shape),
            _const_spec(bsb.shape),
            _const_spec((a_width, d)),
            _const_spec((bw, d)),
        ],
        out_specs=pl.BlockSpec((None, TM, d), lambda b, i: (b, i, 0)),
        out_shape=jax.ShapeDtypeStruct((b, tt, d), F32),
        compiler_params=_cparams(("parallel", "parallel"), V7X_VMEM_LIMIT),
        name="even_out",
    )(ya, p, p, x, mod, vnw, ws, bsb, w_out[:a_width], w_out[a_width:])


def _odd_in_kernel(x_ref, mod_ref, nw_ref, win_ref, qaw_ref, kvaw_ref, kpw_ref, kpsw_ref, wq_ref, qnw_ref,
                   qpw_ref, qpsw_ref, wkv_ref, knw_ref, cos_ref, sin_ref, bc_ref,
                   xcs_ref, q_ref, kt_ref, v_ref, *, heads, cw, ql, kvl, scale):
    h = _modulate(x_ref[...], nw_ref[...], mod_ref[0:1, :], mod_ref[1:2, :])
    p = _dot(h.astype(BF16), win_ref[...])
    xcs_ref[...] = _dot(p[:, :cw].astype(BF16), bc_ref[...]).astype(xcs_ref.dtype)
    cos, sin = cos_ref[...], sin_ref[...]
    nope, rope = MLA_NOPE, MLA_ROPE
    q_lat = (_rms(p[:, cw:cw + ql]) * qaw_ref[...]).astype(BF16)
    kv_lat = (_rms(p[:, cw + ql:cw + ql + kvl]) * kvaw_ref[...]).astype(BF16)
    o = cw + ql + kvl
    kp, kps = p[:, o:o + 128], p[:, o + 128:o + 256]
    kpr = lax.rsqrt(jnp.sum(kp * kp, axis=-1, keepdims=True) * (1.0 / rope) + EPS)
    k_pe = (kp * kpw_ref[...] * cos + kps * kpsw_ref[...] * sin) * kpr
    qf = _dot(q_lat, wq_ref[...])
    kvf = _dot(kv_lat, wkv_ref[...])
    hw = heads * 128
    for hh in range(heads):
        cs = slice(hh * 128, (hh + 1) * 128)
        qn = _rms(qf[:, cs]) * qnw_ref[...]
        qp = qf[:, hw + hh * 128:hw + (hh + 1) * 128]
        qps = qf[:, 2 * hw + hh * 128:2 * hw + (hh + 1) * 128]
        qpr = lax.rsqrt(jnp.sum(qp * qp, axis=-1, keepdims=True) * (1.0 / rope) + EPS)
        q_pe = (qp * qpw_ref[...] * cos + qps * qpsw_ref[...] * sin) * qpr
        q_ref[hh] = (jnp.concatenate([qn, q_pe], axis=1) * scale).astype(q_ref.dtype)
        kn = _rms(kvf[:, cs]) * knw_ref[...]
        kt_ref[hh] = jnp.concatenate([kn, k_pe], axis=1).T.astype(kt_ref.dtype)
        v_ref[hh] = kvf[:, hw + hh * 128:hw + (hh + 1) * 128].astype(v_ref.dtype)


def _odd_in(x, mod, nw, wts, heads, cw, ql, kvl):
    b, tt, d = x.shape
    scale = float(MLA_NOPE + MLA_ROPE) ** -0.5
    consts = [wts[k] for k in ("w_in", "qa_w", "kva_w", "kp_w", "kps_w", "w_q", "qn_w", "qp_w", "qps_w",
                               "w_kv", "kn_w")]
    row128 = pl.BlockSpec((TM, 128), lambda b, i: (i, 0))
    in_specs = ([pl.BlockSpec((None, TM, d), lambda b, i: (b, i, 0)), _mod_spec(d), _const_spec((1, d))]
                + [_const_spec(a.shape) for a in consts] + [row128, row128, _const_spec(wts["bc"].shape)])
    return pl.pallas_call(
        functools.partial(_odd_in_kernel, heads=heads, cw=cw, ql=ql, kvl=kvl, scale=scale),
        grid=(b, tt // TM),
        in_specs=in_specs,
        out_specs=[
            pl.BlockSpec((None, TM, 2 * cw), lambda b, i: (b, i, 0)),
            pl.BlockSpec((None, heads, TM, 256), lambda b, i: (b, 0, i, 0)),
            pl.BlockSpec((None, heads, 256, TM), lambda b, i: (b, 0, 0, i)),
            pl.BlockSpec((None, heads, TM, MLA_V_DIM), lambda b, i: (b, 0, i, 0)),
        ],
        out_shape=[
            jax.ShapeDtypeStruct((b, tt, 2 * cw), BF16),
            jax.ShapeDtypeStruct((b, heads, tt, 256), BF16),
            jax.ShapeDtypeStruct((b, heads, 256, tt), BF16),
            jax.ShapeDtypeStruct((b, heads, tt, MLA_V_DIM), BF16),
        ],
        compiler_params=_cparams(("parallel", "parallel"), V7X_VMEM_LIMIT),
        name="odd_in",
    )(x, mod, nw, *consts, wts["cos"], wts["sin"], wts["bc"])


def _attn_kernel(q_ref, kt_ref, v_ref, o_ref, *, ctx_len, n_lat_chunks):
    q = q_ref[...]

    def step(carry, start, size):
        s = _dot(q, kt_ref[:, start:start + size])
        smax = jnp.max(s, axis=-1, keepdims=True)
        if carry is None:
            m_new = smax
            p = jnp.exp(s - m_new)
            l = jnp.sum(p, axis=-1, keepdims=True)
            acc = _dot(p.astype(BF16), v_ref[start:start + size, :])
        else:
            m, l, acc = carry
            m_new = jnp.maximum(m, smax)
            alpha = jnp.exp(m - m_new)
            p = jnp.exp(s - m_new)
            l = alpha * l + jnp.sum(p, axis=-1, keepdims=True)
            acc = alpha * acc + _dot(p.astype(BF16), v_ref[start:start + size, :])
        return m_new, l, acc

    is_ctx = pl.program_id(2) == 0

    @pl.when(is_ctx)
    def _():
        _, l, acc = step(None, 0, ctx_len)
        o_ref[...] = (acc / l).astype(o_ref.dtype)

    @pl.when(jnp.logical_not(is_ctx))
    def _():
        carry = step(None, 0, ctx_len)
        for j in range(n_lat_chunks):
            carry = step(carry, ctx_len + j * ATTN_KC, ATTN_KC)
        _, l, acc = carry
        o_ref[...] = (acc / l).astype(o_ref.dtype)


def _attention(q, kt, v, ctx_len):
    b, heads, tt, dq = q.shape
    dv = v.shape[-1]
    n_lat_chunks = (tt - ctx_len) // ATTN_KC
    return pl.pallas_call(
        functools.partial(_attn_kernel, ctx_len=ctx_len, n_lat_chunks=n_lat_chunks),
        grid=(b, heads, tt // TM),
        in_specs=[
            pl.BlockSpec((None, None, TM, dq), lambda b, h, i: (b, h, i, 0)),
            pl.BlockSpec((None, None, dq, tt), lambda b, h, i: (b, h, 0, 0)),
            pl.BlockSpec((None, None, tt, dv), lambda b, h, i: (b, h, 0, 0)),
        ],
        out_specs=pl.BlockSpec((None, TM, dv), lambda b, h, i: (b, i, h)),
        out_shape=jax.ShapeDtypeStruct((b, tt, heads * dv), BF16),
        compiler_params=_cparams(("parallel", "parallel", "parallel"), V7X_VMEM_LIMIT),
        name="attention",
    )(q, kt, v)


def _dft_kernel(ct_ref, st_ref, xcs_ref, o_ref):
    half = xcs_ref.shape[1] // 2
    y = _dot(ct_ref[...], xcs_ref[:, :half]) + _dot(st_ref[...], xcs_ref[:, half:])
    o_ref[...] = y.astype(o_ref.dtype)


def _dft_positions(xcs, ct, nst):
    b, t, w2 = xcs.shape
    tr = min(t, 512)
    return pl.pallas_call(
        _dft_kernel,
        grid=(t // tr, b),
        in_specs=[
            pl.BlockSpec((tr, t), lambda i, b: (i, 0)),
            pl.BlockSpec((tr, t), lambda i, b: (i, 0)),
            pl.BlockSpec((None, t, w2), lambda i, b: (b, 0, 0)),
        ],
        out_specs=pl.BlockSpec((None, tr, w2 // 2), lambda i, b: (b, i, 0)),
        out_shape=jax.ShapeDtypeStruct((b, t, w2 // 2), BF16),
        compiler_params=_cparams(("parallel", "parallel"), V7X_VMEM_LIMIT),
        name="dft_positions",
    )(ct, nst, xcs)


def _odd_out_kernel(fc_ref, fl_ref, at_ref, x_ref, mod_ref, wf_ref, wa_ref, o_ref):
    is_ctx = pl.program_id(1) == 0
    fm = jnp.where(is_ctx, fc_ref[...], fl_ref[...])
    y = _dot(fm, wf_ref[...]) + _dot(at_ref[...], wa_ref[...])
    o_ref[...] = x_ref[...] + mod_ref[2:3, :] * y


def _odd_out(fm_ctx, fm_lat, attn, x, mod, w_out):
    b, tt, d = x.shape
    cw = fm_ctx.shape[-1]
    aw = attn.shape[-1]
    return pl.pallas_call(
        _odd_out_kernel,
        grid=(b, tt // TM),
        in_specs=[
            pl.BlockSpec((None, TM, cw), lambda b, i: (b, 0, 0)),
            pl.BlockSpec((None, TM, cw), lambda b, i: (b, jnp.maximum(i - 1, 0), 0)),
            pl.BlockSpec((None, TM, aw), lambda b, i: (b, i, 0)),
            pl.BlockSpec((None, TM, d), lambda b, i: (b, i, 0)),
            _mod_spec(d),
            _const_spec((cw, d)),
            _const_spec((aw, d)),
        ],
        out_specs=pl.BlockSpec((None, TM, d), lambda b, i: (b, i, 0)),
        out_shape=jax.ShapeDtypeStruct((b, tt, d), F32),
        compiler_params=_cparams(("parallel", "parallel"), V7X_VMEM_LIMIT),
        name="odd_out",
    )(fm_ctx, fm_lat, attn, x, mod, w_out[:cw], w_out[cw:])


def _ffn_kernel(xp_ref, x_ref, xn_ref, mod_ref, nw_ref, wg_ref, wv_ref, cw_ref, cb_ref, wd_ref, o_ref,
                *, first_tile, n_tiles):
    i = pl.program_id(1) + first_tile
    x = x_ref[...]
    xe = jnp.concatenate([xp_ref[...], x, xn_ref[...]], axis=0)
    he = _modulate(xe, nw_ref[...], mod_ref[3:4, :], mod_ref[4:5, :])
    row = lax.broadcasted_iota(jnp.int32, (TM + 2 * HALO, 1), 0)
    keep = jnp.logical_and(jnp.logical_or(row >= HALO, i > 1),
                           jnp.logical_or(row < HALO + TM, jnp.logical_and(i > 0, i < n_tiles - 1)))
    he = jnp.where(keep, he, 0.0)
    ge = _dot(he.astype(BF16), wg_ref[...])
    val = _dot(he[HALO:HALO + TM].astype(BF16), wv_ref[...])
    gc = (cb_ref[...] + ge[HALO - 1:HALO - 1 + TM] * cw_ref[0:1, :] + ge[HALO:HALO + TM] * cw_ref[1:2, :]
          + ge[HALO + 1:HALO + 1 + TM] * cw_ref[2:3, :])
    act = (_silu(gc) * val).astype(BF16)
    o_ref[...] = x + mod_ref[5:6, :] * _dot(act, wd_ref[...])


def _conv_ffn(x, mod, nw, wg, wv, conv_w, conv_b, wd, skip_ctx):
    b, tt, d = x.shape
    ff = wg.shape[1]
    n_tiles = tt // TM
    first = 1 if skip_ctx else 0
    r = TM // HALO
    nblk = tt // HALO
    return pl.pallas_call(
        functools.partial(_ffn_kernel, first_tile=first, n_tiles=n_tiles),
        grid=(b, n_tiles - first),
        in_specs=[
            pl.BlockSpec((None, HALO, d), lambda b, i: (b, jnp.maximum((i + first) * r - 1, 0), 0)),
            pl.BlockSpec((None, TM, d), lambda b, i: (b, i + first, 0)),
            pl.BlockSpec((None, HALO, d), lambda b, i: (b, jnp.minimum((i + first + 1) * r, nblk - 1), 0)),
            pl.BlockSpec((None, 6, d), lambda b, i: (2 * b + jnp.minimum(i + first, 1), 0, 0)),
            _const_spec((1, d)),
            _const_spec((d, ff)),
            _const_spec((d, ff)),
            _const_spec((CONV_W, ff)),
            _const_spec((1, ff)),
            _const_spec((ff, d)),
        ],
        out_specs=pl.BlockSpec((None, TM, d), lambda b, i: (b, i, 0)),
        out_shape=jax.ShapeDtypeStruct((b, tt - first * TM, d), F32),
        compiler_params=_cparams(("parallel", "parallel"), V7X_VMEM_LIMIT),
        name="conv_ffn",
    )(x, x, x, mod, nw, wg, wv, conv_w, conv_b, wd)


def _rope_swap_perm():
    q = MLA_ROPE // 4
    return np.concatenate([np.arange(q, 2 * q), np.arange(0, q), np.arange(3 * q, 4 * q), np.arange(2 * q, 3 * q)])


def _pad128(v):
    return jnp.pad(v, (0, 128 - v.shape[0])).reshape(1, 128)


def _odd_weights(w_in, qa_w, w_qb, kva_w, w_kvb, qn_w, kn_w, heads, cw, ql, kvl, cos_t, sin_t):
    d = w_in.shape[0]
    perm = _rope_swap_perm()
    nope, rope, qk = MLA_NOPE, MLA_ROPE, MLA_NOPE + MLA_ROPE
    o = cw + ql + kvl
    z = jnp.zeros((d, 128 - rope), w_in.dtype)
    kpe = w_in[:, o:o + rope]
    w_in_ext = jnp.concatenate([w_in[:, :o], kpe, z, kpe[:, perm], z], axis=1).astype(BF16)
    wq = w_qb.reshape(ql, heads, qk)
    zq = jnp.zeros((ql, heads, 128 - rope), w_qb.dtype)
    wq_rope = wq[:, :, nope:]
    w_q = jnp.concatenate([
        wq[:, :, :nope].reshape(ql, heads * 128),
        jnp.concatenate([wq_rope, zq], axis=2).reshape(ql, heads * 128),
        jnp.concatenate([wq_rope[:, :, perm], zq], axis=2).reshape(ql, heads * 128),
    ], axis=1).astype(BF16)
    wkv = w_kvb.reshape(kvl, heads, nope + MLA_V_DIM)
    w_kv = jnp.concatenate([wkv[:, :, :nope].reshape(kvl, heads * nope),
                            wkv[:, :, nope:].reshape(kvl, heads * MLA_V_DIM)], axis=1).astype(BF16)
    gd = cw // FOURIER_GROUPS
    jk = (np.arange(gd)[:, None] * np.arange(gd)[None, :]) % gd
    ang = 2.0 * np.pi * jk / gd
    eye = np.eye(FOURIER_GROUPS)
    bc = np.concatenate([np.kron(eye, np.cos(ang)), np.kron(eye, np.sin(ang))], axis=1) / np.sqrt(gd)
    return {
        "w_in": w_in_ext, "qa_w": qa_w.reshape(1, ql), "kva_w": kva_w.reshape(1, kvl),
        "kp_w": _pad128(kn_w[nope:]), "kps_w": _pad128(kn_w[nope:][perm]),
        "w_q": w_q, "qn_w": qn_w[:nope].reshape(1, nope),
        "qp_w": _pad128(qn_w[nope:]), "qps_w": _pad128(qn_w[nope:][perm]),
        "w_kv": w_kv, "kn_w": kn_w[:nope].reshape(1, nope),
        "cos": cos_t, "sin": sin_t, "bc": jnp.asarray(bc, BF16),
    }


def _rope_tables(t_lat, ctx_len):
    rows = t_lat // GRID_W
    row = jnp.repeat(jnp.arange(rows), GRID_W)
    col = jnp.tile(jnp.arange(GRID_W), rows)
    r_axis = MLA_ROPE // 2
    inv_freq = ROPE_THETA ** (-jnp.arange(0, r_axis, 2, dtype=F32) / r_axis)
    ang = jnp.stack([row, col], axis=-1).astype(F32)[:, :, None] * inv_freq
    cos, sin = jnp.cos(ang), jnp.sin(ang)
    cos64 = jnp.concatenate([cos[:, 0], cos[:, 0], cos[:, 1], cos[:, 1]], axis=-1)
    sin64 = jnp.concatenate([-sin[:, 0], sin[:, 0], -sin[:, 1], sin[:, 1]], axis=-1)
    cos64 = jnp.concatenate([jnp.ones((ctx_len, MLA_ROPE), F32), cos64], axis=0)
    sin64 = jnp.concatenate([jnp.zeros((ctx_len, MLA_ROPE), F32), sin64], axis=0)
    pad = ((0, 0), (0, 128 - MLA_ROPE))
    return jnp.pad(cos64, pad), jnp.pad(sin64, pad)


def _dft_tables(t):
    idx = jnp.arange(t, dtype=jnp.int32)
    jk = (idx[:, None] * idx[None, :]) % t
    ang = jk.astype(F32) * (2.0 * np.pi / t)
    s = 1.0 / np.sqrt(t)
    return (jnp.cos(ang) * s).astype(BF16), (jnp.sin(ang) * (-s)).astype(BF16)


def kernel(x, c, ctx, c_ctx, ada_w, ada_b, norm_mix_w, norm_ffn_w, ev_w_in, ev_lb_logits, ev_onorm_w, ev_vnorm_w, ev_ws, ev_bs, ev_w_out, od_w_in, od_qa_norm_w, od_w_qb, od_kva_norm_w, od_w_kvb, od_q_norm_w, od_k_norm_w, od_w_out, ffn_w_up, ffn_conv_w, ffn_conv_b, ffn_w_down):
    bsz, t_lat, d = x.shape
    ctx_len = ctx.shape[1]
    depth = ada_w.shape[0]
    assert ctx_len == TM and t_lat % ATTN_KC == 0 and t_lat % GRID_W == 0
    a_width = ev_lb_logits.shape[-1]
    ql, kvl = od_qa_norm_w.shape[-1], od_kva_norm_w.shape[-1]
    cw = od_w_in.shape[-1] - ql - kvl - MLA_ROPE
    heads = od_w_qb.shape[-1] // (MLA_NOPE + MLA_ROPE)
    d_ff = ffn_w_down.shape[1]

    pad_rows = (-(bsz + 1)) % 8
    cvec = jnp.concatenate([c, c_ctx[None, :], jnp.zeros((pad_rows, d), F32)], axis=0)
    mods = _ada_tables(cvec, ada_w, ada_b)
    mod_lat = mods[:, :bsz].reshape(depth, bsz, 1, 6, d)
    mod_ctx = jnp.broadcast_to(mods[:, bsz].reshape(depth, 1, 1, 6, d), (depth, bsz, 1, 6, d))
    mods = jnp.concatenate([mod_ctx, mod_lat], axis=2).reshape(depth, 2 * bsz, 6, d)

    lb_p = jax.nn.softmax(ev_lb_logits.astype(F32), axis=0)
    lbs = jnp.cumsum(lb_p, axis=0) - lb_p[0]
    cos_t, sin_t = _rope_tables(t_lat, ctx_len)
    dft_lat = dft_ctx = None

    xs = jnp.concatenate([ctx, x], axis=1)
    for l in range(depth):
        last = l == depth - 1
        mod = mods[l]
        nmw = norm_mix_w[l].reshape(1, d)
        if l % 2 == 0:
            e = l // 2
            p = _even_in(xs, mod, nmw, ev_w_in[e].astype(BF16))
            o_fwd = _hgrn_scan(p, lbs[e, 0], False, ctx_len, a_width)
            ya = _hgrn_scan(p, lbs[e, 1], True, ctx_len, a_width, (o_fwd, ev_onorm_w[e]))
            bw = d - a_width
            gd = bw // ev_ws.shape[1]
            bsb = jnp.repeat(ev_bs[e].T, gd, axis=1)
            xs = _even_out(ya, p, xs, mod, ev_vnorm_w[e].reshape(1, bw), ev_ws[e].astype(BF16), bsb,
                           ev_w_out[e].astype(BF16), a_width)
        else:
            o = l // 2
            if dft_lat is None:
                dft_lat, dft_ctx = _dft_tables(t_lat), _dft_tables(ctx_len)
            wts = _odd_weights(od_w_in[o], od_qa_norm_w[o], od_w_qb[o], od_kva_norm_w[o], od_w_kvb[o],
                               od_q_norm_w[o], od_k_norm_w[o], heads, cw, ql, kvl, cos_t, sin_t)
            xcs, q, kt, v = _odd_in(xs, mod, nmw, wts, heads, cw, ql, kvl)
            attn = _attention(q, kt, v, ctx_len)
            fm_ctx = _dft_positions(xcs[:, :ctx_len], *dft_ctx)
            fm_lat = _dft_positions(xcs[:, ctx_len:], *dft_lat)
            xs = _odd_out(fm_ctx, fm_lat, attn, xs, mod, od_w_out[o].astype(BF16))
        w_up = ffn_w_up[l].astype(BF16)
        xs = _conv_ffn(xs, mod, norm_ffn_w[l].reshape(1, d), w_up[:, :d_ff], w_up[:, d_ff:], ffn_conv_w[l],
                       ffn_conv_b[l].reshape(1, d_ff), ffn_w_down[l].astype(BF16), skip_ctx=last)
    return xs
```

```python
import functools

import numpy as np
import jax
import jax.numpy as jnp
from jax import lax
from jax.experimental import pallas as pl
from jax.experimental.pallas import tpu as pltpu

F32 = jnp.float32
BF16 = jnp.bfloat16

EPS = 1e-6
F_MIN = 1e-6
GRID_W = 64
ROPE_THETA = 10000.0
HGRN_HEAD_DIM = 128
HGRN_CHUNK = 64
HGRN_STEP_CHUNKS = 4
MLP_CHUNK = 128
MLA_NOPE = 128
MLA_ROPE = 64
MLA_V_DIM = 128
FOURIER_GROUPS = 4
CONV_W = 3

TM = 256
HALO = 8
ATTN_KC = 512
V7X_VMEM_LIMIT = 56 * 1024 * 1024


def _cparams(sem, vmem=None):
    return pltpu.CompilerParams(dimension_semantics=sem, vmem_limit_bytes=vmem)


def _dot(a, b):
    return jnp.dot(a, b, preferred_element_type=F32)


def _dot_nt(a, b):
    return lax.dot_general(a, b, (((1,), (1,)), ((), ())), preferred_element_type=F32)


def _dot_tn(a, b):
    return lax.dot_general(a, b, (((0,), (0,)), ((), ())), preferred_element_type=F32)


def _silu(x):
    return x * jax.nn.sigmoid(x)


def _gelu_tanh(x):
    return 0.5 * x * (1.0 + jnp.tanh(0.7978845608028654 * (x + 0.044715 * (x * x * x))))


def _rms(x, n=None):
    n = x.shape[-1] if n is None else n
    return x * lax.rsqrt(jnp.sum(x * x, axis=-1, keepdims=True) * (1.0 / n) + EPS)


def _modulate(x, nw, shift, scale):
    return (_rms(x) * nw) * (1.0 + scale) + shift


def _mod_spec(d):
    return pl.BlockSpec((None, 6, d), lambda b, i: (2 * b + jnp.minimum(i, 1), 0, 0))


def _const_spec(shape):
    nd = len(shape)
    return pl.BlockSpec(shape, lambda *_: (0,) * nd)


def _ada_kernel(c_ref, w_ref, b_ref, o_ref):
    s = _silu(c_ref[...]).astype(BF16)
    o_ref[...] = _dot(s, w_ref[...].astype(BF16)) + b_ref[...]


def _ada_tables(cvec, ada_w, ada_b):
    depth, d, n = ada_w.shape
    rows = cvec.shape[0]
    tn = 1536
    return pl.pallas_call(
        _ada_kernel,
        grid=(depth, n // tn),
        in_specs=[
            pl.BlockSpec((rows, d), lambda l, j: (0, 0)),
            pl.BlockSpec((None, d, tn), lambda l, j: (l, 0, j)),
            pl.BlockSpec((None, 1, tn), lambda l, j: (l, 0, j)),
        ],
        out_specs=pl.BlockSpec((None, rows, tn), lambda l, j: (l, 0, j)),
        out_shape=jax.ShapeDtypeStruct((depth, rows, n), F32),
        compiler_params=_cparams(("parallel", "parallel"), V7X_VMEM_LIMIT),
        name="ada_tables",
    )(cvec, ada_w, ada_b.reshape(depth, 1, n))


def _even_in_kernel(x_ref, mod_ref, nw_ref, w_ref, o_ref):
    h = _modulate(x_ref[...], nw_ref[...], mod_ref[0:1, :], mod_ref[1:2, :])
    o_ref[...] = _dot(h.astype(BF16), w_ref[...])


def _even_in(x, mod, nw, w_in):
    b, tt, d = x.shape
    n = w_in.shape[1]
    return pl.pallas_call(
        _even_in_kernel,
        grid=(b, tt // TM),
        in_specs=[
            pl.BlockSpec((None, TM, d), lambda b, i: (b, i, 0)),
            _mod_spec(d),
            _const_spec((1, d)),
            _const_spec((d, n)),
        ],
        out_specs=pl.BlockSpec((None, TM, n), lambda b, i: (b, i, 0)),
        out_shape=jax.ShapeDtypeStruct((b, tt, n), F32),
        compiler_params=_cparams(("parallel", "parallel"), V7X_VMEM_LIMIT),
        name="even_in",
    )(x, mod, nw, w_in)


def _hgrn_consts(reverse):
    c = HGRN_CHUNK
    idx = np.arange(c)
    if not reverse:
        tri = (idx[None, :] <= idx[:, None]).astype(np.float32)
    else:
        tri = (idx[None, :] >= idx[:, None]).astype(np.float32)
    masks = []
    for w in (32, 16, 8):
        grp = idx // (2 * w)
        qrow = (idx % (2 * w) >= w) if not reverse else (idx % (2 * w) < w)
        masks.append((grp[:, None] == grp[None, :]) & qrow[:, None] & (~qrow)[None, :])
    blk = idx // 8
    causal = (idx[None, :] <= idx[:, None]) if not reverse else (idx[None, :] >= idx[:, None])
    masks.append((blk[:, None] == blk[None, :]) & causal)
    return np.kron(np.eye(HGRN_STEP_CHUNKS), tri).astype(np.float32), np.stack(masks).astype(np.float32)


def _hgrn_kernel(*refs, reverse, heads, readout):
    if readout:
        pq_ref, pv_ref, pz_ref, lb_ref, tri_ref, masks_ref, pg_ref, of_ref, onw_ref, o_ref, st_ref = refs
    else:
        pq_ref, pv_ref, pz_ref, lb_ref, tri_ref, masks_ref, o_ref, st_ref = refs
    c, hd, nck = HGRN_CHUNK, HGRN_HEAD_DIM, HGRN_STEP_CHUNKS
    width = heads * hd

    @pl.when(pl.program_id(1) == 0)
    def _():
        st_ref[...] = jnp.zeros_like(st_ref)

    lb = lb_ref[...]
    f = jnp.maximum(lb + (1.0 - lb) * jax.nn.sigmoid(pz_ref[...]), F_MIN)
    g = jnp.log(f)
    kk = 1.0 - f
    g1 = g.astype(BF16)
    r1 = g - g1.astype(F32)
    g2 = r1.astype(BF16)
    g3 = (r1 - g2.astype(F32)).astype(BF16)
    b3 = _dot(tri_ref[...], jnp.concatenate([g1, g2, g3], axis=1))
    b_all = b3[:, :width] + b3[:, width:2 * width] + b3[:, 2 * width:]
    q = _silu(pq_ref[...])
    v = pv_ref[...]
    st = [st_ref[h] for h in range(heads)]
    outs = [None] * nck
    for ci in (range(nck) if not reverse else reversed(range(nck))):
        rs = slice(ci * c, (ci + 1) * c)
        b = b_all[rs]

        def ref_rows(rows, n):
            return jnp.concatenate([jnp.broadcast_to(b[r:r + 1, :], (n, width)) for r in rows], axis=0)

        e_lv = []
        for w in (32, 16, 8):
            mids = [gi * 2 * w + (w - 1 if not reverse else w) for gi in range(c // (2 * w))]
            e_lv.append(jnp.exp(-jnp.abs(b - ref_rows(mids, 2 * w))))
        anchor = ref_rows([8 * m + (3 if not reverse else 4) for m in range(c // 8)], 8)
        e_dq = jnp.exp(b - anchor)
        e_dk = jnp.exp(anchor - b)
        last = c - 1 if not reverse else 0
        e_in = jnp.exp(b)
        e_out = jnp.exp(ref_rows([last], c) - b)
        heads_out = []
        for h in range(heads):
            sl = slice(h * hd, (h + 1) * hd)
            qh, kh = q[rs, sl], kk[rs, sl]
            a = None
            for lv in range(3):
                ew = e_lv[lv][:, sl]
                t = _dot_nt((qh * ew).astype(BF16), (kh * ew).astype(BF16)) * masks_ref[lv]
                a = t if a is None else a + t
            a = a + _dot_nt((qh * e_dq[:, sl]).astype(BF16), (kh * e_dk[:, sl]).astype(BF16)) * masks_ref[3]
            vb = v[rs, sl].astype(BF16)
            qhat = (qh * e_in[:, sl]).astype(BF16)
            heads_out.append(_dot(a.astype(BF16), vb) + _dot_nt(qhat, st[h].astype(BF16)))
            khat = (kh * e_out[:, sl]).astype(BF16)
            st[h] = st[h] * e_in[last:last + 1, sl] + _dot_tn(vb, khat)
        outs[ci] = jnp.concatenate(heads_out, axis=1)
    for h in range(heads):
        st_ref[h] = st[h]
    o = jnp.concatenate(outs, axis=0)
    if readout:
        o = o + of_ref[...]
        gate = _silu(pg_ref[...])
        onw = onw_ref[...]
        ys = [_rms(o[:, h * hd:(h + 1) * hd]) * onw for h in range(heads)]
        o_ref[...] = (jnp.concatenate(ys, axis=1) * gate).astype(o_ref.dtype)
    else:
        o_ref[...] = o


def _hgrn_scan(p, lb, reverse, ctx_len, width, readout_args=None):
    b, tt, _ = p.shape
    rows = HGRN_CHUNK * HGRN_STEP_CHUNKS
    heads = width // HGRN_HEAD_DIM
    nb, nbc = tt // rows, ctx_len // rows
    tri, masks = _hgrn_consts(reverse)
    tri = jnp.asarray(tri, BF16)
    masks = jnp.asarray(masks, F32)

    if not reverse:
        def blk(j):
            return j
    else:
        def blk(j):
            return jnp.where(j < nbc, nbc - 1 - j, nb - 1 - (j - nbc))

    def col(k):
        return pl.BlockSpec((None, rows, width), lambda b, j: (b, blk(j), k))

    in_specs = [col(0), col(1), col(3 if reverse else 2), _const_spec((1, width)),
                _const_spec(tri.shape), _const_spec(masks.shape)]
    args = [p, p, p, lb.reshape(1, width), tri, masks]
    readout = readout_args is not None
    if readout:
        o_fwd, onw = readout_args
        in_specs += [col(4), col(0), _const_spec((1, HGRN_HEAD_DIM))]
        args += [p, o_fwd, onw.reshape(1, HGRN_HEAD_DIM)]
    return pl.pallas_call(
        functools.partial(_hgrn_kernel, reverse=reverse, heads=heads, readout=readout),
        grid=(b, nb),
        in_specs=in_specs,
        out_specs=col(0),
        out_shape=jax.ShapeDtypeStruct((b, tt, width), BF16 if readout else F32),
        scratch_shapes=[pltpu.VMEM((heads, HGRN_HEAD_DIM, HGRN_HEAD_DIM), F32)],
        compiler_params=_cparams(("parallel", "arbitrary")),
        name="hgrn_bwd_readout" if readout else "hgrn_fwd",
    )(*args)


def _even_out_kernel(ya_ref, pu_ref, pv_ref, x_ref, mod_ref, vnw_ref, ws_ref, bsb_ref, wa_ref, wb_ref, o_ref, *, groups):
    gd = pu_ref.shape[1] // groups
    ms = []
    for n in range(TM // MLP_CHUNK):
        rows = slice(n * MLP_CHUNK, (n + 1) * MLP_CHUNK)
        u = _gelu_tanh(pu_ref[rows, :])
        v = _gelu_tanh(pv_ref[rows, :])
        parts = []
        for g in range(groups):
            cs = slice(g * gd, (g + 1) * gd)
            vg = (_rms(v[:, cs]) * vnw_ref[:, cs]).astype(BF16)
            sv = _dot(ws_ref[g], vg) + bsb_ref[:, cs]
            parts.append(u[:, cs] * sv)
        ms.append(jnp.concatenate(parts, axis=1))
    m = jnp.concatenate(ms, axis=0).astype(BF16)
    y = _dot(ya_ref[...], wa_ref[...]) + _dot(m, wb_ref[...])
    o_ref[...] = x_ref[...] + mod_ref[2:3, :] * y


def _even_out(ya, p, x, mod, vnw, ws, bsb, w_out, a_width):
    b, tt, d = x.shape
    bw = d - a_width
    groups = ws.shape[0]
    ucol = 5 * a_width // bw
    return pl.pallas_call(
        functools.partial(_even_out_kernel, groups=groups),
        grid=(b, tt // TM),
        in_specs=[
            pl.BlockSpec((None, TM, a_width), lambda b, i: (b, i, 0)),
            pl.BlockSpec((None, TM, bw), lambda b, i: (b, i, ucol)),
            pl.BlockSpec((None, TM, bw), lambda b, i: (b, i, ucol + 1)),
            pl.BlockSpec((None, TM, d), lambda b, i: (b, i, 0)),
            _mod_spec(d),
            _const_spec((1, bw)),
            _const_spec(ws.shape),
            _const_spec(bsb.shape),
            _const_spec((a_width, d)),
            _const_spec((bw, d)),
        ],
        out_specs=pl.BlockSpec((None, TM, d), lambda b, i: (b, i, 0)),
        out_shape=jax.ShapeDtypeStruct((b, tt, d), F32),
        compiler_params=_cparams(("parallel", "parallel"), V7X_VMEM_LIMIT),
        name="even_out",
    )(ya, p, p, x, mod, vnw, ws, bsb, w_out[:a_width], w_out[a_width:])


def _odd_in_kernel(x_ref, mod_ref, nw_ref, win_ref, qaw_ref, kvaw_ref, kpw_ref, kpsw_ref, wq_ref, qnw_ref,
                   qpw_ref, qpsw_ref, wkv_ref, knw_ref, cos_ref, sin_ref, bc_ref,
                   xc_ctx_ref, xc_lat_ref, qt_ref, k_ref, vt_ref, *, heads, cw, ql, kvl, scale):
    h = _modulate(x_ref[...], nw_ref[...], mod_ref[0:1, :], mod_ref[1:2, :])
    p = _dot(h.astype(BF16), win_ref[...])
    xc = _dot(p[:, :cw].astype(BF16), bc_ref[...]).astype(xc_lat_ref.dtype)
    xc_lat_ref[...] = xc

    @pl.when(pl.program_id(1) == 0)
    def _():
        xc_ctx_ref[...] = xc
    cos, sin = cos_ref[...], sin_ref[...]
    nope, rope = MLA_NOPE, MLA_ROPE
    q_lat = (_rms(p[:, cw:cw + ql]) * qaw_ref[...]).astype(BF16)
    kv_lat = (_rms(p[:, cw + ql:cw + ql + kvl]) * kvaw_ref[...]).astype(BF16)
    o = cw + ql + kvl
    kp, kps = p[:, o:o + 128], p[:, o + 128:o + 256]
    kpr = lax.rsqrt(jnp.sum(kp * kp, axis=-1, keepdims=True) * (1.0 / rope) + EPS)
    k_pe = (kp * kpw_ref[...] * cos + kps * kpsw_ref[...] * sin) * kpr
    qf = _dot(q_lat, wq_ref[...])
    kvf = _dot(kv_lat, wkv_ref[...])
    hw = heads * 128
    for hh in range(heads):
        cs = slice(hh * 128, (hh + 1) * 128)
        qn = _rms(qf[:, cs]) * qnw_ref[...]
        qp = qf[:, hw + hh * 128:hw + (hh + 1) * 128]
        qps = qf[:, 2 * hw + hh * 128:2 * hw + (hh + 1) * 128]
        qpr = lax.rsqrt(jnp.sum(qp * qp, axis=-1, keepdims=True) * (1.0 / rope) + EPS)
        q_pe = (qp * qpw_ref[...] * cos + qps * qpsw_ref[...] * sin) * qpr
        qt_ref[hh] = (jnp.concatenate([qn, q_pe], axis=1) * scale).T.astype(qt_ref.dtype)
        kn = _rms(kvf[:, cs]) * knw_ref[...]
        k_ref[hh] = jnp.concatenate([kn, k_pe], axis=1).astype(k_ref.dtype)
        vt_ref[hh] = kvf[:, hw + hh * 128:hw + (hh + 1) * 128].T.astype(vt_ref.dtype)


def _odd_in(x, mod, nw, wts, heads, cw, ql, kvl):
    b, tt, d = x.shape
    scale = float(MLA_NOPE + MLA_ROPE) ** -0.5 * float(np.log2(np.e))
    consts = [wts[k] for k in ("w_in", "qa_w", "kva_w", "kp_w", "kps_w", "w_q", "qn_w", "qp_w", "qps_w",
                               "w_kv", "kn_w")]
    row128 = pl.BlockSpec((TM, 128), lambda b, i: (i, 0))
    in_specs = ([pl.BlockSpec((None, TM, d), lambda b, i: (b, i, 0)), _mod_spec(d), _const_spec((1, d))]
                + [_const_spec(a.shape) for a in consts] + [row128, row128, _const_spec(wts["bc"].shape)])
    return pl.pallas_call(
        functools.partial(_odd_in_kernel, heads=heads, cw=cw, ql=ql, kvl=kvl, scale=scale),
        grid=(b, tt // TM),
        in_specs=in_specs,
        out_specs=[
            pl.BlockSpec((None, TM, 2 * cw), lambda b, i: (b, 0, 0)),
            pl.BlockSpec((None, TM, 2 * cw), lambda b, i: (b, jnp.maximum(i - 1, 0), 0)),
            pl.BlockSpec((None, heads, 256, TM), lambda b, i: (b, 0, 0, i)),
            pl.BlockSpec((None, heads, TM, 256), lambda b, i: (b, 0, i, 0)),
            pl.BlockSpec((None, heads, MLA_V_DIM, TM), lambda b, i: (b, 0, 0, i)),
        ],
        out_shape=[
            jax.ShapeDtypeStruct((b, TM, 2 * cw), BF16),
            jax.ShapeDtypeStruct((b, tt - TM, 2 * cw), BF16),
            jax.ShapeDtypeStruct((b, heads, 256, tt), BF16),
            jax.ShapeDtypeStruct((b, heads, tt, 256), BF16),
            jax.ShapeDtypeStruct((b, heads, MLA_V_DIM, tt), BF16),
        ],
        compiler_params=_cparams(("parallel", "arbitrary"), V7X_VMEM_LIMIT),
        name="odd_in",
    )(x, mod, nw, *consts, wts["cos"], wts["sin"], wts["bc"])


def _attn_kernel(qt_ref, k_ref, vt_ref, o_ref, sa_ref, sb_ref, ma_ref, mb_ref, *, ctx_len, n_tiles):
    i = pl.program_id(2)
    n_all = k_ref.shape[0]
    bufs = ((sa_ref, ma_ref), (sb_ref, mb_ref))

    def chunks(n_keys):
        return [(st, min(ATTN_KC, n_keys - st)) for st in range(0, n_keys, ATTN_KC)]

    def run(step_parity, keys1, keys2):
        s_w, m_w = bufs[step_parity]
        s_r, m_r = bufs[1 - step_parity]
        c1 = chunks(keys1) if keys1 else []
        c2 = chunks(keys2) if keys2 else []
        if keys1:
            qt = qt_ref[...]
        if keys2:
            m_prev = m_r[...]
        m = l = acc = None
        for idx in range(max(len(c1), len(c2))):
            if idx < len(c1):
                st, sz = c1[idx]
                s = _dot(k_ref[st:st + sz, :], qt)
                s_w[st:st + sz, :] = s
                cm = jnp.max(s, axis=0, keepdims=True)
                m = cm if m is None else jnp.maximum(m, cm)
            if idx < len(c2):
                st, sz = c2[idx]
                p = jnp.exp2(s_r[st:st + sz, :] - m_prev)
                cl = jnp.sum(p, axis=0, keepdims=True)
                ca = _dot(vt_ref[:, st:st + sz], p.astype(BF16))
                l = cl if l is None else l + cl
                acc = ca if acc is None else acc + ca
        if keys1:
            m_w[...] = m
        if keys2:
            o_ref[...] = (acc * (1.0 / l)).T.astype(o_ref.dtype)

    @pl.when(i == 0)
    def _():
        run(0, ctx_len, None)

    @pl.when(i == 1)
    def _():
        run(1, n_all, ctx_len)

    for parity in (0, 1):
        @pl.when(jnp.logical_and(jnp.logical_and(i >= 2, i < n_tiles), i % 2 == parity))
        def _():
            run(parity, n_all, n_all)

    @pl.when(i == n_tiles)
    def _():
        run(n_tiles % 2, None, n_all)


def _attention(qt, k, vt, ctx_len):
    b, heads, tt, dq = k.shape
    dv = vt.shape[2]
    n_tiles = tt // TM
    return pl.pallas_call(
        functools.partial(_attn_kernel, ctx_len=ctx_len, n_tiles=n_tiles),
        grid=(b, heads, n_tiles + 1),
        in_specs=[
            pl.BlockSpec((None, None, dq, TM), lambda b, h, i: (b, h, 0, jnp.minimum(i, n_tiles - 1))),
            pl.BlockSpec((None, None, tt, dq), lambda b, h, i: (b, h, 0, 0)),
            pl.BlockSpec((None, None, dv, tt), lambda b, h, i: (b, h, 0, 0)),
        ],
        out_specs=pl.BlockSpec((None, TM, dv), lambda b, h, i: (b, jnp.maximum(i - 1, 0), h)),
        out_shape=jax.ShapeDtypeStruct((b, tt, heads * dv), BF16),
        scratch_shapes=[pltpu.VMEM((tt, TM), F32), pltpu.VMEM((tt, TM), F32),
                        pltpu.VMEM((1, TM), F32), pltpu.VMEM((1, TM), F32)],
        compiler_params=_cparams(("parallel", "parallel", "arbitrary"), V7X_VMEM_LIMIT),
        name="attention",
    )(qt, k, vt)


def _dft_kernel(c_ref, s_ref, x_ref, o_ref):
    half = x_ref.shape[1] // 2
    y = _dot(c_ref[...], x_ref[:, :half]) + _dot(s_ref[...], x_ref[:, half:])
    o_ref[...] = y.astype(o_ref.dtype)


def _dft_dense(xc, t_cos, t_sin):
    b, t, w2 = xc.shape
    return pl.pallas_call(
        _dft_kernel,
        grid=(b,),
        in_specs=[_const_spec((t, t)), _const_spec((t, t)), pl.BlockSpec((None, t, w2), lambda b: (b, 0, 0))],
        out_specs=pl.BlockSpec((None, t, w2 // 2), lambda b: (b, 0, 0)),
        out_shape=jax.ShapeDtypeStruct((b, t, w2 // 2), BF16),
        compiler_params=_cparams(("parallel",), V7X_VMEM_LIMIT),
        name="dft_dense",
    )(t_cos, t_sin, xc)


def _dft_split(t):
    n2 = 1 << ((t.bit_length()) // 2)
    return t // n2, n2


def _dft_stage1_kernel(x_ref, g_ref, zr_ref, zi_ref, *, n1, n2, cw):
    for t2 in range(n2):
        p = _dot(g_ref[t2], x_ref[:, t2 * 2 * cw:(t2 + 1) * 2 * cw])
        zr_ref[t2] = (p[:n1, :cw] - p[n1:, cw:]).astype(zr_ref.dtype)
        zi_ref[t2] = (p[:n1, cw:] + p[n1:, :cw]).astype(zi_ref.dtype)


def _dft_stage2_kernel(zr_ref, zi_ref, c_ref, s_ref, o_ref):
    o_ref[...] = (_dot(c_ref[...], zr_ref[...]) + _dot(s_ref[...], zi_ref[...])).astype(o_ref.dtype)


def _dft_two_stage(xc, tables):
    g, c2, s2 = tables
    b, t, w2 = xc.shape
    cw = w2 // 2
    n1, n2 = _dft_split(t)
    zr, zi = pl.pallas_call(
        functools.partial(_dft_stage1_kernel, n1=n1, n2=n2, cw=cw),
        grid=(b,),
        in_specs=[pl.BlockSpec((None, n1, n2 * w2), lambda b: (b, 0, 0)), _const_spec(g.shape)],
        out_specs=[pl.BlockSpec((None, n2, n1, cw), lambda b: (b, 0, 0, 0))] * 2,
        out_shape=[jax.ShapeDtypeStruct((b, n2, n1, cw), BF16)] * 2,
        compiler_params=_cparams(("parallel",), V7X_VMEM_LIMIT),
        name="dft_stage1",
    )(xc.reshape(b, n1, n2 * w2), g)
    y = pl.pallas_call(
        _dft_stage2_kernel,
        grid=(b,),
        in_specs=[pl.BlockSpec((None, n2, n1 * cw), lambda b: (b, 0, 0))] * 2 + [_const_spec((n2, n2))] * 2,
        out_specs=pl.BlockSpec((None, n2, n1 * cw), lambda b: (b, 0, 0)),
        out_shape=jax.ShapeDtypeStruct((b, n2, n1 * cw), BF16),
        compiler_params=_cparams(("parallel",), V7X_VMEM_LIMIT),
        name="dft_stage2",
    )(zr.reshape(b, n2, n1 * cw), zi.reshape(b, n2, n1 * cw), c2, s2)
    return y.reshape(b, t, cw)


def _odd_out_kernel(fc_ref, fl_ref, at_ref, x_ref, mod_ref, wf_ref, wa_ref, o_ref):
    is_ctx = pl.program_id(1) == 0
    fm = jnp.where(is_ctx, fc_ref[...], fl_ref[...])
    y = _dot(fm, wf_ref[...]) + _dot(at_ref[...], wa_ref[...])
    o_ref[...] = x_ref[...] + mod_ref[2:3, :] * y


def _odd_out(fm_ctx, fm_lat, attn, x, mod, w_out):
    b, tt, d = x.shape
    cw = fm_ctx.shape[-1]
    aw = attn.shape[-1]
    return pl.pallas_call(
        _odd_out_kernel,
        grid=(b, tt // TM),
        in_specs=[
            pl.BlockSpec((None, TM, cw), lambda b, i: (b, 0, 0)),
            pl.BlockSpec((None, TM, cw), lambda b, i: (b, jnp.maximum(i - 1, 0), 0)),
            pl.BlockSpec((None, TM, aw), lambda b, i: (b, i, 0)),
            pl.BlockSpec((None, TM, d), lambda b, i: (b, i, 0)),
            _mod_spec(d),
            _const_spec((cw, d)),
            _const_spec((aw, d)),
        ],
        out_specs=pl.BlockSpec((None, TM, d), lambda b, i: (b, i, 0)),
        out_shape=jax.ShapeDtypeStruct((b, tt, d), F32),
        compiler_params=_cparams(("parallel", "parallel"), V7X_VMEM_LIMIT),
        name="odd_out",
    )(fm_ctx, fm_lat, attn, x, mod, w_out[:cw], w_out[cw:])


def _ffn_kernel(xp_ref, x_ref, xn_ref, mod_ref, nw_ref, wg_ref, wv_ref, cw_ref, cb_ref, wd_ref, o_ref,
                *, first_tile, n_tiles):
    i = pl.program_id(1) + first_tile
    x = x_ref[...]
    xe = jnp.concatenate([xp_ref[...], x, xn_ref[...]], axis=0)
    he = _modulate(xe, nw_ref[...], mod_ref[3:4, :], mod_ref[4:5, :])
    row = lax.broadcasted_iota(jnp.int32, (TM + 2 * HALO, 1), 0)
    keep = jnp.logical_and(jnp.logical_or(row >= HALO, i > 1),
                           jnp.logical_or(row < HALO + TM, jnp.logical_and(i > 0, i < n_tiles - 1)))
    he = jnp.where(keep, he, 0.0)
    ge = _dot(he.astype(BF16), wg_ref[...])
    val = _dot(he[HALO:HALO + TM].astype(BF16), wv_ref[...])
    gc = (cb_ref[...] + ge[HALO - 1:HALO - 1 + TM] * cw_ref[0:1, :] + ge[HALO:HALO + TM] * cw_ref[1:2, :]
          + ge[HALO + 1:HALO + 1 + TM] * cw_ref[2:3, :])
    act = (_silu(gc) * val).astype(BF16)
    o_ref[...] = x + mod_ref[5:6, :] * _dot(act, wd_ref[...])


def _conv_ffn(x, mod, nw, wg, wv, conv_w, conv_b, wd, skip_ctx):
    b, tt, d = x.shape
    ff = wg.shape[1]
    n_tiles = tt // TM
    first = 1 if skip_ctx else 0
    r = TM // HALO
    nblk = tt // HALO
    return pl.pallas_call(
        functools.partial(_ffn_kernel, first_tile=first, n_tiles=n_tiles),
        grid=(b, n_tiles - first),
        in_specs=[
            pl.BlockSpec((None, HALO, d), lambda b, i: (b, jnp.maximum((i + first) * r - 1, 0), 0)),
            pl.BlockSpec((None, TM, d), lambda b, i: (b, i + first, 0)),
            pl.BlockSpec((None, HALO, d), lambda b, i: (b, jnp.minimum((i + first + 1) * r, nblk - 1), 0)),
            pl.BlockSpec((None, 6, d), lambda b, i: (2 * b + jnp.minimum(i + first, 1), 0, 0)),
            _const_spec((1, d)),
            _const_spec((d, ff)),
            _const_spec((d, ff)),
            _const_spec((CONV_W, ff)),
            _const_spec((1, ff)),
            _const_spec((ff, d)),
        ],
        out_specs=pl.BlockSpec((None, TM, d), lambda b, i: (b, i, 0)),
        out_shape=jax.ShapeDtypeStruct((b, tt - first * TM, d), F32),
        compiler_params=_cparams(("parallel", "parallel"), V7X_VMEM_LIMIT),
        name="conv_ffn",
    )(x, x, x, mod, nw, wg, wv, conv_w, conv_b, wd)


def _rope_swap_perm():
    q = MLA_ROPE // 4
    return np.concatenate([np.arange(q, 2 * q), np.arange(0, q), np.arange(3 * q, 4 * q), np.arange(2 * q, 3 * q)])


def _pad128(v):
    return jnp.pad(v, (0, 128 - v.shape[0])).reshape(1, 128)


def _odd_weights(w_in, qa_w, w_qb, kva_w, w_kvb, qn_w, kn_w, heads, cw, ql, kvl, cos_t, sin_t):
    d = w_in.shape[0]
    perm = _rope_swap_perm()
    nope, rope, qk = MLA_NOPE, MLA_ROPE, MLA_NOPE + MLA_ROPE
    o = cw + ql + kvl
    z = jnp.zeros((d, 128 - rope), w_in.dtype)
    kpe = w_in[:, o:o + rope]
    w_in_ext = jnp.concatenate([w_in[:, :o], kpe, z, kpe[:, perm], z], axis=1).astype(BF16)
    wq = w_qb.reshape(ql, heads, qk)
    zq = jnp.zeros((ql, heads, 128 - rope), w_qb.dtype)
    wq_rope = wq[:, :, nope:]
    w_q = jnp.concatenate([
        wq[:, :, :nope].reshape(ql, heads * 128),
        jnp.concatenate([wq_rope, zq], axis=2).reshape(ql, heads * 128),
        jnp.concatenate([wq_rope[:, :, perm], zq], axis=2).reshape(ql, heads * 128),
    ], axis=1).astype(BF16)
    wkv = w_kvb.reshape(kvl, heads, nope + MLA_V_DIM)
    w_kv = jnp.concatenate([wkv[:, :, :nope].reshape(kvl, heads * nope),
                            wkv[:, :, nope:].reshape(kvl, heads * MLA_V_DIM)], axis=1).astype(BF16)
    gd = cw // FOURIER_GROUPS
    jk = (np.arange(gd)[:, None] * np.arange(gd)[None, :]) % gd
    ang = 2.0 * np.pi * jk / gd
    eye = np.eye(FOURIER_GROUPS)
    bc = np.concatenate([np.kron(eye, np.cos(ang)), np.kron(eye, -np.sin(ang))], axis=1) / np.sqrt(gd)
    return {
        "w_in": w_in_ext, "qa_w": qa_w.reshape(1, ql), "kva_w": kva_w.reshape(1, kvl),
        "kp_w": _pad128(kn_w[nope:]), "kps_w": _pad128(kn_w[nope:][perm]),
        "w_q": w_q, "qn_w": qn_w[:nope].reshape(1, nope),
        "qp_w": _pad128(qn_w[nope:]), "qps_w": _pad128(qn_w[nope:][perm]),
        "w_kv": w_kv, "kn_w": kn_w[:nope].reshape(1, nope),
        "cos": cos_t, "sin": sin_t, "bc": jnp.asarray(bc, F32).astype(BF16),
    }


def _rope_tables(t_lat, ctx_len):
    rows = t_lat // GRID_W
    row = jnp.repeat(jnp.arange(rows), GRID_W)
    col = jnp.tile(jnp.arange(GRID_W), rows)
    r_axis = MLA_ROPE // 2
    inv_freq = ROPE_THETA ** (-jnp.arange(0, r_axis, 2, dtype=F32) / r_axis)
    ang = jnp.stack([row, col], axis=-1).astype(F32)[:, :, None] * inv_freq
    cos, sin = jnp.cos(ang), jnp.sin(ang)
    cos64 = jnp.concatenate([cos[:, 0], cos[:, 0], cos[:, 1], cos[:, 1]], axis=-1)
    sin64 = jnp.concatenate([-sin[:, 0], sin[:, 0], -sin[:, 1], sin[:, 1]], axis=-1)
    cos64 = jnp.concatenate([jnp.ones((ctx_len, MLA_ROPE), F32), cos64], axis=0)
    sin64 = jnp.concatenate([jnp.zeros((ctx_len, MLA_ROPE), F32), sin64], axis=0)
    pad = ((0, 0), (0, 128 - MLA_ROPE))
    return jnp.pad(cos64, pad), jnp.pad(sin64, pad)


def _cos_sin(phase, period, scale):
    ang = (phase % period).astype(F32) * (2.0 * np.pi / period)
    return jnp.cos(ang) * scale, jnp.sin(ang) * scale


def _dft_dense_tables(t):
    idx = jnp.arange(t, dtype=jnp.int32)
    c, s = _cos_sin(idx[:, None] * idx[None, :], t, 1.0 / np.sqrt(t))
    return c.astype(BF16), s.astype(BF16)


def _dft_two_stage_tables(t):
    n1, n2 = _dft_split(t)
    u1 = jnp.arange(n1, dtype=jnp.int32)
    pos = jnp.arange(t, dtype=jnp.int32).reshape(n1, n2)
    c, s = _cos_sin(pos.T[:, None, :] * u1[None, :, None], t, 1.0 / np.sqrt(n1))
    g = jnp.concatenate([c, -s], axis=1).astype(BF16)
    i2 = jnp.arange(n2, dtype=jnp.int32)
    c2, s2 = _cos_sin(i2[:, None] * i2[None, :], n2, 1.0 / np.sqrt(n2))
    return g, c2.astype(BF16), s2.astype(BF16)


def kernel(x, c, ctx, c_ctx, ada_w, ada_b, norm_mix_w, norm_ffn_w, ev_w_in, ev_lb_logits, ev_onorm_w, ev_vnorm_w, ev_ws, ev_bs, ev_w_out, od_w_in, od_qa_norm_w, od_w_qb, od_kva_norm_w, od_w_kvb, od_q_norm_w, od_k_norm_w, od_w_out, ffn_w_up, ffn_conv_w, ffn_conv_b, ffn_w_down):
    bsz, t_lat, d = x.shape
    ctx_len = ctx.shape[1]
    depth = ada_w.shape[0]
    assert ctx_len == TM and t_lat % ATTN_KC == 0 and t_lat % GRID_W == 0
    a_width = ev_lb_logits.shape[-1]
    ql, kvl = od_qa_norm_w.shape[-1], od_kva_norm_w.shape[-1]
    cw = od_w_in.shape[-1] - ql - kvl - MLA_ROPE
    heads = od_w_qb.shape[-1] // (MLA_NOPE + MLA_ROPE)
    d_ff = ffn_w_down.shape[1]

    pad_rows = (-(bsz + 1)) % 8
    cvec = jnp.concatenate([c, c_ctx[None, :], jnp.zeros((pad_rows, d), F32)], axis=0)
    mods = _ada_tables(cvec, ada_w, ada_b)
    mod_lat = mods[:, :bsz].reshape(depth, bsz, 1, 6, d)
    mod_ctx = jnp.broadcast_to(mods[:, bsz].reshape(depth, 1, 1, 6, d), (depth, bsz, 1, 6, d))
    mods = jnp.concatenate([mod_ctx, mod_lat], axis=2).reshape(depth, 2 * bsz, 6, d)

    lb_p = jax.nn.softmax(ev_lb_logits.astype(F32), axis=0)
    lbs = jnp.cumsum(lb_p, axis=0) - lb_p[0]
    cos_t, sin_t = _rope_tables(t_lat, ctx_len)
    dft_lat = dft_ctx = None

    xs = jnp.concatenate([ctx, x], axis=1)
    for l in range(depth):
        last = l == depth - 1
        mod = mods[l]
        nmw = norm_mix_w[l].reshape(1, d)
        if l % 2 == 0:
            e = l // 2
            p = _even_in(xs, mod, nmw, ev_w_in[e].astype(BF16))
            o_fwd = _hgrn_scan(p, lbs[e, 0], False, ctx_len, a_width)
            ya = _hgrn_scan(p, lbs[e, 1], True, ctx_len, a_width, (o_fwd, ev_onorm_w[e]))
            bw = d - a_width
            gd = bw // ev_ws.shape[1]
            bsb = jnp.repeat(ev_bs[e].T, gd, axis=1)
            xs = _even_out(ya, p, xs, mod, ev_vnorm_w[e].reshape(1, bw), ev_ws[e].astype(BF16), bsb,
                           ev_w_out[e].astype(BF16), a_width)
        else:
            o = l // 2
            if dft_lat is None:
                dft_lat, dft_ctx = _dft_two_stage_tables(t_lat), _dft_dense_tables(ctx_len)
            wts = _odd_weights(od_w_in[o], od_qa_norm_w[o], od_w_qb[o], od_kva_norm_w[o], od_w_kvb[o],
                               od_q_norm_w[o], od_k_norm_w[o], heads, cw, ql, kvl, cos_t, sin_t)
            xc_ctx, xc_lat, qt, k, vt = _odd_in(xs, mod, nmw, wts, heads, cw, ql, kvl)
            attn = _attention(qt, k, vt, ctx_len)
            fm_ctx = _dft_dense(xc_ctx, *dft_ctx)
            fm_lat = _dft_two_stage(xc_lat, dft_lat)
            xs = _odd_out(fm_ctx, fm_lat, attn, xs, mod, od_w_out[o].astype(BF16))
        w_up = ffn_w_up[l].astype(BF16)
        xs = _conv_ffn(xs, mod, norm_ffn_w[l].reshape(1, d), w_up[:, :d_ff], w_up[:, d_ff:], ffn_conv_w[l],
                       ffn_conv_b[l].reshape(1, d_ff), ffn_w_down[l].astype(BF16), skip_ctx=last)
    return xs
```

```python
import functools

import numpy as np
import jax
import jax.numpy as jnp
from jax import lax
from jax.experimental import pallas as pl
from jax.experimental.pallas import tpu as pltpu

F32 = jnp.float32
BF16 = jnp.bfloat16

EPS = 1e-6
F_MIN = 1e-6
GRID_W = 64
ROPE_THETA = 10000.0
HGRN_HEAD_DIM = 128
HGRN_CHUNK = 64
HGRN_STEP_CHUNKS = 4
MLP_CHUNK = 128
MLA_NOPE = 128
MLA_ROPE = 64
MLA_V_DIM = 128
FOURIER_GROUPS = 4
CONV_W = 3

TM = 256
NB = 2
HALO = 8
ATTN_KC = 512
V7X_VMEM_LIMIT = 56 * 1024 * 1024


def _cparams(sem, vmem=None):
    return pltpu.CompilerParams(dimension_semantics=sem, vmem_limit_bytes=vmem)


def _dot(a, b):
    return jnp.dot(a, b, preferred_element_type=F32)


def _dot_nt(a, b):
    return lax.dot_general(a, b, (((1,), (1,)), ((), ())), preferred_element_type=F32)


def _dot_tn(a, b):
    return lax.dot_general(a, b, (((0,), (0,)), ((), ())), preferred_element_type=F32)


def _silu(x):
    return x * jax.nn.sigmoid(x)


def _gelu_tanh(x):
    return 0.5 * x * (1.0 + jnp.tanh(0.7978845608028654 * (x + 0.044715 * (x * x * x))))


def _rms(x, n=None):
    n = x.shape[-1] if n is None else n
    return x * lax.rsqrt(jnp.sum(x * x, axis=-1, keepdims=True) * (1.0 / n) + EPS)


def _modulate(x, nw, shift, scale):
    return (_rms(x) * nw) * (1.0 + scale) + shift


def _mod_spec(d, nb):
    return pl.BlockSpec((None, nb, 6, d), lambda b, i: (jnp.minimum(i, 1), b, 0, 0))


def _rows_spec(cols, colblk=0):
    return pl.BlockSpec((NB, TM, cols), lambda b, i: (b, i, colblk))


def _const_spec(shape):
    nd = len(shape)
    return pl.BlockSpec(shape, lambda *_: (0,) * nd, pipeline_mode=pl.Buffered(1))


def _ada_kernel(c_ref, w_ref, b_ref, o_ref):
    s = _silu(c_ref[...]).astype(BF16)
    o_ref[...] = _dot(s, w_ref[...].astype(BF16)) + b_ref[...]


def _ada_tables(cvec, ada_w, ada_b):
    depth, d, n = ada_w.shape
    rows = cvec.shape[0]
    tn = 1536
    return pl.pallas_call(
        _ada_kernel,
        grid=(depth, n // tn),
        in_specs=[
            pl.BlockSpec((rows, d), lambda l, j: (0, 0)),
            pl.BlockSpec((None, d, tn), lambda l, j: (l, 0, j)),
            pl.BlockSpec((None, 1, tn), lambda l, j: (l, 0, j)),
        ],
        out_specs=pl.BlockSpec((None, rows, tn), lambda l, j: (l, 0, j)),
        out_shape=jax.ShapeDtypeStruct((depth, rows, n), F32),
        compiler_params=_cparams(("parallel", "parallel"), V7X_VMEM_LIMIT),
        name="ada_tables",
    )(cvec, ada_w, ada_b.reshape(depth, 1, n))


def _stream_specs(xs, d):
    if isinstance(xs, tuple):
        return ([pl.BlockSpec((NB, TM, d), lambda b, i: (b, 0, 0)),
                 pl.BlockSpec((NB, TM, d), lambda b, i: (b, jnp.maximum(i - 1, 0), 0))], list(xs))
    return [_rows_spec(d)], [xs]


def _stream_rows(x_refs, s):
    if len(x_refs) == 1:
        return x_refs[0][s]
    return jnp.where(pl.program_id(1) == 0, x_refs[0][s], x_refs[1][s])


def _even_in_kernel(*refs, n_x):
    x_refs = refs[:n_x]
    mod_ref, nw_ref, w_ref, o_ref = refs[n_x:]
    nw = nw_ref[...]
    h = jnp.concatenate([_modulate(_stream_rows(x_refs, s), nw, mod_ref[s, 0:1, :], mod_ref[s, 1:2, :]).astype(BF16)
                         for s in range(NB)], axis=0)
    p = _dot(h, w_ref[...])
    for s in range(NB):
        o_ref[s] = p[s * TM:(s + 1) * TM]


def _even_in(xs, tt, mod, nw, w_in):
    d, n = w_in.shape
    x_specs, x_args = _stream_specs(xs, d)
    b = x_args[0].shape[0]
    return pl.pallas_call(
        functools.partial(_even_in_kernel, n_x=len(x_args)),
        grid=(b // NB, tt // TM),
        in_specs=x_specs + [
            _mod_spec(d, NB),
            _const_spec((1, d)),
            _const_spec((d, n)),
        ],
        out_specs=_rows_spec(n),
        out_shape=jax.ShapeDtypeStruct((b, tt, n), F32),
        compiler_params=_cparams(("parallel", "parallel"), V7X_VMEM_LIMIT),
        name="even_in",
    )(*x_args, mod, nw, w_in)


def _hgrn_consts(reverse):
    c = HGRN_CHUNK
    idx = np.arange(c)
    if not reverse:
        tri = (idx[None, :] <= idx[:, None]).astype(np.float32)
    else:
        tri = (idx[None, :] >= idx[:, None]).astype(np.float32)
    masks = []
    for w in (32, 16, 8):
        grp = idx // (2 * w)
        qrow = (idx % (2 * w) >= w) if not reverse else (idx % (2 * w) < w)
        masks.append((grp[:, None] == grp[None, :]) & qrow[:, None] & (~qrow)[None, :])
    blk = idx // 8
    causal = (idx[None, :] <= idx[:, None]) if not reverse else (idx[None, :] >= idx[:, None])
    masks.append((blk[:, None] == blk[None, :]) & causal)
    return np.kron(np.eye(HGRN_STEP_CHUNKS), tri).astype(np.float32), np.stack(masks).astype(np.float32)


def _hgrn_kernel(*refs, reverse, heads, readout):
    if readout:
        pq_ref, pv_ref, pz_ref, lb_ref, tri_ref, masks_ref, pg_ref, of_ref, onw_ref, o_ref, st_ref = refs
    else:
        pq_ref, pv_ref, pz_ref, lb_ref, tri_ref, masks_ref, o_ref, st_ref = refs
    c, hd, nck = HGRN_CHUNK, HGRN_HEAD_DIM, HGRN_STEP_CHUNKS
    width = heads * hd

    @pl.when(pl.program_id(1) == 0)
    def _():
        st_ref[...] = jnp.zeros_like(st_ref)

    lb = lb_ref[...]
    f = jnp.maximum(lb + (1.0 - lb) * jax.nn.sigmoid(pz_ref[...]), F_MIN)
    g = jnp.log(f)
    kk = 1.0 - f
    g1 = g.astype(BF16)
    r1 = g - g1.astype(F32)
    g2 = r1.astype(BF16)
    g3 = (r1 - g2.astype(F32)).astype(BF16)
    b3 = _dot(tri_ref[...], jnp.concatenate([g1, g2, g3], axis=1))
    b_all = b3[:, :width] + b3[:, width:2 * width] + b3[:, 2 * width:]
    q = _silu(pq_ref[...])
    v = pv_ref[...]
    st = [st_ref[h] for h in range(heads)]
    outs = [None] * nck
    for ci in (range(nck) if not reverse else reversed(range(nck))):
        rs = slice(ci * c, (ci + 1) * c)
        b = b_all[rs]

        def ref_rows(rows, n):
            return jnp.concatenate([jnp.broadcast_to(b[r:r + 1, :], (n, width)) for r in rows], axis=0)

        e_lv = []
        for w in (32, 16, 8):
            mids = [gi * 2 * w + (w - 1 if not reverse else w) for gi in range(c // (2 * w))]
            e_lv.append(jnp.exp(-jnp.abs(b - ref_rows(mids, 2 * w))))
        anchor = ref_rows([8 * m + (3 if not reverse else 4) for m in range(c // 8)], 8)
        e_dq = jnp.exp(b - anchor)
        e_dk = jnp.exp(anchor - b)
        last = c - 1 if not reverse else 0
        e_in = jnp.exp(b)
        e_out = jnp.exp(ref_rows([last], c) - b)
        heads_out = []
        for h in range(heads):
            sl = slice(h * hd, (h + 1) * hd)
            qh, kh = q[rs, sl], kk[rs, sl]
            a = None
            for lv in range(3):
                ew = e_lv[lv][:, sl]
                t = _dot_nt((qh * ew).astype(BF16), (kh * ew).astype(BF16)) * masks_ref[lv]
                a = t if a is None else a + t
            d8 = _dot_nt((qh * e_dq[:, sl]).astype(BF16), (kh * e_dk[:, sl]).astype(BF16))
            a = a + jnp.where(masks_ref[3] > 0.5, d8, 0.0)
            vb = v[rs, sl].astype(BF16)
            qhat = (qh * e_in[:, sl]).astype(BF16)
            heads_out.append(_dot(a.astype(BF16), vb) + _dot_nt(qhat, st[h].astype(BF16)))
            khat = (kh * e_out[:, sl]).astype(BF16)
            st[h] = st[h] * e_in[last:last + 1, sl] + _dot_tn(vb, khat)
        outs[ci] = jnp.concatenate(heads_out, axis=1)
    for h in range(heads):
        st_ref[h] = st[h]
    o = jnp.concatenate(outs, axis=0)
    if readout:
        o = o + of_ref[...]
        gate = _silu(pg_ref[...])
        onw = onw_ref[...]
        ys = [_rms(o[:, h * hd:(h + 1) * hd]) * onw for h in range(heads)]
        o_ref[...] = (jnp.concatenate(ys, axis=1) * gate).astype(o_ref.dtype)
    else:
        o_ref[...] = o


def _hgrn_scan(p, lb, reverse, ctx_len, width, readout_args=None):
    b, tt, _ = p.shape
    rows = HGRN_CHUNK * HGRN_STEP_CHUNKS
    heads = width // HGRN_HEAD_DIM
    nb, nbc = tt // rows, ctx_len // rows
    tri, masks = _hgrn_consts(reverse)
    tri = jnp.asarray(tri, BF16)
    masks = jnp.asarray(masks, F32)

    if not reverse:
        def blk(j):
            return j
    else:
        def blk(j):
            return jnp.where(j < nbc, nbc - 1 - j, nb - 1 - (j - nbc))

    def col(k):
        return pl.BlockSpec((None, rows, width), lambda b, j: (b, blk(j), k))

    in_specs = [col(0), col(1), col(3 if reverse else 2), _const_spec((1, width)),
                _const_spec(tri.shape), _const_spec(masks.shape)]
    args = [p, p, p, lb.reshape(1, width), tri, masks]
    readout = readout_args is not None
    if readout:
        o_fwd, onw = readout_args
        in_specs += [col(4), col(0), _const_spec((1, HGRN_HEAD_DIM))]
        args += [p, o_fwd, onw.reshape(1, HGRN_HEAD_DIM)]
    return pl.pallas_call(
        functools.partial(_hgrn_kernel, reverse=reverse, heads=heads, readout=readout),
        grid=(b, nb),
        in_specs=in_specs,
        out_specs=col(0),
        out_shape=jax.ShapeDtypeStruct((b, tt, width), BF16 if readout else F32),
        scratch_shapes=[pltpu.VMEM((heads, HGRN_HEAD_DIM, HGRN_HEAD_DIM), F32)],
        compiler_params=_cparams(("parallel", "arbitrary")),
        name="hgrn_bwd_readout" if readout else "hgrn_fwd",
    )(*args)


def _even_out_kernel(*refs, groups, n_x):
    ya_ref, pu_ref, pv_ref = refs[:3]
    x_refs = refs[3:3 + n_x]
    mod_ref, vnw_ref, ws_ref, bsb_ref, wa_ref, wb_ref, o_ref = refs[3 + n_x:]
    gd = pu_ref.shape[2] // groups
    ms = []
    for s in range(NB):
        for n in range(TM // MLP_CHUNK):
            rows = slice(n * MLP_CHUNK, (n + 1) * MLP_CHUNK)
            u = _gelu_tanh(pu_ref[s, rows, :])
            v = _gelu_tanh(pv_ref[s, rows, :])
            parts = []
            for g in range(groups):
                cs = slice(g * gd, (g + 1) * gd)
                vg = (_rms(v[:, cs]) * vnw_ref[:, cs]).astype(BF16)
                sv = _dot(ws_ref[g], vg) + bsb_ref[:, cs]
                parts.append(u[:, cs] * sv)
            ms.append(jnp.concatenate(parts, axis=1).astype(BF16))
    m = jnp.concatenate(ms, axis=0)
    ya = jnp.concatenate([ya_ref[s] for s in range(NB)], axis=0)
    y = _dot(ya, wa_ref[...]) + _dot(m, wb_ref[...])
    for s in range(NB):
        o_ref[s] = _stream_rows(x_refs, s) + mod_ref[s, 2:3, :] * y[s * TM:(s + 1) * TM]


def _even_out(ya, p, xs, mod, vnw, ws, bsb, w_out, a_width):
    b, tt, _ = p.shape
    d = w_out.shape[1]
    bw = d - a_width
    groups = ws.shape[0]
    ucol = 5 * a_width // bw
    x_specs, x_args = _stream_specs(xs, d)
    return pl.pallas_call(
        functools.partial(_even_out_kernel, groups=groups, n_x=len(x_args)),
        grid=(b // NB, tt // TM),
        in_specs=[
            _rows_spec(a_width),
            _rows_spec(bw, ucol),
            _rows_spec(bw, ucol + 1)] + x_specs + [
            _mod_spec(d, NB),
            _const_spec((1, bw)),
            _const_spec(ws.shape),
            _const_spec(bsb.shape),
            _const_spec((a_width, d)),
            _const_spec((bw, d)),
        ],
        out_specs=_rows_spec(d),
        out_shape=jax.ShapeDtypeStruct((b, tt, d), F32),
        compiler_params=_cparams(("parallel", "parallel"), V7X_VMEM_LIMIT),
        name="even_out",
    )(ya, p, p, *x_args, mod, vnw, ws, bsb, w_out[:a_width], w_out[a_width:])


def _odd_in_kernel(x_ref, mod_ref, nw_ref, win_ref, qaw_ref, kvaw_ref, kpw_ref, kpsw_ref, wq_ref, qnw_ref,
                   qpw_ref, qpsw_ref, wkv_ref, knw_ref, cos_ref, sin_ref, bc_ref,
                   xc_ctx_ref, xc_lat_ref, qt_ref, k_ref, vt_ref, *, heads, cw, ql, kvl, scale):
    h = _modulate(x_ref[...], nw_ref[...], mod_ref[0, 0:1, :], mod_ref[0, 1:2, :])
    p = _dot(h.astype(BF16), win_ref[...])
    xc = _dot(p[:, :cw].astype(BF16), bc_ref[...]).astype(xc_lat_ref.dtype)
    xc_lat_ref[...] = xc

    @pl.when(pl.program_id(1) == 0)
    def _():
        xc_ctx_ref[...] = xc
    cos, sin = cos_ref[...], sin_ref[...]
    nope, rope = MLA_NOPE, MLA_ROPE
    q_lat = (_rms(p[:, cw:cw + ql]) * qaw_ref[...]).astype(BF16)
    kv_lat = (_rms(p[:, cw + ql:cw + ql + kvl]) * kvaw_ref[...]).astype(BF16)
    o = cw + ql + kvl
    kp, kps = p[:, o:o + 128], p[:, o + 128:o + 256]
    kpr = lax.rsqrt(jnp.sum(kp * kp, axis=-1, keepdims=True) * (1.0 / rope) + EPS)
    k_pe = (kp * kpw_ref[...] * cos + kps * kpsw_ref[...] * sin) * kpr
    qf = _dot(q_lat, wq_ref[...])
    kvf = _dot(kv_lat, wkv_ref[...])
    hw = heads * 128
    for hh in range(heads):
        cs = slice(hh * 128, (hh + 1) * 128)
        qn = _rms(qf[:, cs]) * qnw_ref[...]
        qp = qf[:, hw + hh * 128:hw + (hh + 1) * 128]
        qps = qf[:, 2 * hw + hh * 128:2 * hw + (hh + 1) * 128]
        qpr = lax.rsqrt(jnp.sum(qp * qp, axis=-1, keepdims=True) * (1.0 / rope) + EPS)
        q_pe = (qp * qpw_ref[...] * cos + qps * qpsw_ref[...] * sin) * qpr
        qt_ref[hh] = (jnp.concatenate([qn, q_pe], axis=1) * scale).T.astype(qt_ref.dtype)
        kn = _rms(kvf[:, cs]) * knw_ref[...]
        k_ref[hh] = jnp.concatenate([kn, k_pe], axis=1).astype(k_ref.dtype)
        vt_ref[hh] = kvf[:, hw + hh * 128:hw + (hh + 1) * 128].T.astype(vt_ref.dtype)


def _odd_in(x, mod, nw, wts, heads, cw, ql, kvl):
    b, tt, d = x.shape
    scale = float(MLA_NOPE + MLA_ROPE) ** -0.5 * float(np.log2(np.e))
    consts = [wts[k] for k in ("w_in", "qa_w", "kva_w", "kp_w", "kps_w", "w_q", "qn_w", "qp_w", "qps_w",
                               "w_kv", "kn_w")]
    row128 = pl.BlockSpec((TM, 128), lambda b, i: (i, 0))
    in_specs = ([pl.BlockSpec((None, TM, d), lambda b, i: (b, i, 0)), _mod_spec(d, 1), _const_spec((1, d))]
                + [_const_spec(a.shape) for a in consts] + [row128, row128, _const_spec(wts["bc"].shape)])
    return pl.pallas_call(
        functools.partial(_odd_in_kernel, heads=heads, cw=cw, ql=ql, kvl=kvl, scale=scale),
        grid=(b, tt // TM),
        in_specs=in_specs,
        out_specs=[
            pl.BlockSpec((None, TM, 2 * cw), lambda b, i: (b, 0, 0)),
            pl.BlockSpec((None, TM, 2 * cw), lambda b, i: (b, jnp.maximum(i - 1, 0), 0)),
            pl.BlockSpec((None, heads, 256, TM), lambda b, i: (b, 0, 0, i)),
            pl.BlockSpec((None, heads, TM, 256), lambda b, i: (b, 0, i, 0)),
            pl.BlockSpec((None, heads, MLA_V_DIM, TM), lambda b, i: (b, 0, 0, i)),
        ],
        out_shape=[
            jax.ShapeDtypeStruct((b, TM, 2 * cw), BF16),
            jax.ShapeDtypeStruct((b, tt - TM, 2 * cw), BF16),
            jax.ShapeDtypeStruct((b, heads, 256, tt), BF16),
            jax.ShapeDtypeStruct((b, heads, tt, 256), BF16),
            jax.ShapeDtypeStruct((b, heads, MLA_V_DIM, tt), BF16),
        ],
        compiler_params=_cparams(("parallel", "arbitrary"), V7X_VMEM_LIMIT),
        name="odd_in",
    )(x, mod, nw, *consts, wts["cos"], wts["sin"], wts["bc"])


def _attn_kernel(qt_ref, k_ref, vt_ref, o_ref, sa_ref, sb_ref, ma_ref, mb_ref, *, ctx_len, n_tiles):
    i = pl.program_id(2)
    n_all = k_ref.shape[0]
    bufs = ((sa_ref, ma_ref), (sb_ref, mb_ref))

    def chunks(n_keys):
        return [(st, min(ATTN_KC, n_keys - st)) for st in range(0, n_keys, ATTN_KC)]

    def run(step_parity, keys1, keys2):
        s_w, m_w = bufs[step_parity]
        s_r, m_r = bufs[1 - step_parity]
        c1 = chunks(keys1) if keys1 else []
        c2 = chunks(keys2) if keys2 else []
        if keys1:
            qt = qt_ref[...]
        if keys2:
            m_prev = m_r[...]
        m = l = acc = None
        for idx in range(max(len(c1), len(c2))):
            if idx < len(c1):
                st, sz = c1[idx]
                s = _dot(k_ref[st:st + sz, :], qt)
                s_w[st:st + sz, :] = s
                cm = jnp.max(s, axis=0, keepdims=True)
                m = cm if m is None else jnp.maximum(m, cm)
            if idx < len(c2):
                st, sz = c2[idx]
                p = jnp.exp2(s_r[st:st + sz, :] - m_prev)
                cl = jnp.sum(p, axis=0, keepdims=True)
                ca = _dot(vt_ref[:, st:st + sz], p.astype(BF16))
                l = cl if l is None else l + cl
                acc = ca if acc is None else acc + ca
        if keys1:
            m_w[...] = m
        if keys2:
            o_ref[...] = (acc * (1.0 / l)).T.astype(o_ref.dtype)

    @pl.when(i == 0)
    def _():
        run(0, ctx_len, None)

    @pl.when(i == 1)
    def _():
        run(1, n_all, ctx_len)

    for parity in (0, 1):
        @pl.when(jnp.logical_and(jnp.logical_and(i >= 2, i < n_tiles), i % 2 == parity))
        def _():
            run(parity, n_all, n_all)

    @pl.when(i == n_tiles)
    def _():
        run(n_tiles % 2, None, n_all)


def _attention(qt, k, vt, ctx_len):
    b, heads, tt, dq = k.shape
    dv = vt.shape[2]
    n_tiles = tt // TM
    return pl.pallas_call(
        functools.partial(_attn_kernel, ctx_len=ctx_len, n_tiles=n_tiles),
        grid=(b, heads, n_tiles + 1),
        in_specs=[
            pl.BlockSpec((None, None, dq, TM), lambda b, h, i: (b, h, 0, jnp.minimum(i, n_tiles - 1))),
            pl.BlockSpec((None, None, tt, dq), lambda b, h, i: (b, h, 0, 0)),
            pl.BlockSpec((None, None, dv, tt), lambda b, h, i: (b, h, 0, 0)),
        ],
        out_specs=pl.BlockSpec((None, TM, dv), lambda b, h, i: (b, jnp.maximum(i - 1, 0), h)),
        out_shape=jax.ShapeDtypeStruct((b, tt, heads * dv), BF16),
        scratch_shapes=[pltpu.VMEM((tt, TM), F32), pltpu.VMEM((tt, TM), F32),
                        pltpu.VMEM((1, TM), F32), pltpu.VMEM((1, TM), F32)],
        compiler_params=_cparams(("parallel", "parallel", "arbitrary"), V7X_VMEM_LIMIT),
        name="attention",
    )(qt, k, vt)


def _dft_kernel(c_ref, s_ref, x_ref, o_ref):
    half = x_ref.shape[1] // 2
    y = _dot(c_ref[...], x_ref[:, :half]) + _dot(s_ref[...], x_ref[:, half:])
    o_ref[...] = y.astype(o_ref.dtype)


def _dft_dense(xc, t_cos, t_sin):
    b, t, w2 = xc.shape
    return pl.pallas_call(
        _dft_kernel,
        grid=(b,),
        in_specs=[_const_spec((t, t)), _const_spec((t, t)), pl.BlockSpec((None, t, w2), lambda b: (b, 0, 0))],
        out_specs=pl.BlockSpec((None, t, w2 // 2), lambda b: (b, 0, 0)),
        out_shape=jax.ShapeDtypeStruct((b, t, w2 // 2), BF16),
        compiler_params=_cparams(("parallel",), V7X_VMEM_LIMIT),
        name="dft_dense",
    )(t_cos, t_sin, xc)


def _dft_split(t):
    n2 = 1 << ((t.bit_length()) // 2)
    return t // n2, n2


def _dft_stage1_kernel(x_ref, g_ref, zr_ref, zi_ref, *, n1, n2, cw):
    for t2 in range(n2):
        p = _dot(g_ref[t2], x_ref[:, t2 * 2 * cw:(t2 + 1) * 2 * cw])
        zr_ref[t2] = (p[:n1, :cw] - p[n1:, cw:]).astype(zr_ref.dtype)
        zi_ref[t2] = (p[:n1, cw:] + p[n1:, :cw]).astype(zi_ref.dtype)


def _dft_stage2_kernel(zr_ref, zi_ref, c_ref, s_ref, o_ref):
    o_ref[...] = (_dot(c_ref[...], zr_ref[...]) + _dot(s_ref[...], zi_ref[...])).astype(o_ref.dtype)


def _dft_two_stage(xc, tables):
    g, c2, s2 = tables
    b, t, w2 = xc.shape
    cw = w2 // 2
    n1, n2 = _dft_split(t)
    zr, zi = pl.pallas_call(
        functools.partial(_dft_stage1_kernel, n1=n1, n2=n2, cw=cw),
        grid=(b,),
        in_specs=[pl.BlockSpec((None, n1, n2 * w2), lambda b: (b, 0, 0)), _const_spec(g.shape)],
        out_specs=[pl.BlockSpec((None, n2, n1, cw), lambda b: (b, 0, 0, 0))] * 2,
        out_shape=[jax.ShapeDtypeStruct((b, n2, n1, cw), BF16)] * 2,
        compiler_params=_cparams(("parallel",), V7X_VMEM_LIMIT),
        name="dft_stage1",
    )(xc.reshape(b, n1, n2 * w2), g)
    y = pl.pallas_call(
        _dft_stage2_kernel,
        grid=(b,),
        in_specs=[pl.BlockSpec((None, n2, n1 * cw), lambda b: (b, 0, 0))] * 2 + [_const_spec((n2, n2))] * 2,
        out_specs=pl.BlockSpec((None, n2, n1 * cw), lambda b: (b, 0, 0)),
        out_shape=jax.ShapeDtypeStruct((b, n2, n1 * cw), BF16),
        compiler_params=_cparams(("parallel",), V7X_VMEM_LIMIT),
        name="dft_stage2",
    )(zr.reshape(b, n2, n1 * cw), zi.reshape(b, n2, n1 * cw), c2, s2)
    return y.reshape(b, t, cw)


def _odd_out_kernel(fc_ref, fl_ref, at_ref, x_ref, mod_ref, wf_ref, wa_ref, o_ref):
    is_ctx = pl.program_id(1) == 0
    fm = jnp.concatenate([jnp.where(is_ctx, fc_ref[s], fl_ref[s]) for s in range(NB)], axis=0)
    at = jnp.concatenate([at_ref[s] for s in range(NB)], axis=0)
    y = _dot(fm, wf_ref[...]) + _dot(at, wa_ref[...])
    for s in range(NB):
        o_ref[s] = x_ref[s] + mod_ref[s, 2:3, :] * y[s * TM:(s + 1) * TM]


def _odd_out(fm_ctx, fm_lat, attn, x, mod, w_out):
    b, tt, d = x.shape
    cw = fm_ctx.shape[-1]
    aw = attn.shape[-1]
    return pl.pallas_call(
        _odd_out_kernel,
        grid=(b // NB, tt // TM),
        in_specs=[
            pl.BlockSpec((NB, TM, cw), lambda b, i: (b, 0, 0)),
            pl.BlockSpec((NB, TM, cw), lambda b, i: (b, jnp.maximum(i - 1, 0), 0)),
            _rows_spec(aw),
            _rows_spec(d),
            _mod_spec(d, NB),
            _const_spec((cw, d)),
            _const_spec((aw, d)),
        ],
        out_specs=_rows_spec(d),
        out_shape=jax.ShapeDtypeStruct((b, tt, d), F32),
        compiler_params=_cparams(("parallel", "parallel"), V7X_VMEM_LIMIT),
        name="odd_out",
    )(fm_ctx, fm_lat, attn, x, mod, w_out[:cw], w_out[cw:])


def _ffn_kernel(xp_ref, x_ref, xn_ref, mod_ref, nw_ref, wg_ref, wv_ref, cw_ref, cb_ref, wd_ref, o_ref,
                *, first_tile, n_tiles):
    i = pl.program_id(1) + first_tile
    ext = TM + 2 * HALO
    row = lax.broadcasted_iota(jnp.int32, (ext, 1), 0)
    keep = jnp.logical_and(jnp.logical_or(row >= HALO, i > 1),
                           jnp.logical_or(row < HALO + TM, jnp.logical_and(i > 0, i < n_tiles - 1)))
    nw = nw_ref[...]
    hes = []
    for s in range(NB):
        xe = jnp.concatenate([xp_ref[s], x_ref[s], xn_ref[s]], axis=0)
        hes.append(jnp.where(keep, _modulate(xe, nw, mod_ref[s, 3:4, :], mod_ref[s, 4:5, :]), 0.0))
    ge = _dot(jnp.concatenate(hes, axis=0).astype(BF16), wg_ref[...])
    val = _dot(jnp.concatenate([h[HALO:HALO + TM] for h in hes], axis=0).astype(BF16), wv_ref[...])
    cw0, cw1, cw2 = cw_ref[0:1, :], cw_ref[1:2, :], cw_ref[2:3, :]
    gc = jnp.concatenate(
        [ge[s * ext + HALO - 1:s * ext + HALO - 1 + TM] * cw0 + ge[s * ext + HALO:s * ext + HALO + TM] * cw1
         + ge[s * ext + HALO + 1:s * ext + HALO + 1 + TM] * cw2 for s in range(NB)], axis=0) + cb_ref[...]
    y = _dot((_silu(gc) * val).astype(BF16), wd_ref[...])
    for s in range(NB):
        o_ref[s] = x_ref[s] + mod_ref[s, 5:6, :] * y[s * TM:(s + 1) * TM]


def _conv_ffn(x, mod, nw, w_up, conv_w, conv_b, wd, skip_ctx):
    b, tt, d = x.shape
    ff = wd.shape[0]
    n_tiles = tt // TM
    first = 1 if skip_ctx else 0
    r = TM // HALO
    nblk = tt // HALO
    return pl.pallas_call(
        functools.partial(_ffn_kernel, first_tile=first, n_tiles=n_tiles),
        grid=(b // NB, n_tiles - first),
        in_specs=[
            pl.BlockSpec((NB, HALO, d), lambda b, i: (b, jnp.maximum((i + first) * r - 1, 0), 0)),
            pl.BlockSpec((NB, TM, d), lambda b, i: (b, i + first, 0)),
            pl.BlockSpec((NB, HALO, d), lambda b, i: (b, jnp.minimum((i + first + 1) * r, nblk - 1), 0)),
            pl.BlockSpec((None, NB, 6, d), lambda b, i: (jnp.minimum(i + first, 1), b, 0, 0)),
            _const_spec((1, d)),
            pl.BlockSpec((d, ff), lambda b, i: (0, 0), pipeline_mode=pl.Buffered(1)),
            pl.BlockSpec((d, ff), lambda b, i: (0, 1), pipeline_mode=pl.Buffered(1)),
            _const_spec((CONV_W, ff)),
            _const_spec((1, ff)),
            _const_spec((ff, d)),
        ],
        out_specs=_rows_spec(d),
        out_shape=jax.ShapeDtypeStruct((b, tt - first * TM, d), F32),
        compiler_params=_cparams(("parallel", "parallel"), V7X_VMEM_LIMIT),
        name="conv_ffn",
    )(x, x, x, mod, nw, w_up, w_up, conv_w, conv_b, wd)


def _rope_swap_perm():
    q = MLA_ROPE // 4
    return np.concatenate([np.arange(q, 2 * q), np.arange(0, q), np.arange(3 * q, 4 * q), np.arange(2 * q, 3 * q)])


def _pad128(v):
    return jnp.pad(v, (0, 128 - v.shape[0])).reshape(1, 128)


def _odd_weights(w_in, qa_w, w_qb, kva_w, w_kvb, qn_w, kn_w, heads, cw, ql, kvl, cos_t, sin_t):
    d = w_in.shape[0]
    perm = _rope_swap_perm()
    nope, rope, qk = MLA_NOPE, MLA_ROPE, MLA_NOPE + MLA_ROPE
    o = cw + ql + kvl
    z = jnp.zeros((d, 128 - rope), w_in.dtype)
    kpe = w_in[:, o:o + rope]
    w_in_ext = jnp.concatenate([w_in[:, :o], kpe, z, kpe[:, perm], z], axis=1).astype(BF16)
    wq = w_qb.reshape(ql, heads, qk)
    zq = jnp.zeros((ql, heads, 128 - rope), w_qb.dtype)
    wq_rope = wq[:, :, nope:]
    w_q = jnp.concatenate([
        wq[:, :, :nope].reshape(ql, heads * 128),
        jnp.concatenate([wq_rope, zq], axis=2).reshape(ql, heads * 128),
        jnp.concatenate([wq_rope[:, :, perm], zq], axis=2).reshape(ql, heads * 128),
    ], axis=1).astype(BF16)
    wkv = w_kvb.reshape(kvl, heads, nope + MLA_V_DIM)
    w_kv = jnp.concatenate([wkv[:, :, :nope].reshape(kvl, heads * nope),
                            wkv[:, :, nope:].reshape(kvl, heads * MLA_V_DIM)], axis=1).astype(BF16)
    gd = cw // FOURIER_GROUPS
    jk = (np.arange(gd)[:, None] * np.arange(gd)[None, :]) % gd
    ang = 2.0 * np.pi * jk / gd
    eye = np.eye(FOURIER_GROUPS)
    bc = np.concatenate([np.kron(eye, np.cos(ang)), np.kron(eye, -np.sin(ang))], axis=1) / np.sqrt(gd)
    return {
        "w_in": w_in_ext, "qa_w": qa_w.reshape(1, ql), "kva_w": kva_w.reshape(1, kvl),
        "kp_w": _pad128(kn_w[nope:]), "kps_w": _pad128(kn_w[nope:][perm]),
        "w_q": w_q, "qn_w": qn_w[:nope].reshape(1, nope),
        "qp_w": _pad128(qn_w[nope:]), "qps_w": _pad128(qn_w[nope:][perm]),
        "w_kv": w_kv, "kn_w": kn_w[:nope].reshape(1, nope),
        "cos": cos_t, "sin": sin_t, "bc": jnp.asarray(bc, F32).astype(BF16),
    }


def _rope_tables(t_lat, ctx_len):
    rows = t_lat // GRID_W
    row = jnp.repeat(jnp.arange(rows), GRID_W)
    col = jnp.tile(jnp.arange(GRID_W), rows)
    r_axis = MLA_ROPE // 2
    inv_freq = ROPE_THETA ** (-jnp.arange(0, r_axis, 2, dtype=F32) / r_axis)
    ang = jnp.stack([row, col], axis=-1).astype(F32)[:, :, None] * inv_freq
    cos, sin = jnp.cos(ang), jnp.sin(ang)
    cos64 = jnp.concatenate([cos[:, 0], cos[:, 0], cos[:, 1], cos[:, 1]], axis=-1)
    sin64 = jnp.concatenate([-sin[:, 0], sin[:, 0], -sin[:, 1], sin[:, 1]], axis=-1)
    cos64 = jnp.concatenate([jnp.ones((ctx_len, MLA_ROPE), F32), cos64], axis=0)
    sin64 = jnp.concatenate([jnp.zeros((ctx_len, MLA_ROPE), F32), sin64], axis=0)
    pad = ((0, 0), (0, 128 - MLA_ROPE))
    return jnp.pad(cos64, pad), jnp.pad(sin64, pad)


def _cos_sin(phase, period, scale):
    ang = (phase % period).astype(F32) * (2.0 * np.pi / period)
    return jnp.cos(ang) * scale, jnp.sin(ang) * scale


def _dft_dense_tables(t):
    idx = jnp.arange(t, dtype=jnp.int32)
    c, s = _cos_sin(idx[:, None] * idx[None, :], t, 1.0 / np.sqrt(t))
    return c.astype(BF16), s.astype(BF16)


def _dft_two_stage_tables(t):
    n1, n2 = _dft_split(t)
    u1 = jnp.arange(n1, dtype=jnp.int32)
    pos = jnp.arange(t, dtype=jnp.int32).reshape(n1, n2)
    c, s = _cos_sin(pos.T[:, None, :] * u1[None, :, None], t, 1.0 / np.sqrt(n1))
    g = jnp.concatenate([c, -s], axis=1).astype(BF16)
    i2 = jnp.arange(n2, dtype=jnp.int32)
    c2, s2 = _cos_sin(i2[:, None] * i2[None, :], n2, 1.0 / np.sqrt(n2))
    return g, c2.astype(BF16), s2.astype(BF16)


def kernel(x, c, ctx, c_ctx, ada_w, ada_b, norm_mix_w, norm_ffn_w, ev_w_in, ev_lb_logits, ev_onorm_w, ev_vnorm_w, ev_ws, ev_bs, ev_w_out, od_w_in, od_qa_norm_w, od_w_qb, od_kva_norm_w, od_w_kvb, od_q_norm_w, od_k_norm_w, od_w_out, ffn_w_up, ffn_conv_w, ffn_conv_b, ffn_w_down):
    bsz, t_lat, d = x.shape
    ctx_len = ctx.shape[1]
    depth = ada_w.shape[0]
    assert ctx_len == TM and t_lat % ATTN_KC == 0 and t_lat % GRID_W == 0 and bsz % NB == 0
    a_width = ev_lb_logits.shape[-1]
    ql, kvl = od_qa_norm_w.shape[-1], od_kva_norm_w.shape[-1]
    cw = od_w_in.shape[-1] - ql - kvl - MLA_ROPE
    heads = od_w_qb.shape[-1] // (MLA_NOPE + MLA_ROPE)
    d_ff = ffn_w_down.shape[1]

    pad_rows = (-(bsz + 1)) % 8
    cvec = jnp.concatenate([c, c_ctx[None, :], jnp.zeros((pad_rows, d), F32)], axis=0)
    mods = _ada_tables(cvec, ada_w, ada_b)
    mod_lat = mods[:, :bsz].reshape(depth, 1, bsz, 6, d)
    mod_ctx = jnp.broadcast_to(mods[:, bsz].reshape(depth, 1, 1, 6, d), (depth, 1, bsz, 6, d))
    mods = jnp.concatenate([mod_ctx, mod_lat], axis=1)

    lb_p = jax.nn.softmax(ev_lb_logits.astype(F32), axis=0)
    lbs = jnp.cumsum(lb_p, axis=0) - lb_p[0]
    cos_t, sin_t = _rope_tables(t_lat, ctx_len)
    dft_lat = dft_ctx = None

    xs = (ctx, x)
    for l in range(depth):
        last = l == depth - 1
        mod = mods[l]
        nmw = norm_mix_w[l].reshape(1, d)
        if l % 2 == 0:
            e = l // 2
            p = _even_in(xs, ctx_len + t_lat, mod, nmw, ev_w_in[e].astype(BF16))
            o_fwd = _hgrn_scan(p, lbs[e, 0], False, ctx_len, a_width)
            ya = _hgrn_scan(p, lbs[e, 1], True, ctx_len, a_width, (o_fwd, ev_onorm_w[e]))
            bw = d - a_width
            gd = bw // ev_ws.shape[1]
            bsb = jnp.repeat(ev_bs[e].T, gd, axis=1)
            xs = _even_out(ya, p, xs, mod, ev_vnorm_w[e].reshape(1, bw), ev_ws[e].astype(BF16), bsb,
                           ev_w_out[e].astype(BF16), a_width)
        else:
            o = l // 2
            if dft_lat is None:
                dft_lat, dft_ctx = _dft_two_stage_tables(t_lat), _dft_dense_tables(ctx_len)
            wts = _odd_weights(od_w_in[o], od_qa_norm_w[o], od_w_qb[o], od_kva_norm_w[o], od_w_kvb[o],
                               od_q_norm_w[o], od_k_norm_w[o], heads, cw, ql, kvl, cos_t, sin_t)
            xc_ctx, xc_lat, qt, k, vt = _odd_in(xs, mod, nmw, wts, heads, cw, ql, kvl)
            attn = _attention(qt, k, vt, ctx_len)
            fm_ctx = _dft_dense(xc_ctx, *dft_ctx)
            fm_lat = _dft_two_stage(xc_lat, dft_lat)
            xs = _odd_out(fm_ctx, fm_lat, attn, xs, mod, od_w_out[o].astype(BF16))
        xs = _conv_ffn(xs, mod, norm_ffn_w[l].reshape(1, d), ffn_w_up[l].astype(BF16), ffn_conv_w[l],
                       ffn_conv_b[l].reshape(1, d_ff), ffn_w_down[l].astype(BF16), skip_ctx=last)
    return xs
```

```python
import functools

import numpy as np
import jax
import jax.numpy as jnp
from jax import lax
from jax.experimental import pallas as pl
from jax.experimental.pallas import tpu as pltpu

F32 = jnp.float32
BF16 = jnp.bfloat16

EPS = 1e-6
F_MIN = 1e-6
GRID_W = 64
ROPE_THETA = 10000.0
HGRN_HEAD_DIM = 128
HGRN_CHUNK = 64
HGRN_STEP_CHUNKS = 4
MLP_CHUNK = 128
MLA_NOPE = 128
MLA_ROPE = 64
MLA_V_DIM = 128
V_PAD = 16
FOURIER_GROUPS = 4
CONV_W = 3

TM = 256
NB = 2
HALO = 8
ATTN_KC = 512
V7X_VMEM_LIMIT = 56 * 1024 * 1024


def _cparams(sem, vmem=None):
    return pltpu.CompilerParams(dimension_semantics=sem, vmem_limit_bytes=vmem)


def _dot(a, b):
    return jnp.dot(a, b, preferred_element_type=F32)


def _dot_nt(a, b):
    return lax.dot_general(a, b, (((1,), (1,)), ((), ())), preferred_element_type=F32)


def _dot_tn(a, b):
    return lax.dot_general(a, b, (((0,), (0,)), ((), ())), preferred_element_type=F32)


def _silu(x):
    return x * jax.nn.sigmoid(x)


def _gelu_tanh(x):
    return 0.5 * x * (1.0 + jnp.tanh(0.7978845608028654 * (x + 0.044715 * (x * x * x))))


def _rms(x, n=None):
    n = x.shape[-1] if n is None else n
    return x * lax.rsqrt(jnp.sum(x * x, axis=-1, keepdims=True) * (1.0 / n) + EPS)


def _modulate(x, nw, shift, scale):
    return (_rms(x) * nw) * (1.0 + scale) + shift


def _mod_spec(d, nb):
    return pl.BlockSpec((None, nb, 6, d), lambda b, i: (jnp.minimum(i, 1), b, 0, 0))


def _rows_spec(cols, colblk=0):
    return pl.BlockSpec((NB, TM, cols), lambda b, i: (b, i, colblk))


def _const_spec(shape):
    nd = len(shape)
    return pl.BlockSpec(shape, lambda *_: (0,) * nd, pipeline_mode=pl.Buffered(1))


def _ada_kernel(c_ref, w_ref, b_ref, o_ref):
    s = _silu(c_ref[...]).astype(BF16)
    o_ref[...] = _dot(s, w_ref[...].astype(BF16)) + b_ref[...]


def _ada_tables(cvec, ada_w, ada_b):
    depth, d, n = ada_w.shape
    rows = cvec.shape[0]
    tn = 1536
    return pl.pallas_call(
        _ada_kernel,
        grid=(depth, n // tn),
        in_specs=[
            pl.BlockSpec((rows, d), lambda l, j: (0, 0)),
            pl.BlockSpec((None, d, tn), lambda l, j: (l, 0, j)),
            pl.BlockSpec((None, 1, tn), lambda l, j: (l, 0, j)),
        ],
        out_specs=pl.BlockSpec((None, rows, tn), lambda l, j: (l, 0, j)),
        out_shape=jax.ShapeDtypeStruct((depth, rows, n), F32),
        compiler_params=_cparams(("parallel", "parallel"), V7X_VMEM_LIMIT),
        name="ada_tables",
    )(cvec, ada_w, ada_b.reshape(depth, 1, n))


def _stream_specs(xs, d):
    if isinstance(xs, tuple):
        return ([pl.BlockSpec((NB, TM, d), lambda b, i: (b, 0, 0)),
                 pl.BlockSpec((NB, TM, d), lambda b, i: (b, jnp.maximum(i - 1, 0), 0))], list(xs))
    return [_rows_spec(d)], [xs]


def _stream_rows(x_refs, s):
    if len(x_refs) == 1:
        return x_refs[0][s]
    return jnp.where(pl.program_id(1) == 0, x_refs[0][s], x_refs[1][s])


def _even_in_kernel(*refs, n_x):
    x_refs = refs[:n_x]
    mod_ref, nw_ref, w_ref, o_ref = refs[n_x:]
    nw = nw_ref[...]
    h = jnp.concatenate([_modulate(_stream_rows(x_refs, s), nw, mod_ref[s, 0:1, :], mod_ref[s, 1:2, :]).astype(BF16)
                         for s in range(NB)], axis=0)
    p = _dot(h, w_ref[...])
    for s in range(NB):
        o_ref[s] = p[s * TM:(s + 1) * TM]


def _even_in(xs, tt, mod, nw, w_in):
    d, n = w_in.shape
    x_specs, x_args = _stream_specs(xs, d)
    b = x_args[0].shape[0]
    return pl.pallas_call(
        functools.partial(_even_in_kernel, n_x=len(x_args)),
        grid=(b // NB, tt // TM),
        in_specs=x_specs + [
            _mod_spec(d, NB),
            _const_spec((1, d)),
            _const_spec((d, n)),
        ],
        out_specs=_rows_spec(n),
        out_shape=jax.ShapeDtypeStruct((b, tt, n), F32),
        compiler_params=_cparams(("parallel", "parallel"), V7X_VMEM_LIMIT),
        name="even_in",
    )(*x_args, mod, nw, w_in)


def _hgrn_consts(reverse):
    c = HGRN_CHUNK
    idx = np.arange(c)
    if not reverse:
        tri = (idx[None, :] <= idx[:, None]).astype(np.float32)
    else:
        tri = (idx[None, :] >= idx[:, None]).astype(np.float32)
    masks = []
    for w in (32, 16, 8):
        grp = idx // (2 * w)
        qrow = (idx % (2 * w) >= w) if not reverse else (idx % (2 * w) < w)
        masks.append((grp[:, None] == grp[None, :]) & qrow[:, None] & (~qrow)[None, :])
    blk = idx // 8
    causal = (idx[None, :] <= idx[:, None]) if not reverse else (idx[None, :] >= idx[:, None])
    masks.append((blk[:, None] == blk[None, :]) & causal)
    return np.kron(np.eye(HGRN_STEP_CHUNKS), tri).astype(np.float32), np.stack(masks).astype(np.float32)


def _hgrn_prepare(pq_ref, pv_ref, pz_ref, lb, tri_ref, width):
    f = jnp.maximum(lb + (1.0 - lb) * jax.nn.sigmoid(pz_ref[...]), F_MIN)
    g = jnp.log(f)
    g1 = g.astype(BF16)
    r1 = g - g1.astype(F32)
    g2 = r1.astype(BF16)
    g3 = (r1 - g2.astype(F32)).astype(BF16)
    b3 = _dot(tri_ref[...], jnp.concatenate([g1, g2, g3], axis=1))
    b_all = b3[:, :width] + b3[:, width:2 * width] + b3[:, 2 * width:]
    return _silu(pq_ref[...]), 1.0 - f, pv_ref[...], b_all


def _hgrn_chunk(feats, ci, reverse, masks_ref, st, heads):
    c, hd = HGRN_CHUNK, HGRN_HEAD_DIM
    width = heads * hd
    q, kk, v, b_all = feats
    rs = slice(ci * c, (ci + 1) * c)
    b = b_all[rs]

    def ref_rows(rows, n):
        return jnp.concatenate([jnp.broadcast_to(b[r:r + 1, :], (n, width)) for r in rows], axis=0)

    e_lv = []
    for w in (32, 16, 8):
        mids = [gi * 2 * w + (w - 1 if not reverse else w) for gi in range(c // (2 * w))]
        e_lv.append(jnp.exp(-jnp.abs(b - ref_rows(mids, 2 * w))))
    anchor = ref_rows([8 * m + (3 if not reverse else 4) for m in range(c // 8)], 8)
    e_dq = jnp.exp(b - anchor)
    e_dk = jnp.exp(anchor - b)
    last = c - 1 if not reverse else 0
    e_in = jnp.exp(b)
    e_out = jnp.exp(ref_rows([last], c) - b)
    parts = []
    for h in range(heads):
        sl = slice(h * hd, (h + 1) * hd)
        qh, kh = q[rs, sl], kk[rs, sl]
        a = None
        for lv in range(3):
            ew = e_lv[lv][:, sl]
            t = _dot_nt((qh * ew).astype(BF16), (kh * ew).astype(BF16)) * masks_ref[lv]
            a = t if a is None else a + t
        d8 = _dot_nt((qh * e_dq[:, sl]).astype(BF16), (kh * e_dk[:, sl]).astype(BF16))
        a = a + jnp.where(masks_ref[3] > 0.5, d8, 0.0)
        vb = v[rs, sl].astype(BF16)
        qhat = (qh * e_in[:, sl]).astype(BF16)
        khat = (kh * e_out[:, sl]).astype(BF16)
        parts.append((a.astype(BF16), vb, qhat, _dot_tn(vb, khat), e_in[last:last + 1, sl]))
    heads_out = []
    for h, (a, vb, qhat, upd, e_last) in enumerate(parts):
        heads_out.append(_dot(a, vb) + _dot_nt(qhat, st[h].astype(BF16)))
        st[h] = st[h] * e_last + upd
    return jnp.concatenate(heads_out, axis=1)


def _hgrn_kernel(fq_ref, fv_ref, fz_ref, bq_ref, bv_ref, bz_ref, lb_ref, ftri_ref, btri_ref, fmask_ref, bmask_ref,
                 of_ref, ob_ref, fst_ref, bst_ref, *, heads):
    nck = HGRN_STEP_CHUNKS
    width = heads * HGRN_HEAD_DIM

    @pl.when(pl.program_id(1) == 0)
    def _():
        fst_ref[...] = jnp.zeros_like(fst_ref)
        bst_ref[...] = jnp.zeros_like(bst_ref)

    ffeats = _hgrn_prepare(fq_ref, fv_ref, fz_ref, lb_ref[0:1, :], ftri_ref, width)
    bfeats = _hgrn_prepare(bq_ref, bv_ref, bz_ref, lb_ref[1:2, :], btri_ref, width)
    fst = [fst_ref[h] for h in range(heads)]
    bst = [bst_ref[h] for h in range(heads)]
    fouts, bouts = [None] * nck, [None] * nck
    for k in range(nck):
        fouts[k] = _hgrn_chunk(ffeats, k, False, fmask_ref, fst, heads)
        bouts[nck - 1 - k] = _hgrn_chunk(bfeats, nck - 1 - k, True, bmask_ref, bst, heads)
    for h in range(heads):
        fst_ref[h] = fst[h]
        bst_ref[h] = bst[h]
    of_ref[...] = jnp.concatenate(fouts, axis=0)
    ob_ref[...] = jnp.concatenate(bouts, axis=0)


def _hgrn_scan(p, lbs, ctx_len, width):
    b, tt, _ = p.shape
    rows = HGRN_CHUNK * HGRN_STEP_CHUNKS
    heads = width // HGRN_HEAD_DIM
    nb, nbc = tt // rows, ctx_len // rows
    ftri, fmask = _hgrn_consts(False)
    btri, bmask = _hgrn_consts(True)

    def bblk(j):
        return jnp.where(j < nbc, nbc - 1 - j, nb - 1 - (j - nbc))

    def fcol(k):
        return pl.BlockSpec((None, rows, width), lambda b, j: (b, j, k))

    def bcol(k):
        return pl.BlockSpec((None, rows, width), lambda b, j: (b, bblk(j), k))

    consts = [lbs, jnp.asarray(ftri, BF16), jnp.asarray(btri, BF16), jnp.asarray(fmask, F32), jnp.asarray(bmask, F32)]
    state = pltpu.VMEM((heads, HGRN_HEAD_DIM, HGRN_HEAD_DIM), F32)
    return pl.pallas_call(
        functools.partial(_hgrn_kernel, heads=heads),
        grid=(b, nb),
        in_specs=[fcol(0), fcol(1), fcol(2), bcol(0), bcol(1), bcol(3)] + [_const_spec(a.shape) for a in consts],
        out_specs=[fcol(0), bcol(0)],
        out_shape=[jax.ShapeDtypeStruct((b, tt, width), F32)] * 2,
        scratch_shapes=[state, state],
        compiler_params=_cparams(("parallel", "arbitrary"), V7X_VMEM_LIMIT),
        name="hgrn_scan",
    )(p, p, p, p, p, p, *consts)


def _even_out_kernel(*refs, groups, n_x):
    of_ref, ob_ref, pg_ref, pu_ref, pv_ref = refs[:5]
    x_refs = refs[5:5 + n_x]
    mod_ref, onw_ref, vnw_ref, ws_ref, bsb_ref, wa_ref, wb_ref, o_ref = refs[5 + n_x:]
    gd = pu_ref.shape[2] // groups
    hd = HGRN_HEAD_DIM
    onw = onw_ref[...]
    ms, yas = [], []
    for s in range(NB):
        o = of_ref[s] + ob_ref[s]
        yn = jnp.concatenate([_rms(o[:, h * hd:(h + 1) * hd]) * onw for h in range(o.shape[1] // hd)], axis=1)
        yas.append((yn * _silu(pg_ref[s])).astype(BF16))
        for n in range(TM // MLP_CHUNK):
            rows = slice(n * MLP_CHUNK, (n + 1) * MLP_CHUNK)
            u = _gelu_tanh(pu_ref[s, rows, :])
            v = _gelu_tanh(pv_ref[s, rows, :])
            parts = []
            for g in range(groups):
                cs = slice(g * gd, (g + 1) * gd)
                vg = (_rms(v[:, cs]) * vnw_ref[:, cs]).astype(BF16)
                sv = _dot(ws_ref[g], vg) + bsb_ref[:, cs]
                parts.append(u[:, cs] * sv)
            ms.append(jnp.concatenate(parts, axis=1).astype(BF16))
    m = jnp.concatenate(ms, axis=0)
    y = _dot(jnp.concatenate(yas, axis=0), wa_ref[...]) + _dot(m, wb_ref[...])
    for s in range(NB):
        o_ref[s] = _stream_rows(x_refs, s) + mod_ref[s, 2:3, :] * y[s * TM:(s + 1) * TM]


def _even_out(o_fwd, o_bwd, onw, p, xs, mod, vnw, ws, bsb, w_out, a_width):
    b, tt, _ = p.shape
    d = w_out.shape[1]
    bw = d - a_width
    groups = ws.shape[0]
    ucol = 5 * a_width // bw
    x_specs, x_args = _stream_specs(xs, d)
    return pl.pallas_call(
        functools.partial(_even_out_kernel, groups=groups, n_x=len(x_args)),
        grid=(b // NB, tt // TM),
        in_specs=[
            _rows_spec(a_width),
            _rows_spec(a_width),
            _rows_spec(a_width, 4),
            _rows_spec(bw, ucol),
            _rows_spec(bw, ucol + 1)] + x_specs + [
            _mod_spec(d, NB),
            _const_spec((1, HGRN_HEAD_DIM)),
            _const_spec((1, bw)),
            _const_spec(ws.shape),
            _const_spec(bsb.shape),
            _const_spec((a_width, d)),
            _const_spec((bw, d)),
        ],
        out_specs=_rows_spec(d),
        out_shape=jax.ShapeDtypeStruct((b, tt, d), F32),
        compiler_params=_cparams(("parallel", "parallel"), V7X_VMEM_LIMIT),
        name="even_out",
    )(o_fwd, o_bwd, p, p, p, *x_args, mod, onw.reshape(1, HGRN_HEAD_DIM), vnw, ws, bsb,
      w_out[:a_width], w_out[a_width:])


def _odd_in_kernel(x_ref, mod_ref, nw_ref, win_ref, qaw_ref, kvaw_ref, kpw_ref, kpsw_ref, wq_ref, qnw_ref,
                   qpw_ref, qpsw_ref, wkv_ref, knw_ref, cos_ref, sin_ref, bc_ref,
                   xc_ctx_ref, xc_lat_ref, qt_ref, k_ref, vt_ref, *, heads, cw, ql, kvl, scale):
    h = _modulate(x_ref[...], nw_ref[...], mod_ref[0, 0:1, :], mod_ref[0, 1:2, :])
    p = _dot(h.astype(BF16), win_ref[...])
    xc = _dot(p[:, :cw].astype(BF16), bc_ref[...]).astype(xc_lat_ref.dtype)
    xc_lat_ref[...] = xc

    @pl.when(pl.program_id(1) == 0)
    def _():
        xc_ctx_ref[...] = xc
    cos, sin = cos_ref[...], sin_ref[...]
    nope, rope = MLA_NOPE, MLA_ROPE
    q_lat = (_rms(p[:, cw:cw + ql]) * qaw_ref[...]).astype(BF16)
    kv_lat = (_rms(p[:, cw + ql:cw + ql + kvl]) * kvaw_ref[...]).astype(BF16)
    o = cw + ql + kvl
    kp, kps = p[:, o:o + 128], p[:, o + 128:o + 256]
    kpr = lax.rsqrt(jnp.sum(kp * kp, axis=-1, keepdims=True) * (1.0 / rope) + EPS)
    k_pe = (kp * kpw_ref[...] * cos + kps * kpsw_ref[...] * sin) * kpr
    qf = _dot(q_lat, wq_ref[...])
    kvf = _dot(kv_lat, wkv_ref[...])
    hw = heads * 128
    ones_pad = (lax.broadcasted_iota(jnp.int32, (V_PAD, TM), 0) == 0).astype(F32)
    for hh in range(heads):
        cs = slice(hh * 128, (hh + 1) * 128)
        qn = _rms(qf[:, cs]) * qnw_ref[...]
        qp = qf[:, hw + hh * 128:hw + (hh + 1) * 128]
        qps = qf[:, 2 * hw + hh * 128:2 * hw + (hh + 1) * 128]
        qpr = lax.rsqrt(jnp.sum(qp * qp, axis=-1, keepdims=True) * (1.0 / rope) + EPS)
        q_pe = (qp * qpw_ref[...] * cos + qps * qpsw_ref[...] * sin) * qpr
        qt_ref[hh] = (jnp.concatenate([qn, q_pe], axis=1) * scale).T.astype(qt_ref.dtype)
        kn = _rms(kvf[:, cs]) * knw_ref[...]
        k_ref[hh] = jnp.concatenate([kn, k_pe], axis=1).astype(k_ref.dtype)
        vt_ref[hh] = jnp.concatenate([kvf[:, hw + hh * 128:hw + (hh + 1) * 128].T, ones_pad],
                                     axis=0).astype(vt_ref.dtype)


def _odd_in(x, mod, nw, wts, heads, cw, ql, kvl):
    b, tt, d = x.shape
    scale = float(MLA_NOPE + MLA_ROPE) ** -0.5 * float(np.log2(np.e))
    consts = [wts[k] for k in ("w_in", "qa_w", "kva_w", "kp_w", "kps_w", "w_q", "qn_w", "qp_w", "qps_w",
                               "w_kv", "kn_w")]
    row128 = pl.BlockSpec((TM, 128), lambda b, i: (i, 0))
    in_specs = ([pl.BlockSpec((None, TM, d), lambda b, i: (b, i, 0)), _mod_spec(d, 1), _const_spec((1, d))]
                + [_const_spec(a.shape) for a in consts] + [row128, row128, _const_spec(wts["bc"].shape)])
    return pl.pallas_call(
        functools.partial(_odd_in_kernel, heads=heads, cw=cw, ql=ql, kvl=kvl, scale=scale),
        grid=(b, tt // TM),
        in_specs=in_specs,
        out_specs=[
            pl.BlockSpec((None, TM, 2 * cw), lambda b, i: (b, 0, 0)),
            pl.BlockSpec((None, TM, 2 * cw), lambda b, i: (b, jnp.maximum(i - 1, 0), 0)),
            pl.BlockSpec((None, heads, 256, TM), lambda b, i: (b, 0, 0, i)),
            pl.BlockSpec((None, heads, TM, 256), lambda b, i: (b, 0, i, 0)),
            pl.BlockSpec((None, heads, MLA_V_DIM + V_PAD, TM), lambda b, i: (b, 0, 0, i)),
        ],
        out_shape=[
            jax.ShapeDtypeStruct((b, TM, 2 * cw), BF16),
            jax.ShapeDtypeStruct((b, tt - TM, 2 * cw), BF16),
            jax.ShapeDtypeStruct((b, heads, 256, tt), BF16),
            jax.ShapeDtypeStruct((b, heads, tt, 256), BF16),
            jax.ShapeDtypeStruct((b, heads, MLA_V_DIM + V_PAD, tt), BF16),
        ],
        compiler_params=_cparams(("parallel", "arbitrary"), V7X_VMEM_LIMIT),
        name="odd_in",
    )(x, mod, nw, *consts, wts["cos"], wts["sin"], wts["bc"])


def _attn_kernel(qt_ref, k_ref, vt_ref, o_ref, sa_ref, sb_ref, ma_ref, mb_ref, *, ctx_len, n_tiles):
    i = pl.program_id(2)
    n_all = k_ref.shape[0]
    bufs = ((sa_ref, ma_ref), (sb_ref, mb_ref))

    def chunks(n_keys):
        return [(st, min(ATTN_KC, n_keys - st)) for st in range(0, n_keys, ATTN_KC)]

    def run(step_parity, keys1, keys2):
        s_w, m_w = bufs[step_parity]
        s_r, m_r = bufs[1 - step_parity]
        c1 = chunks(keys1) if keys1 else []
        c2 = chunks(keys2) if keys2 else []
        if keys1:
            qt = qt_ref[...]
        if keys2:
            m_prev = m_r[...]
        m = acc = None
        for idx in range(max(len(c1), len(c2))):
            if idx < len(c1):
                st, sz = c1[idx]
                s = _dot(k_ref[st:st + sz, :], qt)
                s_w[st:st + sz, :] = s
                cm = jnp.max(s, axis=0, keepdims=True)
                m = cm if m is None else jnp.maximum(m, cm)
            if idx < len(c2):
                st, sz = c2[idx]
                p = jnp.exp2(s_r[st:st + sz, :] - m_prev).astype(BF16)
                ca = _dot(vt_ref[:, st:st + sz], p)
                acc = ca if acc is None else acc + ca
        if keys1:
            m_w[...] = m
        if keys2:
            dv = o_ref.shape[1]
            o_ref[...] = (acc[:dv] * (1.0 / acc[dv:dv + 1])).T.astype(o_ref.dtype)

    @pl.when(i == 0)
    def _():
        run(0, ctx_len, None)

    @pl.when(i == 1)
    def _():
        run(1, n_all, ctx_len)

    for parity in (0, 1):
        @pl.when(jnp.logical_and(jnp.logical_and(i >= 2, i < n_tiles), i % 2 == parity))
        def _():
            run(parity, n_all, n_all)

    @pl.when(i == n_tiles)
    def _():
        run(n_tiles % 2, None, n_all)


def _attention(qt, k, vt, ctx_len):
    b, heads, tt, dq = k.shape
    dvp = vt.shape[2]
    dv = dvp - V_PAD
    n_tiles = tt // TM
    return pl.pallas_call(
        functools.partial(_attn_kernel, ctx_len=ctx_len, n_tiles=n_tiles),
        grid=(b, heads, n_tiles + 1),
        in_specs=[
            pl.BlockSpec((None, None, dq, TM), lambda b, h, i: (b, h, 0, jnp.minimum(i, n_tiles - 1))),
            pl.BlockSpec((None, None, tt, dq), lambda b, h, i: (b, h, 0, 0)),
            pl.BlockSpec((None, None, dvp, tt), lambda b, h, i: (b, h, 0, 0)),
        ],
        out_specs=pl.BlockSpec((None, TM, dv), lambda b, h, i: (b, jnp.maximum(i - 1, 0), h)),
        out_shape=jax.ShapeDtypeStruct((b, tt, heads * dv), BF16),
        scratch_shapes=[pltpu.VMEM((tt, TM), F32), pltpu.VMEM((tt, TM), F32),
                        pltpu.VMEM((1, TM), F32), pltpu.VMEM((1, TM), F32)],
        compiler_params=_cparams(("parallel", "parallel", "arbitrary"), V7X_VMEM_LIMIT),
        name="attention",
    )(qt, k, vt)


def _dft_kernel(c_ref, s_ref, x_ref, o_ref):
    half = x_ref.shape[1] // 2
    y = _dot(c_ref[...], x_ref[:, :half]) + _dot(s_ref[...], x_ref[:, half:])
    o_ref[...] = y.astype(o_ref.dtype)


def _dft_dense(xc, t_cos, t_sin):
    b, t, w2 = xc.shape
    return pl.pallas_call(
        _dft_kernel,
        grid=(b,),
        in_specs=[_const_spec((t, t)), _const_spec((t, t)), pl.BlockSpec((None, t, w2), lambda b: (b, 0, 0))],
        out_specs=pl.BlockSpec((None, t, w2 // 2), lambda b: (b, 0, 0)),
        out_shape=jax.ShapeDtypeStruct((b, t, w2 // 2), BF16),
        compiler_params=_cparams(("parallel",), V7X_VMEM_LIMIT),
        name="dft_dense",
    )(t_cos, t_sin, xc)


def _dft_split(t):
    n2 = 1 << ((t.bit_length()) // 2)
    return t // n2, n2


def _dft_stage1_kernel(x_ref, g_ref, zr_ref, zi_ref, *, n1, n2, cw):
    for t2 in range(n2):
        p = _dot(g_ref[t2], x_ref[:, t2 * 2 * cw:(t2 + 1) * 2 * cw])
        zr_ref[t2] = (p[:n1, :cw] - p[n1:, cw:]).astype(zr_ref.dtype)
        zi_ref[t2] = (p[:n1, cw:] + p[n1:, :cw]).astype(zi_ref.dtype)


def _dft_stage2_kernel(zr_ref, zi_ref, c_ref, s_ref, o_ref):
    o_ref[...] = (_dot(c_ref[...], zr_ref[...]) + _dot(s_ref[...], zi_ref[...])).astype(o_ref.dtype)


def _dft_two_stage(xc, tables):
    g, c2, s2 = tables
    b, t, w2 = xc.shape
    cw = w2 // 2
    n1, n2 = _dft_split(t)
    zr, zi = pl.pallas_call(
        functools.partial(_dft_stage1_kernel, n1=n1, n2=n2, cw=cw),
        grid=(b,),
        in_specs=[pl.BlockSpec((None, n1, n2 * w2), lambda b: (b, 0, 0)), _const_spec(g.shape)],
        out_specs=[pl.BlockSpec((None, n2, n1, cw), lambda b: (b, 0, 0, 0))] * 2,
        out_shape=[jax.ShapeDtypeStruct((b, n2, n1, cw), BF16)] * 2,
        compiler_params=_cparams(("parallel",), V7X_VMEM_LIMIT),
        name="dft_stage1",
    )(xc.reshape(b, n1, n2 * w2), g)
    y = pl.pallas_call(
        _dft_stage2_kernel,
        grid=(b,),
        in_specs=[pl.BlockSpec((None, n2, n1 * cw), lambda b: (b, 0, 0))] * 2 + [_const_spec((n2, n2))] * 2,
        out_specs=pl.BlockSpec((None, n2, n1 * cw), lambda b: (b, 0, 0)),
        out_shape=jax.ShapeDtypeStruct((b, n2, n1 * cw), BF16),
        compiler_params=_cparams(("parallel",), V7X_VMEM_LIMIT),
        name="dft_stage2",
    )(zr.reshape(b, n2, n1 * cw), zi.reshape(b, n2, n1 * cw), c2, s2)
    return y.reshape(b, t, cw)


def _odd_out_kernel(fc_ref, fl_ref, at_ref, x_ref, mod_ref, wf_ref, wa_ref, o_ref):
    is_ctx = pl.program_id(1) == 0
    fm = jnp.concatenate([jnp.where(is_ctx, fc_ref[s], fl_ref[s]) for s in range(NB)], axis=0)
    at = jnp.concatenate([at_ref[s] for s in range(NB)], axis=0)
    y = _dot(fm, wf_ref[...]) + _dot(at, wa_ref[...])
    for s in range(NB):
        o_ref[s] = x_ref[s] + mod_ref[s, 2:3, :] * y[s * TM:(s + 1) * TM]


def _odd_out(fm_ctx, fm_lat, attn, x, mod, w_out):
    b, tt, d = x.shape
    cw = fm_ctx.shape[-1]
    aw = attn.shape[-1]
    return pl.pallas_call(
        _odd_out_kernel,
        grid=(b // NB, tt // TM),
        in_specs=[
            pl.BlockSpec((NB, TM, cw), lambda b, i: (b, 0, 0)),
            pl.BlockSpec((NB, TM, cw), lambda b, i: (b, jnp.maximum(i - 1, 0), 0)),
            _rows_spec(aw),
            _rows_spec(d),
            _mod_spec(d, NB),
            _const_spec((cw, d)),
            _const_spec((aw, d)),
        ],
        out_specs=_rows_spec(d),
        out_shape=jax.ShapeDtypeStruct((b, tt, d), F32),
        compiler_params=_cparams(("parallel", "parallel"), V7X_VMEM_LIMIT),
        name="odd_out",
    )(fm_ctx, fm_lat, attn, x, mod, w_out[:cw], w_out[cw:])


def _ffn_kernel(xp_ref, x_ref, xn_ref, mod_ref, nw_ref, wg_ref, wv_ref, cw_ref, cb_ref, wd_ref, o_ref,
                *, first_tile, n_tiles):
    i = pl.program_id(1) + first_tile
    ext = TM + 2 * HALO
    row = lax.broadcasted_iota(jnp.int32, (ext, 1), 0)
    keep = jnp.logical_and(jnp.logical_or(row >= HALO, i > 1),
                           jnp.logical_or(row < HALO + TM, jnp.logical_and(i > 0, i < n_tiles - 1)))
    nw = nw_ref[...]
    hes = []
    for s in range(NB):
        xe = jnp.concatenate([xp_ref[s], x_ref[s], xn_ref[s]], axis=0)
        hes.append(jnp.where(keep, _modulate(xe, nw, mod_ref[s, 3:4, :], mod_ref[s, 4:5, :]), 0.0))
    ge = _dot(jnp.concatenate(hes, axis=0).astype(BF16), wg_ref[...])
    val = _dot(jnp.concatenate([h[HALO:HALO + TM] for h in hes], axis=0).astype(BF16), wv_ref[...])
    cw0, cw1, cw2 = cw_ref[0:1, :], cw_ref[1:2, :], cw_ref[2:3, :]
    gc = jnp.concatenate(
        [ge[s * ext + HALO - 1:s * ext + HALO - 1 + TM] * cw0 + ge[s * ext + HALO:s * ext + HALO + TM] * cw1
         + ge[s * ext + HALO + 1:s * ext + HALO + 1 + TM] * cw2 for s in range(NB)], axis=0) + cb_ref[...]
    y = _dot((_silu(gc) * val).astype(BF16), wd_ref[...])
    for s in range(NB):
        o_ref[s] = x_ref[s] + mod_ref[s, 5:6, :] * y[s * TM:(s + 1) * TM]


def _conv_ffn(x, mod, nw, w_up, conv_w, conv_b, wd, skip_ctx):
    b, tt, d = x.shape
    ff = wd.shape[0]
    n_tiles = tt // TM
    first = 1 if skip_ctx else 0
    r = TM // HALO
    nblk = tt // HALO
    return pl.pallas_call(
        functools.partial(_ffn_kernel, first_tile=first, n_tiles=n_tiles),
        grid=(b // NB, n_tiles - first),
        in_specs=[
            pl.BlockSpec((NB, HALO, d), lambda b, i: (b, jnp.maximum((i + first) * r - 1, 0), 0)),
            pl.BlockSpec((NB, TM, d), lambda b, i: (b, i + first, 0)),
            pl.BlockSpec((NB, HALO, d), lambda b, i: (b, jnp.minimum((i + first + 1) * r, nblk - 1), 0)),
            pl.BlockSpec((None, NB, 6, d), lambda b, i: (jnp.minimum(i + first, 1), b, 0, 0)),
            _const_spec((1, d)),
            pl.BlockSpec((d, ff), lambda b, i: (0, 0), pipeline_mode=pl.Buffered(1)),
            pl.BlockSpec((d, ff), lambda b, i: (0, 1), pipeline_mode=pl.Buffered(1)),
            _const_spec((CONV_W, ff)),
            _const_spec((1, ff)),
            _const_spec((ff, d)),
        ],
        out_specs=_rows_spec(d),
        out_shape=jax.ShapeDtypeStruct((b, tt - first * TM, d), F32),
        compiler_params=_cparams(("parallel", "parallel"), V7X_VMEM_LIMIT),
        name="conv_ffn",
    )(x, x, x, mod, nw, w_up, w_up, conv_w, conv_b, wd)


def _rope_swap_perm():
    q = MLA_ROPE // 4
    return np.concatenate([np.arange(q, 2 * q), np.arange(0, q), np.arange(3 * q, 4 * q), np.arange(2 * q, 3 * q)])


def _pad128(v):
    return jnp.pad(v, (0, 128 - v.shape[0])).reshape(1, 128)


def _odd_weights(w_in, qa_w, w_qb, kva_w, w_kvb, qn_w, kn_w, heads, cw, ql, kvl, cos_t, sin_t):
    d = w_in.shape[0]
    perm = _rope_swap_perm()
    nope, rope, qk = MLA_NOPE, MLA_ROPE, MLA_NOPE + MLA_ROPE
    o = cw + ql + kvl
    z = jnp.zeros((d, 128 - rope), w_in.dtype)
    kpe = w_in[:, o:o + rope]
    w_in_ext = jnp.concatenate([w_in[:, :o], kpe, z, kpe[:, perm], z], axis=1).astype(BF16)
    wq = w_qb.reshape(ql, heads, qk)
    zq = jnp.zeros((ql, heads, 128 - rope), w_qb.dtype)
    wq_rope = wq[:, :, nope:]
    w_q = jnp.concatenate([
        wq[:, :, :nope].reshape(ql, heads * 128),
        jnp.concatenate([wq_rope, zq], axis=2).reshape(ql, heads * 128),
        jnp.concatenate([wq_rope[:, :, perm], zq], axis=2).reshape(ql, heads * 128),
    ], axis=1).astype(BF16)
    wkv = w_kvb.reshape(kvl, heads, nope + MLA_V_DIM)
    w_kv = jnp.concatenate([wkv[:, :, :nope].reshape(kvl, heads * nope),
                            wkv[:, :, nope:].reshape(kvl, heads * MLA_V_DIM)], axis=1).astype(BF16)
    gd = cw // FOURIER_GROUPS
    jk = (np.arange(gd)[:, None] * np.arange(gd)[None, :]) % gd
    ang = 2.0 * np.pi * jk / gd
    eye = np.eye(FOURIER_GROUPS)
    bc = np.concatenate([np.kron(eye, np.cos(ang)), np.kron(eye, -np.sin(ang))], axis=1) / np.sqrt(gd)
    return {
        "w_in": w_in_ext, "qa_w": qa_w.reshape(1, ql), "kva_w": kva_w.reshape(1, kvl),
        "kp_w": _pad128(kn_w[nope:]), "kps_w": _pad128(kn_w[nope:][perm]),
        "w_q": w_q, "qn_w": qn_w[:nope].reshape(1, nope),
        "qp_w": _pad128(qn_w[nope:]), "qps_w": _pad128(qn_w[nope:][perm]),
        "w_kv": w_kv, "kn_w": kn_w[:nope].reshape(1, nope),
        "cos": cos_t, "sin": sin_t, "bc": jnp.asarray(bc, F32).astype(BF16),
    }


def _rope_tables(t_lat, ctx_len):
    rows = t_lat // GRID_W
    row = jnp.repeat(jnp.arange(rows), GRID_W)
    col = jnp.tile(jnp.arange(GRID_W), rows)
    r_axis = MLA_ROPE // 2
    inv_freq = ROPE_THETA ** (-jnp.arange(0, r_axis, 2, dtype=F32) / r_axis)
    ang = jnp.stack([row, col], axis=-1).astype(F32)[:, :, None] * inv_freq
    cos, sin = jnp.cos(ang), jnp.sin(ang)
    cos64 = jnp.concatenate([cos[:, 0], cos[:, 0], cos[:, 1], cos[:, 1]], axis=-1)
    sin64 = jnp.concatenate([-sin[:, 0], sin[:, 0], -sin[:, 1], sin[:, 1]], axis=-1)
    cos64 = jnp.concatenate([jnp.ones((ctx_len, MLA_ROPE), F32), cos64], axis=0)
    sin64 = jnp.concatenate([jnp.zeros((ctx_len, MLA_ROPE), F32), sin64], axis=0)
    pad = ((0, 0), (0, 128 - MLA_ROPE))
    return jnp.pad(cos64, pad), jnp.pad(sin64, pad)


def _cos_sin(phase, period, scale):
    ang = (phase % period).astype(F32) * (2.0 * np.pi / period)
    return jnp.cos(ang) * scale, jnp.sin(ang) * scale


def _dft_dense_tables(t):
    idx = jnp.arange(t, dtype=jnp.int32)
    c, s = _cos_sin(idx[:, None] * idx[None, :], t, 1.0 / np.sqrt(t))
    return c.astype(BF16), s.astype(BF16)


def _dft_two_stage_tables(t):
    n1, n2 = _dft_split(t)
    u1 = jnp.arange(n1, dtype=jnp.int32)
    pos = jnp.arange(t, dtype=jnp.int32).reshape(n1, n2)
    c, s = _cos_sin(pos.T[:, None, :] * u1[None, :, None], t, 1.0 / np.sqrt(n1))
    g = jnp.concatenate([c, -s], axis=1).astype(BF16)
    i2 = jnp.arange(n2, dtype=jnp.int32)
    c2, s2 = _cos_sin(i2[:, None] * i2[None, :], n2, 1.0 / np.sqrt(n2))
    return g, c2.astype(BF16), s2.astype(BF16)


def kernel(x, c, ctx, c_ctx, ada_w, ada_b, norm_mix_w, norm_ffn_w, ev_w_in, ev_lb_logits, ev_onorm_w, ev_vnorm_w, ev_ws, ev_bs, ev_w_out, od_w_in, od_qa_norm_w, od_w_qb, od_kva_norm_w, od_w_kvb, od_q_norm_w, od_k_norm_w, od_w_out, ffn_w_up, ffn_conv_w, ffn_conv_b, ffn_w_down):
    bsz, t_lat, d = x.shape
    ctx_len = ctx.shape[1]
    depth = ada_w.shape[0]
    assert ctx_len == TM and t_lat % ATTN_KC == 0 and t_lat % GRID_W == 0 and bsz % NB == 0
    a_width = ev_lb_logits.shape[-1]
    ql, kvl = od_qa_norm_w.shape[-1], od_kva_norm_w.shape[-1]
    cw = od_w_in.shape[-1] - ql - kvl - MLA_ROPE
    heads = od_w_qb.shape[-1] // (MLA_NOPE + MLA_ROPE)
    d_ff = ffn_w_down.shape[1]

    pad_rows = (-(bsz + 1)) % 8
    cvec = jnp.concatenate([c, c_ctx[None, :], jnp.zeros((pad_rows, d), F32)], axis=0)
    mods = _ada_tables(cvec, ada_w, ada_b)
    mod_lat = mods[:, :bsz].reshape(depth, 1, bsz, 6, d)
    mod_ctx = jnp.broadcast_to(mods[:, bsz].reshape(depth, 1, 1, 6, d), (depth, 1, bsz, 6, d))
    mods = jnp.concatenate([mod_ctx, mod_lat], axis=1)

    lb_p = jax.nn.softmax(ev_lb_logits.astype(F32), axis=0)
    lbs = jnp.cumsum(lb_p, axis=0) - lb_p[0]
    cos_t, sin_t = _rope_tables(t_lat, ctx_len)
    dft_lat = dft_ctx = None

    xs = (ctx, x)
    for l in range(depth):
        last = l == depth - 1
        mod = mods[l]
        nmw = norm_mix_w[l].reshape(1, d)
        if l % 2 == 0:
            e = l // 2
            p = _even_in(xs, ctx_len + t_lat, mod, nmw, ev_w_in[e].astype(BF16))
            o_fwd, o_bwd = _hgrn_scan(p, lbs[e], ctx_len, a_width)
            bw = d - a_width
            gd = bw // ev_ws.shape[1]
            bsb = jnp.repeat(ev_bs[e].T, gd, axis=1)
            xs = _even_out(o_fwd, o_bwd, ev_onorm_w[e], p, xs, mod, ev_vnorm_w[e].reshape(1, bw),
                           ev_ws[e].astype(BF16), bsb, ev_w_out[e].astype(BF16), a_width)
        else:
            o = l // 2
            if dft_lat is None:
                dft_lat, dft_ctx = _dft_two_stage_tables(t_lat), _dft_dense_tables(ctx_len)
            wts = _odd_weights(od_w_in[o], od_qa_norm_w[o], od_w_qb[o], od_kva_norm_w[o], od_w_kvb[o],
                               od_q_norm_w[o], od_k_norm_w[o], heads, cw, ql, kvl, cos_t, sin_t)
            xc_ctx, xc_lat, qt, k, vt = _odd_in(xs, mod, nmw, wts, heads, cw, ql, kvl)
            attn = _attention(qt, k, vt, ctx_len)
            fm_ctx = _dft_dense(xc_ctx, *dft_ctx)
            fm_lat = _dft_two_stage(xc_lat, dft_lat)
            xs = _odd_out(fm_ctx, fm_lat, attn, xs, mod, od_w_out[o].astype(BF16))
        xs = _conv_ffn(xs, mod, norm_ffn_w[l].reshape(1, d), ffn_w_up[l].astype(BF16), ffn_conv_w[l],
                       ffn_conv_b[l].reshape(1, d_ff), ffn_w_down[l].astype(BF16), skip_ctx=last)
    return xs
```

```python
import functools

import numpy as np
import jax
import jax.numpy as jnp
from jax import lax
from jax.experimental import pallas as pl
from jax.experimental.pallas import tpu as pltpu

F32 = jnp.float32
BF16 = jnp.bfloat16

EPS = 1e-6
F_MIN = 1e-6
GRID_W = 64
ROPE_THETA = 10000.0
HGRN_HEAD_DIM = 128
HGRN_CHUNK = 64
HGRN_STEP_CHUNKS = 4
MLP_CHUNK = 128
MLA_NOPE = 128
MLA_ROPE = 64
MLA_V_DIM = 128
V_PAD = 16
FOURIER_GROUPS = 4
CONV_W = 3

TM = 256
NB = 2
HALO = 8
ATTN_KC = 512
ATTN_HEADS_PER_STEP = 3
V7X_VMEM_LIMIT = 56 * 1024 * 1024


def _cparams(sem, vmem=None):
    return pltpu.CompilerParams(dimension_semantics=sem, vmem_limit_bytes=vmem)


def _dot(a, b):
    return jnp.dot(a, b, preferred_element_type=F32)


def _dot_nt(a, b):
    return lax.dot_general(a, b, (((1,), (1,)), ((), ())), preferred_element_type=F32)


def _dot_tn(a, b):
    return lax.dot_general(a, b, (((0,), (0,)), ((), ())), preferred_element_type=F32)


def _silu(x):
    return x * jax.nn.sigmoid(x)


def _gelu_tanh(x):
    return 0.5 * x * (1.0 + jnp.tanh(0.7978845608028654 * (x + 0.044715 * (x * x * x))))


def _rms(x, n=None):
    n = x.shape[-1] if n is None else n
    return x * lax.rsqrt(jnp.sum(x * x, axis=-1, keepdims=True) * (1.0 / n) + EPS)


def _modulate(x, nw, shift, scale):
    return (_rms(x) * nw) * (1.0 + scale) + shift


def _mod_spec(d, nb):
    return pl.BlockSpec((None, nb, 6, d), lambda b, i: (jnp.minimum(i, 1), b, 0, 0))


def _rows_spec(cols, colblk=0):
    return pl.BlockSpec((NB, TM, cols), lambda b, i: (b, i, colblk))


def _const_spec(shape):
    nd = len(shape)
    return pl.BlockSpec(shape, lambda *_: (0,) * nd, pipeline_mode=pl.Buffered(1))


def _ada_kernel(c_ref, w_ref, b_ref, o_ref):
    s = _silu(c_ref[...]).astype(BF16)
    o_ref[...] = _dot(s, w_ref[...].astype(BF16)) + b_ref[...]


def _ada_tables(cvec, ada_w, ada_b):
    depth, d, n = ada_w.shape
    rows = cvec.shape[0]
    tn = 1536
    return pl.pallas_call(
        _ada_kernel,
        grid=(depth, n // tn),
        in_specs=[
            pl.BlockSpec((rows, d), lambda l, j: (0, 0)),
            pl.BlockSpec((None, d, tn), lambda l, j: (l, 0, j)),
            pl.BlockSpec((None, 1, tn), lambda l, j: (l, 0, j)),
        ],
        out_specs=pl.BlockSpec((None, rows, tn), lambda l, j: (l, 0, j)),
        out_shape=jax.ShapeDtypeStruct((depth, rows, n), F32),
        compiler_params=_cparams(("parallel", "parallel"), V7X_VMEM_LIMIT),
        name="ada_tables",
    )(cvec, ada_w, ada_b.reshape(depth, 1, n))


def _stream_specs(xs, d):
    if isinstance(xs, tuple):
        return ([pl.BlockSpec((NB, TM, d), lambda b, i: (b, 0, 0)),
                 pl.BlockSpec((NB, TM, d), lambda b, i: (b, jnp.maximum(i - 1, 0), 0))], list(xs))
    return [_rows_spec(d)], [xs]


def _stream_rows(x_refs, s):
    if len(x_refs) == 1:
        return x_refs[0][s]
    return jnp.where(pl.program_id(1) == 0, x_refs[0][s], x_refs[1][s])


def _even_in_kernel(*refs, n_x):
    x_refs = refs[:n_x]
    mod_ref, nw_ref, w_ref, o_ref = refs[n_x:]
    nw = nw_ref[...]
    h = jnp.concatenate([_modulate(_stream_rows(x_refs, s), nw, mod_ref[s, 0:1, :], mod_ref[s, 1:2, :]).astype(BF16)
                         for s in range(NB)], axis=0)
    p = _dot(h, w_ref[...])
    for s in range(NB):
        o_ref[s] = p[s * TM:(s + 1) * TM]


def _even_in(xs, tt, mod, nw, w_in):
    d, n = w_in.shape
    x_specs, x_args = _stream_specs(xs, d)
    b = x_args[0].shape[0]
    return pl.pallas_call(
        functools.partial(_even_in_kernel, n_x=len(x_args)),
        grid=(b // NB, tt // TM),
        in_specs=x_specs + [
            _mod_spec(d, NB),
            _const_spec((1, d)),
            _const_spec((d, n)),
        ],
        out_specs=_rows_spec(n),
        out_shape=jax.ShapeDtypeStruct((b, tt, n), F32),
        compiler_params=_cparams(("parallel", "parallel"), V7X_VMEM_LIMIT),
        name="even_in",
    )(*x_args, mod, nw, w_in)


def _hgrn_consts(reverse):
    c = HGRN_CHUNK
    idx = np.arange(c)
    if not reverse:
        tri = (idx[None, :] <= idx[:, None]).astype(np.float32)
    else:
        tri = (idx[None, :] >= idx[:, None]).astype(np.float32)
    masks = []
    for w in (32, 16, 8):
        grp = idx // (2 * w)
        qrow = (idx % (2 * w) >= w) if not reverse else (idx % (2 * w) < w)
        masks.append((grp[:, None] == grp[None, :]) & qrow[:, None] & (~qrow)[None, :])
    blk = idx // 8
    causal = (idx[None, :] <= idx[:, None]) if not reverse else (idx[None, :] >= idx[:, None])
    masks.append((blk[:, None] == blk[None, :]) & causal)
    return np.kron(np.eye(HGRN_STEP_CHUNKS), tri).astype(np.float32), np.stack(masks).astype(np.float32)


def _hgrn_prepare(pq_ref, pv_ref, pz_ref, lb, tri_ref, width):
    f = jnp.maximum(lb + (1.0 - lb) * jax.nn.sigmoid(pz_ref[...]), F_MIN)
    g = jnp.log(f)
    g1 = g.astype(BF16)
    r1 = g - g1.astype(F32)
    g2 = r1.astype(BF16)
    g3 = (r1 - g2.astype(F32)).astype(BF16)
    b3 = _dot(tri_ref[...], jnp.concatenate([g1, g2, g3], axis=1))
    b_all = b3[:, :width] + b3[:, width:2 * width] + b3[:, 2 * width:]
    return _silu(pq_ref[...]), 1.0 - f, pv_ref[...], b_all


def _hgrn_chunk(feats, ci, reverse, masks_ref, st, heads):
    c, hd = HGRN_CHUNK, HGRN_HEAD_DIM
    width = heads * hd
    q, kk, v, b_all = feats
    rs = slice(ci * c, (ci + 1) * c)
    b = b_all[rs]

    def ref_rows(rows, n):
        return jnp.concatenate([jnp.broadcast_to(b[r:r + 1, :], (n, width)) for r in rows], axis=0)

    e_lv = []
    for w in (32, 16, 8):
        mids = [gi * 2 * w + (w - 1 if not reverse else w) for gi in range(c // (2 * w))]
        e_lv.append(jnp.exp(-jnp.abs(b - ref_rows(mids, 2 * w))))
    anchor = ref_rows([8 * m + (3 if not reverse else 4) for m in range(c // 8)], 8)
    e_dq = jnp.exp(b - anchor)
    e_dk = jnp.exp(anchor - b)
    last = c - 1 if not reverse else 0
    e_in = jnp.exp(b)
    e_out = jnp.exp(ref_rows([last], c) - b)
    parts = []
    for h in range(heads):
        sl = slice(h * hd, (h + 1) * hd)
        qh, kh = q[rs, sl], kk[rs, sl]
        a = None
        for lv in range(3):
            ew = e_lv[lv][:, sl]
            t = _dot_nt((qh * ew).astype(BF16), (kh * ew).astype(BF16)) * masks_ref[lv]
            a = t if a is None else a + t
        d8 = _dot_nt((qh * e_dq[:, sl]).astype(BF16), (kh * e_dk[:, sl]).astype(BF16))
        a = a + jnp.where(masks_ref[3] > 0.5, d8, 0.0)
        vb = v[rs, sl].astype(BF16)
        qhat = (qh * e_in[:, sl]).astype(BF16)
        khat = (kh * e_out[:, sl]).astype(BF16)
        parts.append((a.astype(BF16), vb, qhat, _dot_tn(vb, khat), e_in[last:last + 1, sl]))
    heads_out = []
    for h, (a, vb, qhat, upd, e_last) in enumerate(parts):
        heads_out.append(_dot(a, vb) + _dot_nt(qhat, st[h].astype(BF16)))
        st[h] = st[h] * e_last + upd
    return jnp.concatenate(heads_out, axis=1)


def _hgrn_kernel(fq_ref, fv_ref, fz_ref, bq_ref, bv_ref, bz_ref, lb_ref, ftri_ref, btri_ref, fmask_ref, bmask_ref,
                 of_ref, ob_ref, fst_ref, bst_ref, *, heads):
    nck = HGRN_STEP_CHUNKS
    width = heads * HGRN_HEAD_DIM

    @pl.when(pl.program_id(1) == 0)
    def _():
        fst_ref[...] = jnp.zeros_like(fst_ref)
        bst_ref[...] = jnp.zeros_like(bst_ref)

    ffeats = _hgrn_prepare(fq_ref, fv_ref, fz_ref, lb_ref[0:1, :], ftri_ref, width)
    bfeats = _hgrn_prepare(bq_ref, bv_ref, bz_ref, lb_ref[1:2, :], btri_ref, width)
    fst = [fst_ref[h] for h in range(heads)]
    bst = [bst_ref[h] for h in range(heads)]
    fouts, bouts = [None] * nck, [None] * nck
    for k in range(nck):
        fouts[k] = _hgrn_chunk(ffeats, k, False, fmask_ref, fst, heads)
        bouts[nck - 1 - k] = _hgrn_chunk(bfeats, nck - 1 - k, True, bmask_ref, bst, heads)
    for h in range(heads):
        fst_ref[h] = fst[h]
        bst_ref[h] = bst[h]
    of_ref[...] = jnp.concatenate(fouts, axis=0)
    ob_ref[...] = jnp.concatenate(bouts, axis=0)


def _hgrn_scan(p, lbs, ctx_len, width):
    b, tt, _ = p.shape
    rows = HGRN_CHUNK * HGRN_STEP_CHUNKS
    heads = width // HGRN_HEAD_DIM
    nb, nbc = tt // rows, ctx_len // rows
    ftri, fmask = _hgrn_consts(False)
    btri, bmask = _hgrn_consts(True)

    def bblk(j):
        return jnp.where(j < nbc, nbc - 1 - j, nb - 1 - (j - nbc))

    def fcol(k):
        return pl.BlockSpec((None, rows, width), lambda b, j: (b, j, k))

    def bcol(k):
        return pl.BlockSpec((None, rows, width), lambda b, j: (b, bblk(j), k))

    consts = [lbs, jnp.asarray(ftri, BF16), jnp.asarray(btri, BF16), jnp.asarray(fmask, F32), jnp.asarray(bmask, F32)]
    state = pltpu.VMEM((heads, HGRN_HEAD_DIM, HGRN_HEAD_DIM), F32)
    return pl.pallas_call(
        functools.partial(_hgrn_kernel, heads=heads),
        grid=(b, nb),
        in_specs=[fcol(0), fcol(1), fcol(2), bcol(0), bcol(1), bcol(3)] + [_const_spec(a.shape) for a in consts],
        out_specs=[fcol(0), bcol(0)],
        out_shape=[jax.ShapeDtypeStruct((b, tt, width), F32)] * 2,
        scratch_shapes=[state, state],
        compiler_params=_cparams(("parallel", "arbitrary"), V7X_VMEM_LIMIT),
        name="hgrn_scan",
    )(p, p, p, p, p, p, *consts)


def _even_out_kernel(*refs, groups, n_x):
    of_ref, ob_ref, pg_ref, pu_ref, pv_ref = refs[:5]
    x_refs = refs[5:5 + n_x]
    mod_ref, onw_ref, vnw_ref, ws_ref, bsb_ref, wa_ref, wb_ref, o_ref = refs[5 + n_x:]
    gd = pu_ref.shape[2] // groups
    hd = HGRN_HEAD_DIM
    onw = onw_ref[...]
    ms, yas = [], []
    for s in range(NB):
        o = of_ref[s] + ob_ref[s]
        yn = jnp.concatenate([_rms(o[:, h * hd:(h + 1) * hd]) * onw for h in range(o.shape[1] // hd)], axis=1)
        yas.append((yn * _silu(pg_ref[s])).astype(BF16))
        for n in range(TM // MLP_CHUNK):
            rows = slice(n * MLP_CHUNK, (n + 1) * MLP_CHUNK)
            u = _gelu_tanh(pu_ref[s, rows, :])
            v = _gelu_tanh(pv_ref[s, rows, :])
            parts = []
            for g in range(groups):
                cs = slice(g * gd, (g + 1) * gd)
                vg = (_rms(v[:, cs]) * vnw_ref[:, cs]).astype(BF16)
                sv = _dot(ws_ref[g], vg) + bsb_ref[:, cs]
                parts.append(u[:, cs] * sv)
            ms.append(jnp.concatenate(parts, axis=1).astype(BF16))
    m = jnp.concatenate(ms, axis=0)
    y = _dot(jnp.concatenate(yas, axis=0), wa_ref[...]) + _dot(m, wb_ref[...])
    for s in range(NB):
        o_ref[s] = _stream_rows(x_refs, s) + mod_ref[s, 2:3, :] * y[s * TM:(s + 1) * TM]


def _even_out(o_fwd, o_bwd, onw, p, xs, mod, vnw, ws, bsb, w_out, a_width):
    b, tt, _ = p.shape
    d = w_out.shape[1]
    bw = d - a_width
    groups = ws.shape[0]
    ucol = 5 * a_width // bw
    x_specs, x_args = _stream_specs(xs, d)
    return pl.pallas_call(
        functools.partial(_even_out_kernel, groups=groups, n_x=len(x_args)),
        grid=(b // NB, tt // TM),
        in_specs=[
            _rows_spec(a_width),
            _rows_spec(a_width),
            _rows_spec(a_width, 4),
            _rows_spec(bw, ucol),
            _rows_spec(bw, ucol + 1)] + x_specs + [
            _mod_spec(d, NB),
            _const_spec((1, HGRN_HEAD_DIM)),
            _const_spec((1, bw)),
            _const_spec(ws.shape),
            _const_spec(bsb.shape),
            _const_spec((a_width, d)),
            _const_spec((bw, d)),
        ],
        out_specs=_rows_spec(d),
        out_shape=jax.ShapeDtypeStruct((b, tt, d), F32),
        compiler_params=_cparams(("parallel", "parallel"), V7X_VMEM_LIMIT),
        name="even_out",
    )(o_fwd, o_bwd, p, p, p, *x_args, mod, onw.reshape(1, HGRN_HEAD_DIM), vnw, ws, bsb,
      w_out[:a_width], w_out[a_width:])


def _odd_in_kernel(x_ref, mod_ref, nw_ref, win_ref, qaw_ref, kvaw_ref, kpw_ref, kpsw_ref, wq_ref, qnw_ref,
                   qpw_ref, qpsw_ref, wkv_ref, knw_ref, cos_ref, sin_ref, bc_ref,
                   xc_ctx_ref, xc_lat_ref, qt_ref, k_ref, vt_ref, *, heads, cw, ql, kvl, scale):
    cos, sin = cos_ref[...], sin_ref[...]
    rope = MLA_ROPE
    o = cw + ql + kvl
    hw = heads * 128
    ones_pad = (lax.broadcasted_iota(jnp.int32, (V_PAD, TM), 0) == 0).astype(F32)
    is_ctx = pl.program_id(1) == 0

    def project(s):
        h = _modulate(x_ref[s], nw_ref[...], mod_ref[s, 0:1, :], mod_ref[s, 1:2, :])
        return _dot(h.astype(BF16), win_ref[...])

    def latents(s, p):
        xc = _dot(p[:, :cw].astype(BF16), bc_ref[...]).astype(xc_lat_ref.dtype)
        xc_lat_ref[s] = xc

        @pl.when(is_ctx)
        def _():
            xc_ctx_ref[s] = xc
        q_lat = (_rms(p[:, cw:cw + ql]) * qaw_ref[...]).astype(BF16)
        kv_lat = (_rms(p[:, cw + ql:cw + ql + kvl]) * kvaw_ref[...]).astype(BF16)
        kp, kps = p[:, o:o + 128], p[:, o + 128:o + 256]
        kpr = lax.rsqrt(jnp.sum(kp * kp, axis=-1, keepdims=True) * (1.0 / rope) + EPS)
        k_pe = (kp * kpw_ref[...] * cos + kps * kpsw_ref[...] * sin) * kpr
        qf = _dot(q_lat, wq_ref[...])
        kvf = _dot(kv_lat, wkv_ref[...])
        return qf, kvf, k_pe

    def heads_out(s, qf, kvf, k_pe):
        for hh in range(heads):
            cs = slice(hh * 128, (hh + 1) * 128)
            qn = _rms(qf[:, cs]) * qnw_ref[...]
            qp = qf[:, hw + hh * 128:hw + (hh + 1) * 128]
            qps = qf[:, 2 * hw + hh * 128:2 * hw + (hh + 1) * 128]
            qpr = lax.rsqrt(jnp.sum(qp * qp, axis=-1, keepdims=True) * (1.0 / rope) + EPS)
            q_pe = (qp * qpw_ref[...] * cos + qps * qpsw_ref[...] * sin) * qpr
            qt_ref[s, hh] = (jnp.concatenate([qn, q_pe], axis=1) * scale).T.astype(qt_ref.dtype)
            kn = _rms(kvf[:, cs]) * knw_ref[...]
            k_ref[s, hh] = jnp.concatenate([kn, k_pe], axis=1).astype(k_ref.dtype)
            vt_ref[s, hh] = jnp.concatenate([kvf[:, hw + hh * 128:hw + (hh + 1) * 128].T, ones_pad],
                                            axis=0).astype(vt_ref.dtype)

    mid = latents(0, project(0))
    for s in range(1, NB):
        p_next = project(s)
        heads_out(s - 1, *mid)
        mid = latents(s, p_next)
    heads_out(NB - 1, *mid)


def _odd_in(x, mod, nw, wts, heads, cw, ql, kvl):
    b, tt, d = x.shape
    scale = float(MLA_NOPE + MLA_ROPE) ** -0.5 * float(np.log2(np.e))
    consts = [wts[k] for k in ("w_in", "qa_w", "kva_w", "kp_w", "kps_w", "w_q", "qn_w", "qp_w", "qps_w",
                               "w_kv", "kn_w")]
    row128 = pl.BlockSpec((TM, 128), lambda b, i: (i, 0))
    in_specs = ([_rows_spec(d), _mod_spec(d, NB), _const_spec((1, d))]
                + [_const_spec(a.shape) for a in consts] + [row128, row128, _const_spec(wts["bc"].shape)])
    return pl.pallas_call(
        functools.partial(_odd_in_kernel, heads=heads, cw=cw, ql=ql, kvl=kvl, scale=scale),
        grid=(b // NB, tt // TM),
        in_specs=in_specs,
        out_specs=[
            pl.BlockSpec((NB, TM, 2 * cw), lambda b, i: (b, 0, 0)),
            pl.BlockSpec((NB, TM, 2 * cw), lambda b, i: (b, jnp.maximum(i - 1, 0), 0)),
            pl.BlockSpec((NB, heads, 256, TM), lambda b, i: (b, 0, 0, i)),
            pl.BlockSpec((NB, heads, TM, 256), lambda b, i: (b, 0, i, 0)),
            pl.BlockSpec((NB, heads, MLA_V_DIM + V_PAD, TM), lambda b, i: (b, 0, 0, i)),
        ],
        out_shape=[
            jax.ShapeDtypeStruct((b, TM, 2 * cw), BF16),
            jax.ShapeDtypeStruct((b, tt - TM, 2 * cw), BF16),
            jax.ShapeDtypeStruct((b, heads, 256, tt), BF16),
            jax.ShapeDtypeStruct((b, heads, tt, 256), BF16),
            jax.ShapeDtypeStruct((b, heads, MLA_V_DIM + V_PAD, tt), BF16),
        ],
        compiler_params=_cparams(("parallel", "arbitrary"), V7X_VMEM_LIMIT),
        name="odd_in",
    )(x, mod, nw, *consts, wts["cos"], wts["sin"], wts["bc"])


def _attn_kernel(qt_ref, k_ref, vt_ref, o_ref, sa_ref, sb_ref, ma_ref, mb_ref, *, ctx_len, n_tiles):
    i = pl.program_id(2)
    n_all = k_ref.shape[1]
    nh = k_ref.shape[0]
    bufs = ((sa_ref, ma_ref), (sb_ref, mb_ref))

    def chunks(n_keys):
        return [(st, min(ATTN_KC, n_keys - st)) for st in range(0, n_keys, ATTN_KC)]

    def run(step_parity, keys1, keys2):
        s_w, m_w = bufs[step_parity]
        s_r, m_r = bufs[1 - step_parity]
        c1 = chunks(keys1) if keys1 else []
        c2 = chunks(keys2) if keys2 else []
        if keys1:
            qts = [qt_ref[hh] for hh in range(nh)]
        if keys2:
            m_prev = [m_r[hh] for hh in range(nh)]
        m, acc = [None] * nh, [None] * nh
        for idx in range(max(len(c1), len(c2))):
            for hh in range(nh):
                if idx < len(c1):
                    st, sz = c1[idx]
                    s = _dot(k_ref[hh, st:st + sz, :], qts[hh])
                    s_w[hh, st:st + sz, :] = s
                    cm = jnp.max(s, axis=0, keepdims=True)
                    m[hh] = cm if m[hh] is None else jnp.maximum(m[hh], cm)
                if idx < len(c2):
                    st, sz = c2[idx]
                    p = jnp.exp2(s_r[hh, st:st + sz, :] - m_prev[hh]).astype(BF16)
                    ca = _dot(vt_ref[hh, :, st:st + sz], p)
                    acc[hh] = ca if acc[hh] is None else acc[hh] + ca
        dv = o_ref.shape[1] // nh
        for hh in range(nh):
            if keys1:
                m_w[hh] = m[hh]
            if keys2:
                o_ref[:, hh * dv:(hh + 1) * dv] = (acc[hh][:dv] * (1.0 / acc[hh][dv:dv + 1])).T.astype(o_ref.dtype)

    @pl.when(i == 0)
    def _():
        run(0, ctx_len, None)

    @pl.when(i == 1)
    def _():
        run(1, n_all, ctx_len)

    for parity in (0, 1):
        @pl.when(jnp.logical_and(jnp.logical_and(i >= 2, i < n_tiles), i % 2 == parity))
        def _():
            run(parity, n_all, n_all)

    @pl.when(i == n_tiles)
    def _():
        run(n_tiles % 2, None, n_all)


def _attention(qt, k, vt, ctx_len):
    b, heads, tt, dq = k.shape
    dvp = vt.shape[2]
    dv = dvp - V_PAD
    n_tiles = tt // TM
    nh = ATTN_HEADS_PER_STEP
    assert heads % nh == 0
    return pl.pallas_call(
        functools.partial(_attn_kernel, ctx_len=ctx_len, n_tiles=n_tiles),
        grid=(b, heads // nh, n_tiles + 1),
        in_specs=[
            pl.BlockSpec((None, nh, dq, TM), lambda b, h, i: (b, h, 0, jnp.minimum(i, n_tiles - 1))),
            pl.BlockSpec((None, nh, tt, dq), lambda b, h, i: (b, h, 0, 0)),
            pl.BlockSpec((None, nh, dvp, tt), lambda b, h, i: (b, h, 0, 0)),
        ],
        out_specs=pl.BlockSpec((None, TM, nh * dv), lambda b, h, i: (b, jnp.maximum(i - 1, 0), h)),
        out_shape=jax.ShapeDtypeStruct((b, tt, heads * dv), BF16),
        scratch_shapes=[pltpu.VMEM((nh, tt, TM), F32), pltpu.VMEM((nh, tt, TM), F32),
                        pltpu.VMEM((nh, 1, TM), F32), pltpu.VMEM((nh, 1, TM), F32)],
        compiler_params=_cparams(("parallel", "parallel", "arbitrary"), V7X_VMEM_LIMIT),
        name="attention",
    )(qt, k, vt)


def _dft_kernel(c_ref, s_ref, x_ref, o_ref):
    half = x_ref.shape[1] // 2
    y = _dot(c_ref[...], x_ref[:, :half]) + _dot(s_ref[...], x_ref[:, half:])
    o_ref[...] = y.astype(o_ref.dtype)


def _dft_dense(xc, t_cos, t_sin):
    b, t, w2 = xc.shape
    return pl.pallas_call(
        _dft_kernel,
        grid=(b,),
        in_specs=[_const_spec((t, t)), _const_spec((t, t)), pl.BlockSpec((None, t, w2), lambda b: (b, 0, 0))],
        out_specs=pl.BlockSpec((None, t, w2 // 2), lambda b: (b, 0, 0)),
        out_shape=jax.ShapeDtypeStruct((b, t, w2 // 2), BF16),
        compiler_params=_cparams(("parallel",), V7X_VMEM_LIMIT),
        name="dft_dense",
    )(t_cos, t_sin, xc)


def _dft_split(t):
    n2 = 1 << ((t.bit_length()) // 2)
    return t // n2, n2


def _dft_stage1_kernel(x_ref, g_ref, zr_ref, zi_ref, *, n1, n2, cw):
    for t2 in range(n2):
        p = _dot(g_ref[t2], x_ref[:, t2 * 2 * cw:(t2 + 1) * 2 * cw])
        zr_ref[t2] = (p[:n1, :cw] - p[n1:, cw:]).astype(zr_ref.dtype)
        zi_ref[t2] = (p[:n1, cw:] + p[n1:, :cw]).astype(zi_ref.dtype)


def _dft_stage2_kernel(zr_ref, zi_ref, c_ref, s_ref, o_ref):
    o_ref[...] = (_dot(c_ref[...], zr_ref[...]) + _dot(s_ref[...], zi_ref[...])).astype(o_ref.dtype)


def _dft_two_stage(xc, tables):
    g, c2, s2 = tables
    b, t, w2 = xc.shape
    cw = w2 // 2
    n1, n2 = _dft_split(t)
    zr, zi = pl.pallas_call(
        functools.partial(_dft_stage1_kernel, n1=n1, n2=n2, cw=cw),
        grid=(b,),
        in_specs=[pl.BlockSpec((None, n1, n2 * w2), lambda b: (b, 0, 0)), _const_spec(g.shape)],
        out_specs=[pl.BlockSpec((None, n2, n1, cw), lambda b: (b, 0, 0, 0))] * 2,
        out_shape=[jax.ShapeDtypeStruct((b, n2, n1, cw), BF16)] * 2,
        compiler_params=_cparams(("parallel",), V7X_VMEM_LIMIT),
        name="dft_stage1",
    )(xc.reshape(b, n1, n2 * w2), g)
    y = pl.pallas_call(
        _dft_stage2_kernel,
        grid=(b,),
        in_specs=[pl.BlockSpec((None, n2, n1 * cw), lambda b: (b, 0, 0))] * 2 + [_const_spec((n2, n2))] * 2,
        out_specs=pl.BlockSpec((None, n2, n1 * cw), lambda b: (b, 0, 0)),
        out_shape=jax.ShapeDtypeStruct((b, n2, n1 * cw), BF16),
        compiler_params=_cparams(("parallel",), V7X_VMEM_LIMIT),
        name="dft_stage2",
    )(zr.reshape(b, n2, n1 * cw), zi.reshape(b, n2, n1 * cw), c2, s2)
    return y.reshape(b, t, cw)


def _odd_out_kernel(fc_ref, fl_ref, at_ref, x_ref, mod_ref, wf_ref, wa_ref, o_ref):
    is_ctx = pl.program_id(1) == 0
    fm = jnp.concatenate([jnp.where(is_ctx, fc_ref[s], fl_ref[s]) for s in range(NB)], axis=0)
    at = jnp.concatenate([at_ref[s] for s in range(NB)], axis=0)
    y = _dot(fm, wf_ref[...]) + _dot(at, wa_ref[...])
    for s in range(NB):
        o_ref[s] = x_ref[s] + mod_ref[s, 2:3, :] * y[s * TM:(s + 1) * TM]


def _odd_out(fm_ctx, fm_lat, attn, x, mod, w_out):
    b, tt, d = x.shape
    cw = fm_ctx.shape[-1]
    aw = attn.shape[-1]
    return pl.pallas_call(
        _odd_out_kernel,
        grid=(b // NB, tt // TM),
        in_specs=[
            pl.BlockSpec((NB, TM, cw), lambda b, i: (b, 0, 0)),
            pl.BlockSpec((NB, TM, cw), lambda b, i: (b, jnp.maximum(i - 1, 0), 0)),
            _rows_spec(aw),
            _rows_spec(d),
            _mod_spec(d, NB),
            _const_spec((cw, d)),
            _const_spec((aw, d)),
        ],
        out_specs=_rows_spec(d),
        out_shape=jax.ShapeDtypeStruct((b, tt, d), F32),
        compiler_params=_cparams(("parallel", "parallel"), V7X_VMEM_LIMIT),
        name="odd_out",
    )(fm_ctx, fm_lat, attn, x, mod, w_out[:cw], w_out[cw:])


def _ffn_kernel(xp_ref, x_ref, xn_ref, mod_ref, nw_ref, wg_ref, wv_ref, cw_ref, cb_ref, wd_ref, o_ref,
                *, first_tile, n_tiles):
    i = pl.program_id(1) + first_tile
    ext = TM + 2 * HALO
    row = lax.broadcasted_iota(jnp.int32, (ext, 1), 0)
    keep = jnp.logical_and(jnp.logical_or(row >= HALO, i > 1),
                           jnp.logical_or(row < HALO + TM, jnp.logical_and(i > 0, i < n_tiles - 1)))
    nw = nw_ref[...]
    hes = []
    for s in range(NB):
        xe = jnp.concatenate([xp_ref[s], x_ref[s], xn_ref[s]], axis=0)
        hes.append(jnp.where(keep, _modulate(xe, nw, mod_ref[s, 3:4, :], mod_ref[s, 4:5, :]), 0.0))
    ge = _dot(jnp.concatenate(hes, axis=0).astype(BF16), wg_ref[...])
    val = _dot(jnp.concatenate([h[HALO:HALO + TM] for h in hes], axis=0).astype(BF16), wv_ref[...])
    cw0, cw1, cw2 = cw_ref[0:1, :], cw_ref[1:2, :], cw_ref[2:3, :]
    gc = jnp.concatenate(
        [ge[s * ext + HALO - 1:s * ext + HALO - 1 + TM] * cw0 + ge[s * ext + HALO:s * ext + HALO + TM] * cw1
         + ge[s * ext + HALO + 1:s * ext + HALO + 1 + TM] * cw2 for s in range(NB)], axis=0) + cb_ref[...]
    y = _dot((_silu(gc) * val).astype(BF16), wd_ref[...])
    for s in range(NB):
        o_ref[s] = x_ref[s] + mod_ref[s, 5:6, :] * y[s * TM:(s + 1) * TM]


def _conv_ffn(x, mod, nw, w_up, conv_w, conv_b, wd, skip_ctx):
    b, tt, d = x.shape
    ff = wd.shape[0]
    n_tiles = tt // TM
    first = 1 if skip_ctx else 0
    r = TM // HALO
    nblk = tt // HALO
    return pl.pallas_call(
        functools.partial(_ffn_kernel, first_tile=first, n_tiles=n_tiles),
        grid=(b // NB, n_tiles - first),
        in_specs=[
            pl.BlockSpec((NB, HALO, d), lambda b, i: (b, jnp.maximum((i + first) * r - 1, 0), 0)),
            pl.BlockSpec((NB, TM, d), lambda b, i: (b, i + first, 0)),
            pl.BlockSpec((NB, HALO, d), lambda b, i: (b, jnp.minimum((i + first + 1) * r, nblk - 1), 0)),
            pl.BlockSpec((None, NB, 6, d), lambda b, i: (jnp.minimum(i + first, 1), b, 0, 0)),
            _const_spec((1, d)),
            pl.BlockSpec((d, ff), lambda b, i: (0, 0), pipeline_mode=pl.Buffered(1)),
            pl.BlockSpec((d, ff), lambda b, i: (0, 1), pipeline_mode=pl.Buffered(1)),
            _const_spec((CONV_W, ff)),
            _const_spec((1, ff)),
            _const_spec((ff, d)),
        ],
        out_specs=_rows_spec(d),
        out_shape=jax.ShapeDtypeStruct((b, tt - first * TM, d), F32),
        compiler_params=_cparams(("parallel", "parallel"), V7X_VMEM_LIMIT),
        name="conv_ffn",
    )(x, x, x, mod, nw, w_up, w_up, conv_w, conv_b, wd)


def _rope_swap_perm():
    q = MLA_ROPE // 4
    return np.concatenate([np.arange(q, 2 * q), np.arange(0, q), np.arange(3 * q, 4 * q), np.arange(2 * q, 3 * q)])


def _pad128(v):
    return jnp.pad(v, (0, 128 - v.shape[0])).reshape(1, 128)


def _odd_weights(w_in, qa_w, w_qb, kva_w, w_kvb, qn_w, kn_w, heads, cw, ql, kvl, cos_t, sin_t):
    d = w_in.shape[0]
    perm = _rope_swap_perm()
    nope, rope, qk = MLA_NOPE, MLA_ROPE, MLA_NOPE + MLA_ROPE
    o = cw + ql + kvl
    z = jnp.zeros((d, 128 - rope), w_in.dtype)
    kpe = w_in[:, o:o + rope]
    w_in_ext = jnp.concatenate([w_in[:, :o], kpe, z, kpe[:, perm], z], axis=1).astype(BF16)
    wq = w_qb.reshape(ql, heads, qk)
    zq = jnp.zeros((ql, heads, 128 - rope), w_qb.dtype)
    wq_rope = wq[:, :, nope:]
    w_q = jnp.concatenate([
        wq[:, :, :nope].reshape(ql, heads * 128),
        jnp.concatenate([wq_rope, zq], axis=2).reshape(ql, heads * 128),
        jnp.concatenate([wq_rope[:, :, perm], zq], axis=2).reshape(ql, heads * 128),
    ], axis=1).astype(BF16)
    wkv = w_kvb.reshape(kvl, heads, nope + MLA_V_DIM)
    w_kv = jnp.concatenate([wkv[:, :, :nope].reshape(kvl, heads * nope),
                            wkv[:, :, nope:].reshape(kvl, heads * MLA_V_DIM)], axis=1).astype(BF16)
    gd = cw // FOURIER_GROUPS
    jk = (np.arange(gd)[:, None] * np.arange(gd)[None, :]) % gd
    ang = 2.0 * np.pi * jk / gd
    eye = np.eye(FOURIER_GROUPS)
    bc = np.concatenate([np.kron(eye, np.cos(ang)), np.kron(eye, -np.sin(ang))], axis=1) / np.sqrt(gd)
    return {
        "w_in": w_in_ext, "qa_w": qa_w.reshape(1, ql), "kva_w": kva_w.reshape(1, kvl),
        "kp_w": _pad128(kn_w[nope:]), "kps_w": _pad128(kn_w[nope:][perm]),
        "w_q": w_q, "qn_w": qn_w[:nope].reshape(1, nope),
        "qp_w": _pad128(qn_w[nope:]), "qps_w": _pad128(qn_w[nope:][perm]),
        "w_kv": w_kv, "kn_w": kn_w[:nope].reshape(1, nope),
        "cos": cos_t, "sin": sin_t, "bc": jnp.asarray(bc, F32).astype(BF16),
    }


def _rope_tables(t_lat, ctx_len):
    rows = t_lat // GRID_W
    row = jnp.repeat(jnp.arange(rows), GRID_W)
    col = jnp.tile(jnp.arange(GRID_W), rows)
    r_axis = MLA_ROPE // 2
    inv_freq = ROPE_THETA ** (-jnp.arange(0, r_axis, 2, dtype=F32) / r_axis)
    ang = jnp.stack([row, col], axis=-1).astype(F32)[:, :, None] * inv_freq
    cos, sin = jnp.cos(ang), jnp.sin(ang)
    cos64 = jnp.concatenate([cos[:, 0], cos[:, 0], cos[:, 1], cos[:, 1]], axis=-1)
    sin64 = jnp.concatenate([-sin[:, 0], sin[:, 0], -sin[:, 1], sin[:, 1]], axis=-1)
    cos64 = jnp.concatenate([jnp.ones((ctx_len, MLA_ROPE), F32), cos64], axis=0)
    sin64 = jnp.concatenate([jnp.zeros((ctx_len, MLA_ROPE), F32), sin64], axis=0)
    pad = ((0, 0), (0, 128 - MLA_ROPE))
    return jnp.pad(cos64, pad), jnp.pad(sin64, pad)


def _cos_sin(phase, period, scale):
    ang = (phase % period).astype(F32) * (2.0 * np.pi / period)
    return jnp.cos(ang) * scale, jnp.sin(ang) * scale


def _dft_dense_tables(t):
    idx = jnp.arange(t, dtype=jnp.int32)
    c, s = _cos_sin(idx[:, None] * idx[None, :], t, 1.0 / np.sqrt(t))
    return c.astype(BF16), s.astype(BF16)


def _dft_two_stage_tables(t):
    n1, n2 = _dft_split(t)
    u1 = jnp.arange(n1, dtype=jnp.int32)
    pos = jnp.arange(t, dtype=jnp.int32).reshape(n1, n2)
    c, s = _cos_sin(pos.T[:, None, :] * u1[None, :, None], t, 1.0 / np.sqrt(n1))
    g = jnp.concatenate([c, -s], axis=1).astype(BF16)
    i2 = jnp.arange(n2, dtype=jnp.int32)
    c2, s2 = _cos_sin(i2[:, None] * i2[None, :], n2, 1.0 / np.sqrt(n2))
    return g, c2.astype(BF16), s2.astype(BF16)


def kernel(x, c, ctx, c_ctx, ada_w, ada_b, norm_mix_w, norm_ffn_w, ev_w_in, ev_lb_logits, ev_onorm_w, ev_vnorm_w, ev_ws, ev_bs, ev_w_out, od_w_in, od_qa_norm_w, od_w_qb, od_kva_norm_w, od_w_kvb, od_q_norm_w, od_k_norm_w, od_w_out, ffn_w_up, ffn_conv_w, ffn_conv_b, ffn_w_down):
    bsz, t_lat, d = x.shape
    ctx_len = ctx.shape[1]
    depth = ada_w.shape[0]
    assert ctx_len == TM and t_lat % ATTN_KC == 0 and t_lat % GRID_W == 0 and bsz % NB == 0
    a_width = ev_lb_logits.shape[-1]
    ql, kvl = od_qa_norm_w.shape[-1], od_kva_norm_w.shape[-1]
    cw = od_w_in.shape[-1] - ql - kvl - MLA_ROPE
    heads = od_w_qb.shape[-1] // (MLA_NOPE + MLA_ROPE)
    d_ff = ffn_w_down.shape[1]

    pad_rows = (-(bsz + 1)) % 8
    cvec = jnp.concatenate([c, c_ctx[None, :], jnp.zeros((pad_rows, d), F32)], axis=0)
    mods = _ada_tables(cvec, ada_w, ada_b)
    mod_lat = mods[:, :bsz].reshape(depth, 1, bsz, 6, d)
    mod_ctx = jnp.broadcast_to(mods[:, bsz].reshape(depth, 1, 1, 6, d), (depth, 1, bsz, 6, d))
    mods = jnp.concatenate([mod_ctx, mod_lat], axis=1)

    lb_p = jax.nn.softmax(ev_lb_logits.astype(F32), axis=0)
    lbs = jnp.cumsum(lb_p, axis=0) - lb_p[0]
    cos_t, sin_t = _rope_tables(t_lat, ctx_len)
    dft_lat = dft_ctx = None

    xs = (ctx, x)
    for l in range(depth):
        last = l == depth - 1
        mod = mods[l]
        nmw = norm_mix_w[l].reshape(1, d)
        if l % 2 == 0:
            e = l // 2
            p = _even_in(xs, ctx_len + t_lat, mod, nmw, ev_w_in[e].astype(BF16))
            o_fwd, o_bwd = _hgrn_scan(p, lbs[e], ctx_len, a_width)
            bw = d - a_width
            gd = bw // ev_ws.shape[1]
            bsb = jnp.repeat(ev_bs[e].T, gd, axis=1)
            xs = _even_out(o_fwd, o_bwd, ev_onorm_w[e], p, xs, mod, ev_vnorm_w[e].reshape(1, bw),
                           ev_ws[e].astype(BF16), bsb, ev_w_out[e].astype(BF16), a_width)
        else:
            o = l // 2
            if dft_lat is None:
                dft_lat, dft_ctx = _dft_two_stage_tables(t_lat), _dft_dense_tables(ctx_len)
            wts = _odd_weights(od_w_in[o], od_qa_norm_w[o], od_w_qb[o], od_kva_norm_w[o], od_w_kvb[o],
                               od_q_norm_w[o], od_k_norm_w[o], heads, cw, ql, kvl, cos_t, sin_t)
            xc_ctx, xc_lat, qt, k, vt = _odd_in(xs, mod, nmw, wts, heads, cw, ql, kvl)
            attn = _attention(qt, k, vt, ctx_len)
            fm_ctx = _dft_dense(xc_ctx, *dft_ctx)
            fm_lat = _dft_two_stage(xc_lat, dft_lat)
            xs = _odd_out(fm_ctx, fm_lat, attn, xs, mod, od_w_out[o].astype(BF16))
        xs = _conv_ffn(xs, mod, norm_ffn_w[l].reshape(1, d), ffn_w_up[l].astype(BF16), ffn_conv_w[l],
                       ffn_conv_b[l].reshape(1, d_ff), ffn_w_down[l].astype(BF16), skip_ctx=last)
    return xs
```

```python
import functools

import numpy as np
import jax
import jax.numpy as jnp
from jax import lax
from jax.experimental import pallas as pl
from jax.experimental.pallas import tpu as pltpu

F32 = jnp.float32
BF16 = jnp.bfloat16

EPS = 1e-6
F_MIN = 1e-6
LOG2_E = float(np.log2(np.e))
GRID_W = 64
ROPE_THETA = 10000.0
HGRN_HEAD_DIM = 128
HGRN_CHUNK = 64
HGRN_STEP_CHUNKS = 4
MLP_CHUNK = 128
MLA_NOPE = 128
MLA_ROPE = 64
MLA_V_DIM = 128
V_PAD = 16
FOURIER_GROUPS = 4
CONV_W = 3

TM = 256
NB = 2
HALO = 8
ATTN_KC = 512
ATTN_HEADS_PER_STEP = 3
V7X_VMEM_LIMIT = 56 * 1024 * 1024


def _cparams(sem, vmem=None):
    return pltpu.CompilerParams(dimension_semantics=sem, vmem_limit_bytes=vmem)


def _dot(a, b):
    return jnp.dot(a, b, preferred_element_type=F32)


def _dot_nt(a, b):
    return lax.dot_general(a, b, (((1,), (1,)), ((), ())), preferred_element_type=F32)


def _dot_tn(a, b):
    return lax.dot_general(a, b, (((0,), (0,)), ((), ())), preferred_element_type=F32)


def _silu(x):
    return x * jax.nn.sigmoid(x)


def _gelu_tanh(x):
    c = 0.7978845608028654
    return x * (0.5 + 0.5 * jnp.tanh(x * (c + (c * 0.044715) * (x * x))))


def _rms(x, n=None):
    n = x.shape[-1] if n is None else n
    return x * lax.rsqrt(jnp.sum(x * x, axis=-1, keepdims=True) * (1.0 / n) + EPS)


def _modulate(x, nw, shift, scale):
    return (_rms(x) * nw) * (1.0 + scale) + shift


def _mod_spec(d, nb):
    return pl.BlockSpec((None, nb, 6, d), lambda b, i: (jnp.minimum(i, 1), b, 0, 0))


def _rows_spec(cols, colblk=0):
    return pl.BlockSpec((NB, TM, cols), lambda b, i: (b, i, colblk))


def _const_spec(shape):
    nd = len(shape)
    return pl.BlockSpec(shape, lambda *_: (0,) * nd, pipeline_mode=pl.Buffered(1))


def _ada_kernel(c_ref, w_ref, b_ref, o_ref):
    s = _silu(c_ref[...]).astype(BF16)
    o_ref[...] = _dot(s, w_ref[...].astype(BF16)) + b_ref[...]


def _ada_tables(cvec, ada_w, ada_b):
    depth, d, n = ada_w.shape
    rows = cvec.shape[0]
    tn = 1536
    return pl.pallas_call(
        _ada_kernel,
        grid=(depth, n // tn),
        in_specs=[
            pl.BlockSpec((rows, d), lambda l, j: (0, 0)),
            pl.BlockSpec((None, d, tn), lambda l, j: (l, 0, j)),
            pl.BlockSpec((None, 1, tn), lambda l, j: (l, 0, j)),
        ],
        out_specs=pl.BlockSpec((None, rows, tn), lambda l, j: (l, 0, j)),
        out_shape=jax.ShapeDtypeStruct((depth, rows, n), F32),
        compiler_params=_cparams(("parallel", "parallel"), V7X_VMEM_LIMIT),
        name="ada_tables",
    )(cvec, ada_w, ada_b.reshape(depth, 1, n))


def _stream_specs(xs, d):
    if isinstance(xs, tuple):
        return ([pl.BlockSpec((NB, TM, d), lambda b, i: (b, 0, 0)),
                 pl.BlockSpec((NB, TM, d), lambda b, i: (b, jnp.maximum(i - 1, 0), 0))], list(xs))
    return [_rows_spec(d)], [xs]


def _stream_rows(x_refs, s):
    if len(x_refs) == 1:
        return x_refs[0][s]
    return jnp.where(pl.program_id(1) == 0, x_refs[0][s], x_refs[1][s])


def _even_in_kernel(*refs, n_x, a_width):
    x_refs = refs[:n_x]
    mod_ref, nw_ref, w_ref, oz_ref, ob_ref = refs[n_x:]
    nw = nw_ref[...]
    h = jnp.concatenate([_modulate(_stream_rows(x_refs, s), nw, mod_ref[s, 0:1, :], mod_ref[s, 1:2, :]).astype(BF16)
                         for s in range(NB)], axis=0)
    p = _dot(h, w_ref[...])
    z0, z1 = a_width * 2, a_width * 4
    for s in range(NB):
        rows = slice(s * TM, (s + 1) * TM)
        oz_ref[s] = p[rows, z0:z1]
        ob_ref[s] = jnp.concatenate([p[rows, :z0], p[rows, z1:]], axis=1).astype(ob_ref.dtype)


def _even_in(xs, tt, mod, nw, w_in, a_width):
    d, n = w_in.shape
    x_specs, x_args = _stream_specs(xs, d)
    b = x_args[0].shape[0]
    return pl.pallas_call(
        functools.partial(_even_in_kernel, n_x=len(x_args), a_width=a_width),
        grid=(b // NB, tt // TM),
        in_specs=x_specs + [
            _mod_spec(d, NB),
            _const_spec((1, d)),
            _const_spec((d, n)),
        ],
        out_specs=[_rows_spec(2 * a_width), _rows_spec(n - 2 * a_width)],
        out_shape=[jax.ShapeDtypeStruct((b, tt, 2 * a_width), F32),
                   jax.ShapeDtypeStruct((b, tt, n - 2 * a_width), BF16)],
        compiler_params=_cparams(("parallel", "parallel"), V7X_VMEM_LIMIT),
        name="even_in",
    )(*x_args, mod, nw, w_in)


def _hgrn_consts(reverse):
    c = HGRN_CHUNK
    idx = np.arange(c)
    if not reverse:
        tri = (idx[None, :] <= idx[:, None]).astype(np.float32)
    else:
        tri = (idx[None, :] >= idx[:, None]).astype(np.float32)
    masks = []
    for w in (32, 16, 8):
        grp = idx // (2 * w)
        qrow = (idx % (2 * w) >= w) if not reverse else (idx % (2 * w) < w)
        masks.append((grp[:, None] == grp[None, :]) & qrow[:, None] & (~qrow)[None, :])
    blk = idx // 8
    causal = (idx[None, :] <= idx[:, None]) if not reverse else (idx[None, :] >= idx[:, None])
    masks.append((blk[:, None] == blk[None, :]) & causal)
    return np.kron(np.eye(HGRN_STEP_CHUNKS), tri).astype(np.float32), np.stack(masks).astype(np.float32)


def _hgrn_prepare(pq_ref, pv_ref, pz_ref, lb, tri_ref, width):
    f = jnp.maximum(lb + (1.0 - lb) * jax.nn.sigmoid(pz_ref[...]), F_MIN)
    g = jnp.log(f) * LOG2_E
    g1 = g.astype(BF16)
    r1 = g - g1.astype(F32)
    g2 = r1.astype(BF16)
    g3 = (r1 - g2.astype(F32)).astype(BF16)
    b3 = _dot(tri_ref[...], jnp.concatenate([g1, g2, g3], axis=1))
    b_all = b3[:, :width] + b3[:, width:2 * width] + b3[:, 2 * width:]
    return _silu(pq_ref[...].astype(F32)).astype(BF16), (1.0 - f).astype(BF16), pv_ref[...], b_all


def _hgrn_chunk(feats, ci, reverse, masks_ref, st, heads):
    c, hd = HGRN_CHUNK, HGRN_HEAD_DIM
    width = heads * hd
    q, kk, v, b_all = feats
    rs = slice(ci * c, (ci + 1) * c)
    last = c - 1 if not reverse else 0
    parts = []
    for h in range(heads):
        sl = slice(h * hd, (h + 1) * hd)
        qh, kh, b = q[rs, sl], kk[rs, sl], b_all[rs, sl]

        def ref_rows(rows, n):
            return jnp.concatenate([jnp.broadcast_to(b[r:r + 1, :], (n, hd)) for r in rows], axis=0)

        a = None
        for lv, w in enumerate((32, 16, 8)):
            mids = [gi * 2 * w + (w - 1 if not reverse else w) for gi in range(c // (2 * w))]
            ew = jnp.exp2(-jnp.abs(b - ref_rows(mids, 2 * w))).astype(BF16)
            t = _dot_nt(qh * ew, kh * ew) * masks_ref[lv]
            a = t if a is None else a + t
        anchor = ref_rows([8 * m + (3 if not reverse else 4) for m in range(c // 8)], 8)
        d8 = _dot_nt(qh * jnp.exp2(b - anchor).astype(BF16), kh * jnp.exp2(anchor - b).astype(BF16))
        a = a + jnp.where(masks_ref[3] > 0.5, d8, 0.0)
        vb = v[rs, sl]
        e_in = jnp.exp2(b)
        qhat = qh * e_in.astype(BF16)
        khat = kh * jnp.exp2(ref_rows([last], c) - b).astype(BF16)
        parts.append((a.astype(BF16), vb, qhat, _dot_tn(vb, khat), e_in[last:last + 1, :]))
    heads_out = []
    for h, (a, vb, qhat, upd, e_last) in enumerate(parts):
        heads_out.append(_dot(a, vb) + _dot_nt(qhat, st[h].astype(BF16)))
        st[h] = st[h] * e_last + upd
    return jnp.concatenate(heads_out, axis=1)


def _hgrn_kernel(fq_ref, fv_ref, fz_ref, bq_ref, bv_ref, bz_ref, lb_ref, ftri_ref, btri_ref, fmask_ref, bmask_ref,
                 of_ref, ob_ref, fst_ref, bst_ref, *, heads):
    nck = HGRN_STEP_CHUNKS
    width = heads * HGRN_HEAD_DIM

    @pl.when(pl.program_id(1) == 0)
    def _():
        fst_ref[...] = jnp.zeros_like(fst_ref)
        bst_ref[...] = jnp.zeros_like(bst_ref)

    ffeats = _hgrn_prepare(fq_ref, fv_ref, fz_ref, lb_ref[0:1, :], ftri_ref, width)
    bfeats = _hgrn_prepare(bq_ref, bv_ref, bz_ref, lb_ref[1:2, :], btri_ref, width)
    fst = [fst_ref[h] for h in range(heads)]
    bst = [bst_ref[h] for h in range(heads)]
    fouts, bouts = [None] * nck, [None] * nck
    for k in range(nck):
        fouts[k] = _hgrn_chunk(ffeats, k, False, fmask_ref, fst, heads)
        bouts[nck - 1 - k] = _hgrn_chunk(bfeats, nck - 1 - k, True, bmask_ref, bst, heads)
    for h in range(heads):
        fst_ref[h] = fst[h]
        bst_ref[h] = bst[h]
    of_ref[...] = jnp.concatenate(fouts, axis=0).astype(of_ref.dtype)
    ob_ref[...] = jnp.concatenate(bouts, axis=0).astype(ob_ref.dtype)


def _hgrn_scan(pz, pb, lbs, ctx_len, width):
    b, tt, _ = pz.shape
    rows = HGRN_CHUNK * HGRN_STEP_CHUNKS
    heads = width // HGRN_HEAD_DIM
    nb, nbc = tt // rows, ctx_len // rows
    ftri, fmask = _hgrn_consts(False)
    btri, bmask = _hgrn_consts(True)

    def bblk(j):
        return jnp.where(j < nbc, nbc - 1 - j, nb - 1 - (j - nbc))

    def fcol(k):
        return pl.BlockSpec((None, rows, width), lambda b, j: (b, j, k))

    def bcol(k):
        return pl.BlockSpec((None, rows, width), lambda b, j: (b, bblk(j), k))

    consts = [lbs, jnp.asarray(ftri, BF16), jnp.asarray(btri, BF16), jnp.asarray(fmask, F32), jnp.asarray(bmask, F32)]
    state = pltpu.VMEM((heads, HGRN_HEAD_DIM, HGRN_HEAD_DIM), F32)
    return pl.pallas_call(
        functools.partial(_hgrn_kernel, heads=heads),
        grid=(b, nb),
        in_specs=[fcol(0), fcol(1), fcol(0), bcol(0), bcol(1), bcol(1)] + [_const_spec(a.shape) for a in consts],
        out_specs=[fcol(0), bcol(0)],
        out_shape=[jax.ShapeDtypeStruct((b, tt, width), BF16)] * 2,
        scratch_shapes=[state, state],
        compiler_params=_cparams(("parallel", "arbitrary"), V7X_VMEM_LIMIT),
        name="hgrn_scan",
    )(pb, pb, pz, pb, pb, pz, *consts)


def _even_out_kernel(*refs, groups, n_x):
    of_ref, ob_ref, pg_ref, pu_ref, pv_ref = refs[:5]
    x_refs = refs[5:5 + n_x]
    mod_ref, onw_ref, vnw_ref, ws_ref, bsb_ref, wa_ref, wb_ref, o_ref = refs[5 + n_x:]
    gd = pu_ref.shape[2] // groups
    hd = HGRN_HEAD_DIM
    onw = onw_ref[...]
    ms, yas = [], []
    for s in range(NB):
        o = of_ref[s].astype(F32) + ob_ref[s].astype(F32)
        yn = jnp.concatenate([_rms(o[:, h * hd:(h + 1) * hd]) * onw for h in range(o.shape[1] // hd)], axis=1)
        yas.append((yn * _silu(pg_ref[s].astype(F32))).astype(BF16))
    y = _dot(jnp.concatenate(yas, axis=0), wa_ref[...])
    for s in range(NB):
        for n in range(TM // MLP_CHUNK):
            rows = slice(n * MLP_CHUNK, (n + 1) * MLP_CHUNK)
            u = _gelu_tanh(pu_ref[s, rows, :].astype(F32))
            v = _gelu_tanh(pv_ref[s, rows, :].astype(F32))
            parts = []
            for g in range(groups):
                cs = slice(g * gd, (g + 1) * gd)
                vg = (_rms(v[:, cs]) * vnw_ref[:, cs]).astype(BF16)
                sv = _dot(ws_ref[g], vg) + bsb_ref[:, cs]
                parts.append(u[:, cs] * sv)
            ms.append(jnp.concatenate(parts, axis=1).astype(BF16))
    y = y + _dot(jnp.concatenate(ms, axis=0), wb_ref[...])
    for s in range(NB):
        o_ref[s] = _stream_rows(x_refs, s) + mod_ref[s, 2:3, :] * y[s * TM:(s + 1) * TM]


def _even_out(o_fwd, o_bwd, onw, pb, xs, mod, vnw, ws, bsb, w_out, a_width):
    b, tt, _ = pb.shape
    d = w_out.shape[1]
    bw = d - a_width
    groups = ws.shape[0]
    ucol = 3 * a_width // bw
    x_specs, x_args = _stream_specs(xs, d)
    return pl.pallas_call(
        functools.partial(_even_out_kernel, groups=groups, n_x=len(x_args)),
        grid=(b // NB, tt // TM),
        in_specs=[
            _rows_spec(a_width),
            _rows_spec(a_width),
            _rows_spec(a_width, 2),
            _rows_spec(bw, ucol),
            _rows_spec(bw, ucol + 1)] + x_specs + [
            _mod_spec(d, NB),
            _const_spec((1, HGRN_HEAD_DIM)),
            _const_spec((1, bw)),
            _const_spec(ws.shape),
            _const_spec(bsb.shape),
            _const_spec((a_width, d)),
            _const_spec((bw, d)),
        ],
        out_specs=_rows_spec(d),
        out_shape=jax.ShapeDtypeStruct((b, tt, d), F32),
        compiler_params=_cparams(("parallel", "parallel"), V7X_VMEM_LIMIT),
        name="even_out",
    )(o_fwd, o_bwd, pb, pb, pb, *x_args, mod, onw.reshape(1, HGRN_HEAD_DIM), vnw, ws, bsb,
      w_out[:a_width], w_out[a_width:])


def _odd_in_kernel(x_ref, mod_ref, nw_ref, win_ref, qaw_ref, kvaw_ref, kpw_ref, kpsw_ref, wq_ref, qnw_ref,
                   qpw_ref, qpsw_ref, wkv_ref, knw_ref, cos_ref, sin_ref, bc_ref,
                   xc_ctx_ref, xc_lat_ref, qt_ref, k_ref, vt_ref, *, heads, cw, ql, kvl, scale):
    cos, sin = cos_ref[...], sin_ref[...]
    rope = MLA_ROPE
    o = cw + ql + kvl
    hw = heads * 128
    ones_pad = (lax.broadcasted_iota(jnp.int32, (V_PAD, TM), 0) == 0).astype(F32)
    is_ctx = pl.program_id(1) == 0

    def project(s):
        h = _modulate(x_ref[s], nw_ref[...], mod_ref[s, 0:1, :], mod_ref[s, 1:2, :])
        return _dot(h.astype(BF16), win_ref[...])

    def latents(s, p):
        xc = _dot(p[:, :cw].astype(BF16), bc_ref[...]).astype(xc_lat_ref.dtype)
        xc_lat_ref[s] = xc

        @pl.when(is_ctx)
        def _():
            xc_ctx_ref[s] = xc
        q_lat = (_rms(p[:, cw:cw + ql]) * qaw_ref[...]).astype(BF16)
        kv_lat = (_rms(p[:, cw + ql:cw + ql + kvl]) * kvaw_ref[...]).astype(BF16)
        kp, kps = p[:, o:o + 128], p[:, o + 128:o + 256]
        kpr = lax.rsqrt(jnp.sum(kp * kp, axis=-1, keepdims=True) * (1.0 / rope) + EPS)
        k_pe = (kp * kpw_ref[...] * cos + kps * kpsw_ref[...] * sin) * kpr
        qf = _dot(q_lat, wq_ref[...])
        kvf = _dot(kv_lat, wkv_ref[...])
        return qf, kvf, k_pe

    def heads_out(s, qf, kvf, k_pe):
        for hh in range(heads):
            cs = slice(hh * 128, (hh + 1) * 128)
            qn = _rms(qf[:, cs]) * qnw_ref[...]
            qp = qf[:, hw + hh * 128:hw + (hh + 1) * 128]
            qps = qf[:, 2 * hw + hh * 128:2 * hw + (hh + 1) * 128]
            qpr = lax.rsqrt(jnp.sum(qp * qp, axis=-1, keepdims=True) * (1.0 / rope) + EPS)
            q_pe = (qp * qpw_ref[...] * cos + qps * qpsw_ref[...] * sin) * qpr
            qt_ref[s, hh] = (jnp.concatenate([qn, q_pe], axis=1) * scale).T.astype(qt_ref.dtype)
            kn = _rms(kvf[:, cs]) * knw_ref[...]
            k_ref[s, hh] = jnp.concatenate([kn, k_pe], axis=1).astype(k_ref.dtype)
            vt_ref[s, hh] = jnp.concatenate([kvf[:, hw + hh * 128:hw + (hh + 1) * 128].T, ones_pad],
                                            axis=0).astype(vt_ref.dtype)

    mid = latents(0, project(0))
    for s in range(1, NB):
        p_next = project(s)
        heads_out(s - 1, *mid)
        mid = latents(s, p_next)
    heads_out(NB - 1, *mid)


def _odd_in(x, mod, nw, wts, heads, cw, ql, kvl):
    b, tt, d = x.shape
    scale = float(MLA_NOPE + MLA_ROPE) ** -0.5 * LOG2_E
    consts = [wts[k] for k in ("w_in", "qa_w", "kva_w", "kp_w", "kps_w", "w_q", "qn_w", "qp_w", "qps_w",
                               "w_kv", "kn_w")]
    row128 = pl.BlockSpec((TM, 128), lambda b, i: (i, 0))
    in_specs = ([_rows_spec(d), _mod_spec(d, NB), _const_spec((1, d))]
                + [_const_spec(a.shape) for a in consts] + [row128, row128, _const_spec(wts["bc"].shape)])
    return pl.pallas_call(
        functools.partial(_odd_in_kernel, heads=heads, cw=cw, ql=ql, kvl=kvl, scale=scale),
        grid=(b // NB, tt // TM),
        in_specs=in_specs,
        out_specs=[
            pl.BlockSpec((NB, TM, 2 * cw), lambda b, i: (b, 0, 0)),
            pl.BlockSpec((NB, TM, 2 * cw), lambda b, i: (b, jnp.maximum(i - 1, 0), 0)),
            pl.BlockSpec((NB, heads, 256, TM), lambda b, i: (b, 0, 0, i)),
            pl.BlockSpec((NB, heads, TM, 256), lambda b, i: (b, 0, i, 0)),
            pl.BlockSpec((NB, heads, MLA_V_DIM + V_PAD, TM), lambda b, i: (b, 0, 0, i)),
        ],
        out_shape=[
            jax.ShapeDtypeStruct((b, TM, 2 * cw), BF16),
            jax.ShapeDtypeStruct((b, tt - TM, 2 * cw), BF16),
            jax.ShapeDtypeStruct((b, heads, 256, tt), BF16),
            jax.ShapeDtypeStruct((b, heads, tt, 256), BF16),
            jax.ShapeDtypeStruct((b, heads, MLA_V_DIM + V_PAD, tt), BF16),
        ],
        compiler_params=_cparams(("parallel", "arbitrary"), V7X_VMEM_LIMIT),
        name="odd_in",
    )(x, mod, nw, *consts, wts["cos"], wts["sin"], wts["bc"])


def _attn_kernel(qt_ref, k_ref, vt_ref, o_ref, sa_ref, sb_ref, ma_ref, mb_ref, *, ctx_len, n_tiles):
    i = pl.program_id(2)
    n_all = k_ref.shape[1]
    nh = k_ref.shape[0]
    bufs = ((sa_ref, ma_ref), (sb_ref, mb_ref))

    def chunks(n_keys):
        return [(st, min(ATTN_KC, n_keys - st)) for st in range(0, n_keys, ATTN_KC)]

    def run(step_parity, keys1, keys2):
        s_w, m_w = bufs[step_parity]
        s_r, m_r = bufs[1 - step_parity]
        c1 = chunks(keys1) if keys1 else []
        c2 = chunks(keys2) if keys2 else []
        if keys1:
            qts = [qt_ref[hh] for hh in range(nh)]
        if keys2:
            m_prev = [m_r[hh] for hh in range(nh)]
        m, acc = [None] * nh, [None] * nh
        for idx in range(max(len(c1), len(c2))):
            for hh in range(nh):
                if idx < len(c1):
                    st, sz = c1[idx]
                    s = _dot(k_ref[hh, st:st + sz, :], qts[hh])
                    s_w[hh, st:st + sz, :] = s
                    cm = jnp.max(s, axis=0, keepdims=True)
                    m[hh] = cm if m[hh] is None else jnp.maximum(m[hh], cm)
                if idx < len(c2):
                    st, sz = c2[idx]
                    p = jnp.exp2(s_r[hh, st:st + sz, :] - m_prev[hh]).astype(BF16)
                    ca = _dot(vt_ref[hh, :, st:st + sz], p)
                    acc[hh] = ca if acc[hh] is None else acc[hh] + ca
        dv = o_ref.shape[1] // nh
        for hh in range(nh):
            if keys1:
                m_w[hh] = m[hh]
            if keys2:
                o_ref[:, hh * dv:(hh + 1) * dv] = (acc[hh][:dv] * (1.0 / acc[hh][dv:dv + 1])).T.astype(o_ref.dtype)

    @pl.when(i == 0)
    def _():
        run(0, ctx_len, None)

    @pl.when(i == 1)
    def _():
        run(1, n_all, ctx_len)

    for parity in (0, 1):
        @pl.when(jnp.logical_and(jnp.logical_and(i >= 2, i < n_tiles), i % 2 == parity))
        def _():
            run(parity, n_all, n_all)

    @pl.when(i == n_tiles)
    def _():
        run(n_tiles % 2, None, n_all)


def _attention(qt, k, vt, ctx_len):
    b, heads, tt, dq = k.shape
    dvp = vt.shape[2]
    dv = dvp - V_PAD
    n_tiles = tt // TM
    nh = ATTN_HEADS_PER_STEP
    assert heads % nh == 0
    return pl.pallas_call(
        functools.partial(_attn_kernel, ctx_len=ctx_len, n_tiles=n_tiles),
        grid=(b, heads // nh, n_tiles + 1),
        in_specs=[
            pl.BlockSpec((None, nh, dq, TM), lambda b, h, i: (b, h, 0, jnp.minimum(i, n_tiles - 1))),
            pl.BlockSpec((None, nh, tt, dq), lambda b, h, i: (b, h, 0, 0)),
            pl.BlockSpec((None, nh, dvp, tt), lambda b, h, i: (b, h, 0, 0)),
        ],
        out_specs=pl.BlockSpec((None, TM, nh * dv), lambda b, h, i: (b, jnp.maximum(i - 1, 0), h)),
        out_shape=jax.ShapeDtypeStruct((b, tt, heads * dv), BF16),
        scratch_shapes=[pltpu.VMEM((nh, tt, TM), F32), pltpu.VMEM((nh, tt, TM), F32),
                        pltpu.VMEM((nh, 1, TM), F32), pltpu.VMEM((nh, 1, TM), F32)],
        compiler_params=_cparams(("parallel", "parallel", "arbitrary"), V7X_VMEM_LIMIT),
        name="attention",
    )(qt, k, vt)


def _dft_kernel(c_ref, s_ref, x_ref, o_ref):
    half = x_ref.shape[1] // 2
    y = _dot(c_ref[...], x_ref[:, :half]) + _dot(s_ref[...], x_ref[:, half:])
    o_ref[...] = y.astype(o_ref.dtype)


def _dft_dense(xc, t_cos, t_sin):
    b, t, w2 = xc.shape
    return pl.pallas_call(
        _dft_kernel,
        grid=(b,),
        in_specs=[_const_spec((t, t)), _const_spec((t, t)), pl.BlockSpec((None, t, w2), lambda b: (b, 0, 0))],
        out_specs=pl.BlockSpec((None, t, w2 // 2), lambda b: (b, 0, 0)),
        out_shape=jax.ShapeDtypeStruct((b, t, w2 // 2), BF16),
        compiler_params=_cparams(("parallel",), V7X_VMEM_LIMIT),
        name="dft_dense",
    )(t_cos, t_sin, xc)


def _dft_split(t):
    n2 = 1 << ((t.bit_length()) // 2)
    return t // n2, n2


def _dft_stage1_kernel(x_ref, g_ref, zr_ref, zi_ref, *, n1, n2, cw):
    for t2 in range(n2):
        p = _dot(g_ref[t2], x_ref[:, t2 * 2 * cw:(t2 + 1) * 2 * cw])
        zr_ref[t2] = (p[:n1, :cw] - p[n1:, cw:]).astype(zr_ref.dtype)
        zi_ref[t2] = (p[:n1, cw:] + p[n1:, :cw]).astype(zi_ref.dtype)


def _dft_stage2_kernel(zr_ref, zi_ref, c_ref, s_ref, o_ref):
    o_ref[...] = (_dot(c_ref[...], zr_ref[...]) + _dot(s_ref[...], zi_ref[...])).astype(o_ref.dtype)


def _dft_two_stage(xc, tables):
    g, c2, s2 = tables
    b, t, w2 = xc.shape
    cw = w2 // 2
    n1, n2 = _dft_split(t)
    zr, zi = pl.pallas_call(
        functools.partial(_dft_stage1_kernel, n1=n1, n2=n2, cw=cw),
        grid=(b,),
        in_specs=[pl.BlockSpec((None, n1, n2 * w2), lambda b: (b, 0, 0)), _const_spec(g.shape)],
        out_specs=[pl.BlockSpec((None, n2, n1, cw), lambda b: (b, 0, 0, 0))] * 2,
        out_shape=[jax.ShapeDtypeStruct((b, n2, n1, cw), BF16)] * 2,
        compiler_params=_cparams(("parallel",), V7X_VMEM_LIMIT),
        name="dft_stage1",
    )(xc.reshape(b, n1, n2 * w2), g)
    y = pl.pallas_call(
        _dft_stage2_kernel,
        grid=(b,),
        in_specs=[pl.BlockSpec((None, n2, n1 * cw), lambda b: (b, 0, 0))] * 2 + [_const_spec((n2, n2))] * 2,
        out_specs=pl.BlockSpec((None, n2, n1 * cw), lambda b: (b, 0, 0)),
        out_shape=jax.ShapeDtypeStruct((b, n2, n1 * cw), BF16),
        compiler_params=_cparams(("parallel",), V7X_VMEM_LIMIT),
        name="dft_stage2",
    )(zr.reshape(b, n2, n1 * cw), zi.reshape(b, n2, n1 * cw), c2, s2)
    return y.reshape(b, t, cw)


def _odd_out_kernel(fc_ref, fl_ref, at_ref, x_ref, mod_ref, wf_ref, wa_ref, o_ref):
    is_ctx = pl.program_id(1) == 0
    fm = jnp.concatenate([jnp.where(is_ctx, fc_ref[s], fl_ref[s]) for s in range(NB)], axis=0)
    at = jnp.concatenate([at_ref[s] for s in range(NB)], axis=0)
    y = _dot(fm, wf_ref[...]) + _dot(at, wa_ref[...])
    for s in range(NB):
        o_ref[s] = x_ref[s] + mod_ref[s, 2:3, :] * y[s * TM:(s + 1) * TM]


def _odd_out(fm_ctx, fm_lat, attn, x, mod, w_out):
    b, tt, d = x.shape
    cw = fm_ctx.shape[-1]
    aw = attn.shape[-1]
    return pl.pallas_call(
        _odd_out_kernel,
        grid=(b // NB, tt // TM),
        in_specs=[
            pl.BlockSpec((NB, TM, cw), lambda b, i: (b, 0, 0)),
            pl.BlockSpec((NB, TM, cw), lambda b, i: (b, jnp.maximum(i - 1, 0), 0)),
            _rows_spec(aw),
            _rows_spec(d),
            _mod_spec(d, NB),
            _const_spec((cw, d)),
            _const_spec((aw, d)),
        ],
        out_specs=_rows_spec(d),
        out_shape=jax.ShapeDtypeStruct((b, tt, d), F32),
        compiler_params=_cparams(("parallel", "parallel"), V7X_VMEM_LIMIT),
        name="odd_out",
    )(fm_ctx, fm_lat, attn, x, mod, w_out[:cw], w_out[cw:])


def _ffn_kernel(xp_ref, x_ref, xn_ref, mod_ref, nw_ref, wg_ref, wv_ref, cw_ref, cb_ref, wd_ref, o_ref,
                *, first_tile, n_tiles):
    i = pl.program_id(1) + first_tile
    ext = TM + 2 * HALO
    row = lax.broadcasted_iota(jnp.int32, (ext, 1), 0)
    keep = jnp.logical_and(jnp.logical_or(row >= HALO, i > 1),
                           jnp.logical_or(row < HALO + TM, jnp.logical_and(i > 0, i < n_tiles - 1)))
    nw = nw_ref[...]
    hes = []
    for s in range(NB):
        xe = jnp.concatenate([xp_ref[s], x_ref[s], xn_ref[s]], axis=0)
        hes.append(jnp.where(keep, _modulate(xe, nw, mod_ref[s, 3:4, :], mod_ref[s, 4:5, :]), 0.0))
    ge = _dot(jnp.concatenate(hes, axis=0).astype(BF16), wg_ref[...])
    val = _dot(jnp.concatenate([h[HALO:HALO + TM] for h in hes], axis=0).astype(BF16), wv_ref[...])
    cw0, cw1, cw2 = cw_ref[0:1, :], cw_ref[1:2, :], cw_ref[2:3, :]
    gc = jnp.concatenate(
        [ge[s * ext + HALO - 1:s * ext + HALO - 1 + TM] * cw0 + ge[s * ext + HALO:s * ext + HALO + TM] * cw1
         + ge[s * ext + HALO + 1:s * ext + HALO + 1 + TM] * cw2 for s in range(NB)], axis=0) + cb_ref[...]
    y = _dot((_silu(gc) * val).astype(BF16), wd_ref[...])
    for s in range(NB):
        o_ref[s] = x_ref[s] + mod_ref[s, 5:6, :] * y[s * TM:(s + 1) * TM]


def _conv_ffn(x, mod, nw, w_up, conv_w, conv_b, wd, skip_ctx):
    b, tt, d = x.shape
    ff = wd.shape[0]
    n_tiles = tt // TM
    first = 1 if skip_ctx else 0
    r = TM // HALO
    nblk = tt // HALO
    return pl.pallas_call(
        functools.partial(_ffn_kernel, first_tile=first, n_tiles=n_tiles),
        grid=(b // NB, n_tiles - first),
        in_specs=[
            pl.BlockSpec((NB, HALO, d), lambda b, i: (b, jnp.maximum((i + first) * r - 1, 0), 0)),
            pl.BlockSpec((NB, TM, d), lambda b, i: (b, i + first, 0)),
            pl.BlockSpec((NB, HALO, d), lambda b, i: (b, jnp.minimum((i + first + 1) * r, nblk - 1), 0)),
            pl.BlockSpec((None, NB, 6, d), lambda b, i: (jnp.minimum(i + first, 1), b, 0, 0)),
            _const_spec((1, d)),
            pl.BlockSpec((d, ff), lambda b, i: (0, 0), pipeline_mode=pl.Buffered(1)),
            pl.BlockSpec((d, ff), lambda b, i: (0, 1), pipeline_mode=pl.Buffered(1)),
            _const_spec((CONV_W, ff)),
            _const_spec((1, ff)),
            _const_spec((ff, d)),
        ],
        out_specs=_rows_spec(d),
        out_shape=jax.ShapeDtypeStruct((b, tt - first * TM, d), F32),
        compiler_params=_cparams(("parallel", "parallel"), V7X_VMEM_LIMIT),
        name="conv_ffn",
    )(x, x, x, mod, nw, w_up, w_up, conv_w, conv_b, wd)


def _rope_swap_perm():
    q = MLA_ROPE // 4
    return np.concatenate([np.arange(q, 2 * q), np.arange(0, q), np.arange(3 * q, 4 * q), np.arange(2 * q, 3 * q)])


def _pad128(v):
    return jnp.pad(v, (0, 128 - v.shape[0])).reshape(1, 128)


def _odd_weights(w_in, qa_w, w_qb, kva_w, w_kvb, qn_w, kn_w, heads, cw, ql, kvl, cos_t, sin_t):
    d = w_in.shape[0]
    perm = _rope_swap_perm()
    nope, rope, qk = MLA_NOPE, MLA_ROPE, MLA_NOPE + MLA_ROPE
    o = cw + ql + kvl
    z = jnp.zeros((d, 128 - rope), w_in.dtype)
    kpe = w_in[:, o:o + rope]
    w_in_ext = jnp.concatenate([w_in[:, :o], kpe, z, kpe[:, perm], z], axis=1).astype(BF16)
    wq = w_qb.reshape(ql, heads, qk)
    zq = jnp.zeros((ql, heads, 128 - rope), w_qb.dtype)
    wq_rope = wq[:, :, nope:]
    w_q = jnp.concatenate([
        wq[:, :, :nope].reshape(ql, heads * 128),
        jnp.concatenate([wq_rope, zq], axis=2).reshape(ql, heads * 128),
        jnp.concatenate([wq_rope[:, :, perm], zq], axis=2).reshape(ql, heads * 128),
    ], axis=1).astype(BF16)
    wkv = w_kvb.reshape(kvl, heads, nope + MLA_V_DIM)
    w_kv = jnp.concatenate([wkv[:, :, :nope].reshape(kvl, heads * nope),
                            wkv[:, :, nope:].reshape(kvl, heads * MLA_V_DIM)], axis=1).astype(BF16)
    gd = cw // FOURIER_GROUPS
    jk = (np.arange(gd)[:, None] * np.arange(gd)[None, :]) % gd
    ang = 2.0 * np.pi * jk / gd
    eye = np.eye(FOURIER_GROUPS)
    bc = np.concatenate([np.kron(eye, np.cos(ang)), np.kron(eye, -np.sin(ang))], axis=1) / np.sqrt(gd)
    return {
        "w_in": w_in_ext, "qa_w": qa_w.reshape(1, ql), "kva_w": kva_w.reshape(1, kvl),
        "kp_w": _pad128(kn_w[nope:]), "kps_w": _pad128(kn_w[nope:][perm]),
        "w_q": w_q, "qn_w": qn_w[:nope].reshape(1, nope),
        "qp_w": _pad128(qn_w[nope:]), "qps_w": _pad128(qn_w[nope:][perm]),
        "w_kv": w_kv, "kn_w": kn_w[:nope].reshape(1, nope),
        "cos": cos_t, "sin": sin_t, "bc": jnp.asarray(bc, F32).astype(BF16),
    }


def _rope_tables(t_lat, ctx_len):
    rows = t_lat // GRID_W
    row = jnp.repeat(jnp.arange(rows), GRID_W)
    col = jnp.tile(jnp.arange(GRID_W), rows)
    r_axis = MLA_ROPE // 2
    inv_freq = ROPE_THETA ** (-jnp.arange(0, r_axis, 2, dtype=F32) / r_axis)
    ang = jnp.stack([row, col], axis=-1).astype(F32)[:, :, None] * inv_freq
    cos, sin = jnp.cos(ang), jnp.sin(ang)
    cos64 = jnp.concatenate([cos[:, 0], cos[:, 0], cos[:, 1], cos[:, 1]], axis=-1)
    sin64 = jnp.concatenate([-sin[:, 0], sin[:, 0], -sin[:, 1], sin[:, 1]], axis=-1)
    cos64 = jnp.concatenate([jnp.ones((ctx_len, MLA_ROPE), F32), cos64], axis=0)
    sin64 = jnp.concatenate([jnp.zeros((ctx_len, MLA_ROPE), F32), sin64], axis=0)
    pad = ((0, 0), (0, 128 - MLA_ROPE))
    return jnp.pad(cos64, pad), jnp.pad(sin64, pad)


def _cos_sin(phase, period, scale):
    ang = (phase % period).astype(F32) * (2.0 * np.pi / period)
    return jnp.cos(ang) * scale, jnp.sin(ang) * scale


def _dft_dense_tables(t):
    idx = jnp.arange(t, dtype=jnp.int32)
    c, s = _cos_sin(idx[:, None] * idx[None, :], t, 1.0 / np.sqrt(t))
    return c.astype(BF16), s.astype(BF16)


def _dft_two_stage_tables(t):
    n1, n2 = _dft_split(t)
    u1 = jnp.arange(n1, dtype=jnp.int32)
    pos = jnp.arange(t, dtype=jnp.int32).reshape(n1, n2)
    c, s = _cos_sin(pos.T[:, None, :] * u1[None, :, None], t, 1.0 / np.sqrt(n1))
    g = jnp.concatenate([c, -s], axis=1).astype(BF16)
    i2 = jnp.arange(n2, dtype=jnp.int32)
    c2, s2 = _cos_sin(i2[:, None] * i2[None, :], n2, 1.0 / np.sqrt(n2))
    return g, c2.astype(BF16), s2.astype(BF16)


def kernel(x, c, ctx, c_ctx, ada_w, ada_b, norm_mix_w, norm_ffn_w, ev_w_in, ev_lb_logits, ev_onorm_w, ev_vnorm_w, ev_ws, ev_bs, ev_w_out, od_w_in, od_qa_norm_w, od_w_qb, od_kva_norm_w, od_w_kvb, od_q_norm_w, od_k_norm_w, od_w_out, ffn_w_up, ffn_conv_w, ffn_conv_b, ffn_w_down):
    bsz, t_lat, d = x.shape
    ctx_len = ctx.shape[1]
    depth = ada_w.shape[0]
    assert ctx_len == TM and t_lat % ATTN_KC == 0 and t_lat % GRID_W == 0 and bsz % NB == 0
    a_width = ev_lb_logits.shape[-1]
    ql, kvl = od_qa_norm_w.shape[-1], od_kva_norm_w.shape[-1]
    cw = od_w_in.shape[-1] - ql - kvl - MLA_ROPE
    heads = od_w_qb.shape[-1] // (MLA_NOPE + MLA_ROPE)
    d_ff = ffn_w_down.shape[1]

    pad_rows = (-(bsz + 1)) % 8
    cvec = jnp.concatenate([c, c_ctx[None, :], jnp.zeros((pad_rows, d), F32)], axis=0)
    mods = _ada_tables(cvec, ada_w, ada_b)
    mod_lat = mods[:, :bsz].reshape(depth, 1, bsz, 6, d)
    mod_ctx = jnp.broadcast_to(mods[:, bsz].reshape(depth, 1, 1, 6, d), (depth, 1, bsz, 6, d))
    mods = jnp.concatenate([mod_ctx, mod_lat], axis=1)

    lb_p = jax.nn.softmax(ev_lb_logits.astype(F32), axis=0)
    lbs = jnp.cumsum(lb_p, axis=0) - lb_p[0]
    cos_t, sin_t = _rope_tables(t_lat, ctx_len)
    dft_lat = dft_ctx = None

    xs = (ctx, x)
    for l in range(depth):
        last = l == depth - 1
        mod = mods[l]
        nmw = norm_mix_w[l].reshape(1, d)
        if l % 2 == 0:
            e = l // 2
            pz, pb = _even_in(xs, ctx_len + t_lat, mod, nmw, ev_w_in[e].astype(BF16), a_width)
            o_fwd, o_bwd = _hgrn_scan(pz, pb, lbs[e], ctx_len, a_width)
            bw = d - a_width
            gd = bw // ev_ws.shape[1]
            bsb = jnp.repeat(ev_bs[e].T, gd, axis=1)
            xs = _even_out(o_fwd, o_bwd, ev_onorm_w[e], pb, xs, mod, ev_vnorm_w[e].reshape(1, bw),
                           ev_ws[e].astype(BF16), bsb, ev_w_out[e].astype(BF16), a_width)
        else:
            o = l // 2
            if dft_lat is None:
                dft_lat, dft_ctx = _dft_two_stage_tables(t_lat), _dft_dense_tables(ctx_len)
            wts = _odd_weights(od_w_in[o], od_qa_norm_w[o], od_w_qb[o], od_kva_norm_w[o], od_w_kvb[o],
                               od_q_norm_w[o], od_k_norm_w[o], heads, cw, ql, kvl, cos_t, sin_t)
            xc_ctx, xc_lat, qt, k, vt = _odd_in(xs, mod, nmw, wts, heads, cw, ql, kvl)
            attn = _attention(qt, k, vt, ctx_len)
            fm_ctx = _dft_dense(xc_ctx, *dft_ctx)
            fm_lat = _dft_two_stage(xc_lat, dft_lat)
            xs = _odd_out(fm_ctx, fm_lat, attn, xs, mod, od_w_out[o].astype(BF16))
        xs = _conv_ffn(xs, mod, norm_ffn_w[l].reshape(1, d), ffn_w_up[l].astype(BF16), ffn_conv_w[l],
                       ffn_conv_b[l].reshape(1, d_ff), ffn_w_down[l].astype(BF16), skip_ctx=last)
    return xs
```

```python
import functools

import numpy as np
import jax
import jax.numpy as jnp
from jax import lax
from jax.experimental import pallas as pl
from jax.experimental.pallas import tpu as pltpu

F32 = jnp.float32
BF16 = jnp.bfloat16

EPS = 1e-6
F_MIN = 1e-6
LOG2_E = float(np.log2(np.e))
GRID_W = 64
ROPE_THETA = 10000.0
HGRN_HEAD_DIM = 128
HGRN_CHUNK = 64
HGRN_STEP_CHUNKS = 4
MLP_CHUNK = 128
MLA_NOPE = 128
MLA_ROPE = 64
MLA_V_DIM = 128
V_PAD = 16
FOURIER_GROUPS = 4
CONV_W = 3

TM = 256
NB = 2
HALO = 8
ATTN_KC = 512
ATTN_HEADS_PER_STEP = 3
V7X_VMEM_LIMIT = 56 * 1024 * 1024


def _cparams(sem, vmem=None):
    return pltpu.CompilerParams(dimension_semantics=sem, vmem_limit_bytes=vmem)


def _dot(a, b):
    return jnp.dot(a, b, preferred_element_type=F32)


def _dot_nt(a, b):
    return lax.dot_general(a, b, (((1,), (1,)), ((), ())), preferred_element_type=F32)


def _dot_tn(a, b):
    return lax.dot_general(a, b, (((0,), (0,)), ((), ())), preferred_element_type=F32)


def _silu(x):
    return x * jax.nn.sigmoid(x)


def _gelu_tanh(x):
    c = 0.7978845608028654
    return x * (0.5 + 0.5 * jnp.tanh(x * (c + (c * 0.044715) * (x * x))))


def _rms(x, n=None):
    n = x.shape[-1] if n is None else n
    return x * lax.rsqrt(jnp.sum(x * x, axis=-1, keepdims=True) * (1.0 / n) + EPS)


def _modulate(x, nw, shift, scale):
    return (_rms(x) * nw) * (1.0 + scale) + shift


def _mod_spec(d, nb):
    return pl.BlockSpec((None, nb, 6, d), lambda b, i: (jnp.minimum(i, 1), b, 0, 0))


def _rows_spec(cols, colblk=0):
    return pl.BlockSpec((NB, TM, cols), lambda b, i: (b, i, colblk))


def _const_spec(shape):
    nd = len(shape)
    return pl.BlockSpec(shape, lambda *_: (0,) * nd, pipeline_mode=pl.Buffered(1))


def _ada_kernel(c_ref, w_ref, b_ref, o_ref):
    s = _silu(c_ref[...]).astype(BF16)
    o_ref[...] = _dot(s, w_ref[...].astype(BF16)) + b_ref[...]


def _ada_tables(cvec, ada_w, ada_b):
    depth, d, n = ada_w.shape
    rows = cvec.shape[0]
    tn = 1536
    return pl.pallas_call(
        _ada_kernel,
        grid=(depth, n // tn),
        in_specs=[
            pl.BlockSpec((rows, d), lambda l, j: (0, 0)),
            pl.BlockSpec((None, d, tn), lambda l, j: (l, 0, j)),
            pl.BlockSpec((None, 1, tn), lambda l, j: (l, 0, j)),
        ],
        out_specs=pl.BlockSpec((None, rows, tn), lambda l, j: (l, 0, j)),
        out_shape=jax.ShapeDtypeStruct((depth, rows, n), F32),
        compiler_params=_cparams(("parallel", "parallel"), V7X_VMEM_LIMIT),
        name="ada_tables",
    )(cvec, ada_w, ada_b.reshape(depth, 1, n))


def _stream_specs(xs, d):
    if isinstance(xs, tuple):
        return ([pl.BlockSpec((NB, TM, d), lambda b, i: (b, 0, 0)),
                 pl.BlockSpec((NB, TM, d), lambda b, i: (b, jnp.maximum(i - 1, 0), 0))], list(xs))
    return [_rows_spec(d)], [xs]


def _stream_rows(x_refs, s):
    if len(x_refs) == 1:
        return x_refs[0][s]
    return jnp.where(pl.program_id(1) == 0, x_refs[0][s], x_refs[1][s])


def _even_in_kernel(*refs, n_x, a_width):
    x_refs = refs[:n_x]
    mod_ref, nw_ref, w_ref, oz_ref, ob_ref = refs[n_x:]
    nw = nw_ref[...]
    h = jnp.concatenate([_modulate(_stream_rows(x_refs, s), nw, mod_ref[s, 0:1, :], mod_ref[s, 1:2, :]).astype(BF16)
                         for s in range(NB)], axis=0)
    p = _dot(h, w_ref[...])
    z0, z1 = a_width * 2, a_width * 4
    for s in range(NB):
        rows = slice(s * TM, (s + 1) * TM)
        oz_ref[s] = p[rows, z0:z1]
        ob_ref[s] = jnp.concatenate([p[rows, :z0], p[rows, z1:]], axis=1).astype(ob_ref.dtype)


def _even_in(xs, tt, mod, nw, w_in, a_width):
    d, n = w_in.shape
    x_specs, x_args = _stream_specs(xs, d)
    b = x_args[0].shape[0]
    return pl.pallas_call(
        functools.partial(_even_in_kernel, n_x=len(x_args), a_width=a_width),
        grid=(b // NB, tt // TM),
        in_specs=x_specs + [
            _mod_spec(d, NB),
            _const_spec((1, d)),
            _const_spec((d, n)),
        ],
        out_specs=[_rows_spec(2 * a_width), _rows_spec(n - 2 * a_width)],
        out_shape=[jax.ShapeDtypeStruct((b, tt, 2 * a_width), F32),
                   jax.ShapeDtypeStruct((b, tt, n - 2 * a_width), BF16)],
        compiler_params=_cparams(("parallel", "parallel"), V7X_VMEM_LIMIT),
        name="even_in",
    )(*x_args, mod, nw, w_in)


def _hgrn_consts(reverse):
    c = HGRN_CHUNK
    idx = np.arange(c)
    if not reverse:
        tri = (idx[None, :] <= idx[:, None]).astype(np.float32)
    else:
        tri = (idx[None, :] >= idx[:, None]).astype(np.float32)
    masks = []
    for w in (32, 16, 8):
        grp = idx // (2 * w)
        qrow = (idx % (2 * w) >= w) if not reverse else (idx % (2 * w) < w)
        masks.append((grp[:, None] == grp[None, :]) & qrow[:, None] & (~qrow)[None, :])
    blk = idx // 8
    causal = (idx[None, :] <= idx[:, None]) if not reverse else (idx[None, :] >= idx[:, None])
    masks.append((blk[:, None] == blk[None, :]) & causal)
    return np.kron(np.eye(HGRN_STEP_CHUNKS), tri).astype(np.float32), np.stack(masks).astype(np.float32)


def _hgrn_prepare(pq_ref, pv_ref, pz_ref, lb, tri_ref, width):
    f = jnp.maximum(lb + (1.0 - lb) * jax.nn.sigmoid(pz_ref[...]), F_MIN)
    g = jnp.log(f) * LOG2_E
    g1 = g.astype(BF16)
    r1 = g - g1.astype(F32)
    g2 = r1.astype(BF16)
    g3 = (r1 - g2.astype(F32)).astype(BF16)
    b3 = _dot(tri_ref[...], jnp.concatenate([g1, g2, g3], axis=1))
    b_all = b3[:, :width] + b3[:, width:2 * width] + b3[:, 2 * width:]
    return _silu(pq_ref[...].astype(F32)).astype(BF16), (1.0 - f).astype(BF16), pv_ref[...], b_all


def _hgrn_chunk(feats, ci, reverse, masks_ref, st, heads):
    c, hd = HGRN_CHUNK, HGRN_HEAD_DIM
    width = heads * hd
    q, kk, v, b_all = feats
    rs = slice(ci * c, (ci + 1) * c)
    last = c - 1 if not reverse else 0
    parts = []
    for h in range(heads):
        sl = slice(h * hd, (h + 1) * hd)
        qh, kh, b = q[rs, sl], kk[rs, sl], b_all[rs, sl]

        def ref_rows(rows, n):
            return jnp.concatenate([jnp.broadcast_to(b[r:r + 1, :], (n, hd)) for r in rows], axis=0)

        a = None
        for lv, w in enumerate((32, 16, 8)):
            mids = [gi * 2 * w + (w - 1 if not reverse else w) for gi in range(c // (2 * w))]
            ew = jnp.exp2(-jnp.abs(b - ref_rows(mids, 2 * w))).astype(BF16)
            t = _dot_nt(qh * ew, kh * ew) * masks_ref[lv]
            a = t if a is None else a + t
        anchor = ref_rows([8 * m + (3 if not reverse else 4) for m in range(c // 8)], 8)
        d8 = _dot_nt(qh * jnp.exp2(b - anchor).astype(BF16), kh * jnp.exp2(anchor - b).astype(BF16))
        a = a + jnp.where(masks_ref[3] > 0.5, d8, 0.0)
        vb = v[rs, sl]
        e_in = jnp.exp2(b)
        qhat = qh * e_in.astype(BF16)
        khat = kh * jnp.exp2(ref_rows([last], c) - b).astype(BF16)
        parts.append((a.astype(BF16), vb, qhat, _dot_tn(vb, khat), e_in[last:last + 1, :]))
    heads_out = []
    for h, (a, vb, qhat, upd, e_last) in enumerate(parts):
        heads_out.append(_dot(a, vb) + _dot_nt(qhat, st[h].astype(BF16)))
        st[h] = st[h] * e_last + upd
    return jnp.concatenate(heads_out, axis=1)


def _hgrn_kernel(fq_ref, fv_ref, fz_ref, bq_ref, bv_ref, bz_ref, lb_ref, ftri_ref, btri_ref, fmask_ref, bmask_ref,
                 of_ref, ob_ref, fst_ref, bst_ref, *, heads):
    nck = HGRN_STEP_CHUNKS
    width = heads * HGRN_HEAD_DIM

    @pl.when(pl.program_id(1) == 0)
    def _():
        fst_ref[...] = jnp.zeros_like(fst_ref)
        bst_ref[...] = jnp.zeros_like(bst_ref)

    ffeats = _hgrn_prepare(fq_ref, fv_ref, fz_ref, lb_ref[0:1, :], ftri_ref, width)
    bfeats = _hgrn_prepare(bq_ref, bv_ref, bz_ref, lb_ref[1:2, :], btri_ref, width)
    fst = [fst_ref[h] for h in range(heads)]
    bst = [bst_ref[h] for h in range(heads)]
    fouts, bouts = [None] * nck, [None] * nck
    for k in range(nck):
        fouts[k] = _hgrn_chunk(ffeats, k, False, fmask_ref, fst, heads)
        bouts[nck - 1 - k] = _hgrn_chunk(bfeats, nck - 1 - k, True, bmask_ref, bst, heads)
    for h in range(heads):
        fst_ref[h] = fst[h]
        bst_ref[h] = bst[h]
    of_ref[...] = jnp.concatenate(fouts, axis=0).astype(of_ref.dtype)
    ob_ref[...] = jnp.concatenate(bouts, axis=0).astype(ob_ref.dtype)


def _hgrn_scan(pz, pb, lbs, ctx_len, width):
    b, tt, _ = pz.shape
    rows = HGRN_CHUNK * HGRN_STEP_CHUNKS
    heads = width // HGRN_HEAD_DIM
    nb, nbc = tt // rows, ctx_len // rows
    ftri, fmask = _hgrn_consts(False)
    btri, bmask = _hgrn_consts(True)

    def bblk(j):
        return jnp.where(j < nbc, nbc - 1 - j, nb - 1 - (j - nbc))

    def fcol(k):
        return pl.BlockSpec((None, rows, width), lambda b, j: (b, j, k))

    def bcol(k):
        return pl.BlockSpec((None, rows, width), lambda b, j: (b, bblk(j), k))

    consts = [lbs, jnp.asarray(ftri, BF16), jnp.asarray(btri, BF16), jnp.asarray(fmask, F32), jnp.asarray(bmask, F32)]
    state = pltpu.VMEM((heads, HGRN_HEAD_DIM, HGRN_HEAD_DIM), F32)
    return pl.pallas_call(
        functools.partial(_hgrn_kernel, heads=heads),
        grid=(b, nb),
        in_specs=[fcol(0), fcol(1), fcol(0), bcol(0), bcol(1), bcol(1)] + [_const_spec(a.shape) for a in consts],
        out_specs=[fcol(0), bcol(0)],
        out_shape=[jax.ShapeDtypeStruct((b, tt, width), BF16)] * 2,
        scratch_shapes=[state, state],
        compiler_params=_cparams(("parallel", "arbitrary"), V7X_VMEM_LIMIT),
        name="hgrn_scan",
    )(pb, pb, pz, pb, pb, pz, *consts)


def _even_out_kernel(*refs, groups, n_x):
    of_ref, ob_ref, pg_ref, pu_ref, pv_ref = refs[:5]
    x_refs = refs[5:5 + n_x]
    mod_ref, onw_ref, vnw_ref, ws_ref, bsb_ref, wa_ref, wb_ref, o_ref = refs[5 + n_x:]
    gd = pu_ref.shape[2] // groups
    hd = HGRN_HEAD_DIM
    onw = onw_ref[...]
    ms, yas = [], []
    for s in range(NB):
        o = of_ref[s].astype(F32) + ob_ref[s].astype(F32)
        yn = jnp.concatenate([_rms(o[:, h * hd:(h + 1) * hd]) * onw for h in range(o.shape[1] // hd)], axis=1)
        yas.append((yn * _silu(pg_ref[s].astype(F32))).astype(BF16))
    y = _dot(jnp.concatenate(yas, axis=0), wa_ref[...])
    for s in range(NB):
        for n in range(TM // MLP_CHUNK):
            rows = slice(n * MLP_CHUNK, (n + 1) * MLP_CHUNK)
            u = _gelu_tanh(pu_ref[s, rows, :].astype(F32))
            v = _gelu_tanh(pv_ref[s, rows, :].astype(F32))
            parts = []
            for g in range(groups):
                cs = slice(g * gd, (g + 1) * gd)
                vg = (_rms(v[:, cs]) * vnw_ref[:, cs]).astype(BF16)
                sv = _dot(ws_ref[g], vg) + bsb_ref[:, cs]
                parts.append(u[:, cs] * sv)
            ms.append(jnp.concatenate(parts, axis=1).astype(BF16))
    y = y + _dot(jnp.concatenate(ms, axis=0), wb_ref[...])
    for s in range(NB):
        o_ref[s] = _stream_rows(x_refs, s) + mod_ref[s, 2:3, :] * y[s * TM:(s + 1) * TM]


def _even_out(o_fwd, o_bwd, onw, pb, xs, mod, vnw, ws, bsb, w_out, a_width):
    b, tt, _ = pb.shape
    d = w_out.shape[1]
    bw = d - a_width
    groups = ws.shape[0]
    ucol = 3 * a_width // bw
    x_specs, x_args = _stream_specs(xs, d)
    return pl.pallas_call(
        functools.partial(_even_out_kernel, groups=groups, n_x=len(x_args)),
        grid=(b // NB, tt // TM),
        in_specs=[
            _rows_spec(a_width),
            _rows_spec(a_width),
            _rows_spec(a_width, 2),
            _rows_spec(bw, ucol),
            _rows_spec(bw, ucol + 1)] + x_specs + [
            _mod_spec(d, NB),
            _const_spec((1, HGRN_HEAD_DIM)),
            _const_spec((1, bw)),
            _const_spec(ws.shape),
            _const_spec(bsb.shape),
            _const_spec((a_width, d)),
            _const_spec((bw, d)),
        ],
        out_specs=_rows_spec(d),
        out_shape=jax.ShapeDtypeStruct((b, tt, d), F32),
        compiler_params=_cparams(("parallel", "parallel"), V7X_VMEM_LIMIT),
        name="even_out",
    )(o_fwd, o_bwd, pb, pb, pb, *x_args, mod, onw.reshape(1, HGRN_HEAD_DIM), vnw, ws, bsb,
      w_out[:a_width], w_out[a_width:])


def _odd_in_kernel(x_ref, mod_ref, nw_ref, win_ref, qaw_ref, kvaw_ref, kpw_ref, kpsw_ref, wq_ref, qnw_ref,
                   qpw_ref, qpsw_ref, wkv_ref, knw_ref, cos_ref, sin_ref, bc_ref,
                   xc_ctx_ref, xc_lat_ref, qt_ref, k_ref, vt_ref, *, heads, cw, ql, kvl, scale):
    cos, sin = cos_ref[...], sin_ref[...]
    rope = MLA_ROPE
    o = cw + ql + kvl
    hw = heads * 128
    ones_pad = (lax.broadcasted_iota(jnp.int32, (V_PAD, TM), 0) == 0).astype(F32)
    is_ctx = pl.program_id(1) == 0

    def project(s):
        h = _modulate(x_ref[s], nw_ref[...], mod_ref[s, 0:1, :], mod_ref[s, 1:2, :])
        return _dot(h.astype(BF16), win_ref[...])

    def latents(s, p):
        xc = _dot(p[:, :cw].astype(BF16), bc_ref[...])
        xc_lat_ref[s] = xc.astype(xc_lat_ref.dtype)

        @pl.when(is_ctx)
        def _():
            xc_ctx_ref[s] = xc.astype(xc_ctx_ref.dtype)
        q_lat = (_rms(p[:, cw:cw + ql]) * qaw_ref[...]).astype(BF16)
        kv_lat = (_rms(p[:, cw + ql:cw + ql + kvl]) * kvaw_ref[...]).astype(BF16)
        kp, kps = p[:, o:o + 128], p[:, o + 128:o + 256]
        kpr = lax.rsqrt(jnp.sum(kp * kp, axis=-1, keepdims=True) * (1.0 / rope) + EPS)
        k_pe = (kp * kpw_ref[...] * cos + kps * kpsw_ref[...] * sin) * kpr
        qf = _dot(q_lat, wq_ref[...])
        kvf = _dot(kv_lat, wkv_ref[...])
        return qf, kvf, k_pe

    def heads_out(s, qf, kvf, k_pe):
        for hh in range(heads):
            cs = slice(hh * 128, (hh + 1) * 128)
            qn = _rms(qf[:, cs]) * qnw_ref[...]
            qp = qf[:, hw + hh * 128:hw + (hh + 1) * 128]
            qps = qf[:, 2 * hw + hh * 128:2 * hw + (hh + 1) * 128]
            qpr = lax.rsqrt(jnp.sum(qp * qp, axis=-1, keepdims=True) * (1.0 / rope) + EPS)
            q_pe = (qp * qpw_ref[...] * cos + qps * qpsw_ref[...] * sin) * qpr
            qt_ref[s, hh] = (jnp.concatenate([qn, q_pe], axis=1) * scale).T.astype(qt_ref.dtype)
            kn = _rms(kvf[:, cs]) * knw_ref[...]
            k_ref[s, hh] = jnp.concatenate([kn, k_pe], axis=1).astype(k_ref.dtype)
            vt_ref[s, hh] = jnp.concatenate([kvf[:, hw + hh * 128:hw + (hh + 1) * 128].T, ones_pad],
                                            axis=0).astype(vt_ref.dtype)

    mid = latents(0, project(0))
    for s in range(1, NB):
        p_next = project(s)
        heads_out(s - 1, *mid)
        mid = latents(s, p_next)
    heads_out(NB - 1, *mid)


def _odd_in(x, mod, nw, wts, heads, cw, ql, kvl):
    b, tt, d = x.shape
    scale = float(MLA_NOPE + MLA_ROPE) ** -0.5 * LOG2_E
    consts = [wts[k] for k in ("w_in", "qa_w", "kva_w", "kp_w", "kps_w", "w_q", "qn_w", "qp_w", "qps_w",
                               "w_kv", "kn_w")]
    row128 = pl.BlockSpec((TM, 128), lambda b, i: (i, 0))
    in_specs = ([_rows_spec(d), _mod_spec(d, NB), _const_spec((1, d))]
                + [_const_spec(a.shape) for a in consts] + [row128, row128, _const_spec(wts["bc"].shape)])
    return pl.pallas_call(
        functools.partial(_odd_in_kernel, heads=heads, cw=cw, ql=ql, kvl=kvl, scale=scale),
        grid=(b // NB, tt // TM),
        in_specs=in_specs,
        out_specs=[
            pl.BlockSpec((NB, TM, 2 * cw), lambda b, i: (b, 0, 0)),
            pl.BlockSpec((NB, TM, 2 * cw), lambda b, i: (b, jnp.maximum(i - 1, 0), 0)),
            pl.BlockSpec((NB, heads, 256, TM), lambda b, i: (b, 0, 0, i)),
            pl.BlockSpec((NB, heads, TM, 256), lambda b, i: (b, 0, i, 0)),
            pl.BlockSpec((NB, heads, MLA_V_DIM + V_PAD, TM), lambda b, i: (b, 0, 0, i)),
        ],
        out_shape=[
            jax.ShapeDtypeStruct((b, TM, 2 * cw), BF16),
            jax.ShapeDtypeStruct((b, tt - TM, 2 * cw), F32),
            jax.ShapeDtypeStruct((b, heads, 256, tt), BF16),
            jax.ShapeDtypeStruct((b, heads, tt, 256), BF16),
            jax.ShapeDtypeStruct((b, heads, MLA_V_DIM + V_PAD, tt), BF16),
        ],
        compiler_params=_cparams(("parallel", "arbitrary"), V7X_VMEM_LIMIT),
        name="odd_in",
    )(x, mod, nw, *consts, wts["cos"], wts["sin"], wts["bc"])


def _attn_kernel(qt_ref, k_ref, vt_ref, o_ref, sa_ref, sb_ref, ma_ref, mb_ref, *, ctx_len, n_tiles):
    i = pl.program_id(2)
    n_all = k_ref.shape[1]
    nh = k_ref.shape[0]
    bufs = ((sa_ref, ma_ref), (sb_ref, mb_ref))

    def chunks(n_keys):
        return [(st, min(ATTN_KC, n_keys - st)) for st in range(0, n_keys, ATTN_KC)]

    def run(step_parity, keys1, keys2):
        s_w, m_w = bufs[step_parity]
        s_r, m_r = bufs[1 - step_parity]
        c1 = chunks(keys1) if keys1 else []
        c2 = chunks(keys2) if keys2 else []
        if keys1:
            qts = [qt_ref[hh] for hh in range(nh)]
        if keys2:
            m_prev = [m_r[hh] for hh in range(nh)]
        m, acc = [None] * nh, [None] * nh
        for idx in range(max(len(c1), len(c2))):
            for hh in range(nh):
                if idx < len(c1):
                    st, sz = c1[idx]
                    s = _dot(k_ref[hh, st:st + sz, :], qts[hh])
                    s_w[hh, st:st + sz, :] = s
                    cm = jnp.max(s, axis=0, keepdims=True)
                    m[hh] = cm if m[hh] is None else jnp.maximum(m[hh], cm)
                if idx < len(c2):
                    st, sz = c2[idx]
                    p = jnp.exp2(s_r[hh, st:st + sz, :] - m_prev[hh]).astype(BF16)
                    ca = _dot(vt_ref[hh, :, st:st + sz], p)
                    acc[hh] = ca if acc[hh] is None else acc[hh] + ca
        dv = o_ref.shape[1] // nh
        for hh in range(nh):
            if keys1:
                m_w[hh] = m[hh]
            if keys2:
                o_ref[:, hh * dv:(hh + 1) * dv] = (acc[hh][:dv] * (1.0 / acc[hh][dv:dv + 1])).T.astype(o_ref.dtype)

    @pl.when(i == 0)
    def _():
        run(0, ctx_len, None)

    @pl.when(i == 1)
    def _():
        run(1, n_all, ctx_len)

    for parity in (0, 1):
        @pl.when(jnp.logical_and(jnp.logical_and(i >= 2, i < n_tiles), i % 2 == parity))
        def _():
            run(parity, n_all, n_all)

    @pl.when(i == n_tiles)
    def _():
        run(n_tiles % 2, None, n_all)


def _attention(qt, k, vt, ctx_len):
    b, heads, tt, dq = k.shape
    dvp = vt.shape[2]
    dv = dvp - V_PAD
    n_tiles = tt // TM
    nh = ATTN_HEADS_PER_STEP
    assert heads % nh == 0
    return pl.pallas_call(
        functools.partial(_attn_kernel, ctx_len=ctx_len, n_tiles=n_tiles),
        grid=(b, heads // nh, n_tiles + 1),
        in_specs=[
            pl.BlockSpec((None, nh, dq, TM), lambda b, h, i: (b, h, 0, jnp.minimum(i, n_tiles - 1))),
            pl.BlockSpec((None, nh, tt, dq), lambda b, h, i: (b, h, 0, 0)),
            pl.BlockSpec((None, nh, dvp, tt), lambda b, h, i: (b, h, 0, 0)),
        ],
        out_specs=pl.BlockSpec((None, TM, nh * dv), lambda b, h, i: (b, jnp.maximum(i - 1, 0), h)),
        out_shape=jax.ShapeDtypeStruct((b, tt, heads * dv), BF16),
        scratch_shapes=[pltpu.VMEM((nh, tt, TM), F32), pltpu.VMEM((nh, tt, TM), F32),
                        pltpu.VMEM((nh, 1, TM), F32), pltpu.VMEM((nh, 1, TM), F32)],
        compiler_params=_cparams(("parallel", "parallel", "arbitrary"), V7X_VMEM_LIMIT),
        name="attention",
    )(qt, k, vt)


def _dft_kernel(c_ref, s_ref, x_ref, o_ref):
    half = x_ref.shape[1] // 2
    y = _dot(c_ref[...], x_ref[:, :half]) + _dot(s_ref[...], x_ref[:, half:])
    o_ref[...] = y.astype(o_ref.dtype)


def _dft_dense(xc, t_cos, t_sin):
    b, t, w2 = xc.shape
    return pl.pallas_call(
        _dft_kernel,
        grid=(b,),
        in_specs=[_const_spec((t, t)), _const_spec((t, t)), pl.BlockSpec((None, t, w2), lambda b: (b, 0, 0))],
        out_specs=pl.BlockSpec((None, t, w2 // 2), lambda b: (b, 0, 0)),
        out_shape=jax.ShapeDtypeStruct((b, t, w2 // 2), BF16),
        compiler_params=_cparams(("parallel",), V7X_VMEM_LIMIT),
        name="dft_dense",
    )(t_cos, t_sin, xc)


def _dft_split(t):
    n2 = 1 << ((t.bit_length()) // 2)
    return t // n2, n2


def _swap_major(x):
    return pltpu.einshape("abc->bac", x)


def _dft_two_stage_kernel(xr_ref, xi_ref, g_ref, cs2_ref, o_ref, zr_ref, zi_ref, y_ref, *, n1, n2):
    lw = xr_ref.shape[1]
    xr = _swap_major(xr_ref[...].reshape(n1, n2, lw))
    xi = _swap_major(xi_ref[...].reshape(n1, n2, lw))
    for t2 in range(n2):
        p = _dot(g_ref[t2], jnp.concatenate([xr[t2], xi[t2]], axis=1).astype(BF16))
        zr_ref[t2] = p[:n1, :lw] - p[n1:, lw:]
        zi_ref[t2] = p[:n1, lw:] + p[n1:, :lw]
    zr = _swap_major(zr_ref[...])
    zi = _swap_major(zi_ref[...])
    for u1 in range(n1):
        y_ref[u1] = _dot(cs2_ref[...], jnp.concatenate([zr[u1], zi[u1]], axis=0).astype(BF16))
    o_ref[...] = _swap_major(y_ref[...]).reshape(n1 * n2, lw).astype(o_ref.dtype)


def _dft_two_stage(xc, tables):
    g, cs2 = tables
    b, t, w2 = xc.shape
    cw = w2 // 2
    lw = 128
    n1, n2 = _dft_split(t)
    return pl.pallas_call(
        functools.partial(_dft_two_stage_kernel, n1=n1, n2=n2),
        grid=(b, cw // lw),
        in_specs=[pl.BlockSpec((None, t, lw), lambda b, j: (b, 0, j)),
                  pl.BlockSpec((None, t, lw), lambda b, j: (b, 0, cw // lw + j)),
                  _const_spec(g.shape), _const_spec(cs2.shape)],
        out_specs=pl.BlockSpec((None, t, lw), lambda b, j: (b, 0, j)),
        out_shape=jax.ShapeDtypeStruct((b, t, cw), BF16),
        scratch_shapes=[pltpu.VMEM((n2, n1, lw), F32), pltpu.VMEM((n2, n1, lw), F32), pltpu.VMEM((n1, n2, lw), F32)],
        compiler_params=_cparams(("parallel", "parallel"), V7X_VMEM_LIMIT),
        name="dft_two_stage",
    )(xc, xc, g, cs2)


def _odd_out_kernel(fc_ref, fl_ref, at_ref, x_ref, mod_ref, wf_ref, wa_ref, o_ref):
    is_ctx = pl.program_id(1) == 0
    fm = jnp.concatenate([jnp.where(is_ctx, fc_ref[s], fl_ref[s]) for s in range(NB)], axis=0)
    at = jnp.concatenate([at_ref[s] for s in range(NB)], axis=0)
    y = _dot(fm, wf_ref[...]) + _dot(at, wa_ref[...])
    for s in range(NB):
        o_ref[s] = x_ref[s] + mod_ref[s, 2:3, :] * y[s * TM:(s + 1) * TM]


def _odd_out(fm_ctx, fm_lat, attn, x, mod, w_out):
    b, tt, d = x.shape
    cw = fm_ctx.shape[-1]
    aw = attn.shape[-1]
    return pl.pallas_call(
        _odd_out_kernel,
        grid=(b // NB, tt // TM),
        in_specs=[
            pl.BlockSpec((NB, TM, cw), lambda b, i: (b, 0, 0)),
            pl.BlockSpec((NB, TM, cw), lambda b, i: (b, jnp.maximum(i - 1, 0), 0)),
            _rows_spec(aw),
            _rows_spec(d),
            _mod_spec(d, NB),
            _const_spec((cw, d)),
            _const_spec((aw, d)),
        ],
        out_specs=_rows_spec(d),
        out_shape=jax.ShapeDtypeStruct((b, tt, d), F32),
        compiler_params=_cparams(("parallel", "parallel"), V7X_VMEM_LIMIT),
        name="odd_out",
    )(fm_ctx, fm_lat, attn, x, mod, w_out[:cw], w_out[cw:])


def _ffn_kernel(xp_ref, x_ref, xn_ref, mod_ref, nw_ref, wg_ref, wv_ref, cw_ref, cb_ref, wd_ref, o_ref,
                *, first_tile, n_tiles):
    i = pl.program_id(1) + first_tile
    ext = TM + 2 * HALO
    row = lax.broadcasted_iota(jnp.int32, (ext, 1), 0)
    keep = jnp.logical_and(jnp.logical_or(row >= HALO, i > 1),
                           jnp.logical_or(row < HALO + TM, jnp.logical_and(i > 0, i < n_tiles - 1)))
    nw = nw_ref[...]
    hes = []
    for s in range(NB):
        xe = jnp.concatenate([xp_ref[s], x_ref[s], xn_ref[s]], axis=0)
        hes.append(jnp.where(keep, _modulate(xe, nw, mod_ref[s, 3:4, :], mod_ref[s, 4:5, :]), 0.0))
    ge = _dot(jnp.concatenate(hes, axis=0).astype(BF16), wg_ref[...])
    val = _dot(jnp.concatenate([h[HALO:HALO + TM] for h in hes], axis=0).astype(BF16), wv_ref[...])
    cw0, cw1, cw2 = cw_ref[0:1, :], cw_ref[1:2, :], cw_ref[2:3, :]
    gc = jnp.concatenate(
        [ge[s * ext + HALO - 1:s * ext + HALO - 1 + TM] * cw0 + ge[s * ext + HALO:s * ext + HALO + TM] * cw1
         + ge[s * ext + HALO + 1:s * ext + HALO + 1 + TM] * cw2 for s in range(NB)], axis=0) + cb_ref[...]
    y = _dot((_silu(gc) * val).astype(BF16), wd_ref[...])
    for s in range(NB):
        o_ref[s] = x_ref[s] + mod_ref[s, 5:6, :] * y[s * TM:(s + 1) * TM]


def _conv_ffn(x, mod, nw, w_up, conv_w, conv_b, wd, skip_ctx):
    b, tt, d = x.shape
    ff = wd.shape[0]
    n_tiles = tt // TM
    first = 1 if skip_ctx else 0
    r = TM // HALO
    nblk = tt // HALO
    return pl.pallas_call(
        functools.partial(_ffn_kernel, first_tile=first, n_tiles=n_tiles),
        grid=(b // NB, n_tiles - first),
        in_specs=[
            pl.BlockSpec((NB, HALO, d), lambda b, i: (b, jnp.maximum((i + first) * r - 1, 0), 0)),
            pl.BlockSpec((NB, TM, d), lambda b, i: (b, i + first, 0)),
            pl.BlockSpec((NB, HALO, d), lambda b, i: (b, jnp.minimum((i + first + 1) * r, nblk - 1), 0)),
            pl.BlockSpec((None, NB, 6, d), lambda b, i: (jnp.minimum(i + first, 1), b, 0, 0)),
            _const_spec((1, d)),
            pl.BlockSpec((d, ff), lambda b, i: (0, 0), pipeline_mode=pl.Buffered(1)),
            pl.BlockSpec((d, ff), lambda b, i: (0, 1), pipeline_mode=pl.Buffered(1)),
            _const_spec((CONV_W, ff)),
            _const_spec((1, ff)),
            _const_spec((ff, d)),
        ],
        out_specs=_rows_spec(d),
        out_shape=jax.ShapeDtypeStruct((b, tt - first * TM, d), F32),
        compiler_params=_cparams(("parallel", "parallel"), V7X_VMEM_LIMIT),
        name="conv_ffn",
    )(x, x, x, mod, nw, w_up, w_up, conv_w, conv_b, wd)


def _rope_swap_perm():
    q = MLA_ROPE // 4
    return np.concatenate([np.arange(q, 2 * q), np.arange(0, q), np.arange(3 * q, 4 * q), np.arange(2 * q, 3 * q)])


def _pad128(v):
    return jnp.pad(v, (0, 128 - v.shape[0])).reshape(1, 128)


def _odd_weights(w_in, qa_w, w_qb, kva_w, w_kvb, qn_w, kn_w, heads, cw, ql, kvl, cos_t, sin_t):
    d = w_in.shape[0]
    perm = _rope_swap_perm()
    nope, rope, qk = MLA_NOPE, MLA_ROPE, MLA_NOPE + MLA_ROPE
    o = cw + ql + kvl
    z = jnp.zeros((d, 128 - rope), w_in.dtype)
    kpe = w_in[:, o:o + rope]
    w_in_ext = jnp.concatenate([w_in[:, :o], kpe, z, kpe[:, perm], z], axis=1).astype(BF16)
    wq = w_qb.reshape(ql, heads, qk)
    zq = jnp.zeros((ql, heads, 128 - rope), w_qb.dtype)
    wq_rope = wq[:, :, nope:]
    w_q = jnp.concatenate([
        wq[:, :, :nope].reshape(ql, heads * 128),
        jnp.concatenate([wq_rope, zq], axis=2).reshape(ql, heads * 128),
        jnp.concatenate([wq_rope[:, :, perm], zq], axis=2).reshape(ql, heads * 128),
    ], axis=1).astype(BF16)
    wkv = w_kvb.reshape(kvl, heads, nope + MLA_V_DIM)
    w_kv = jnp.concatenate([wkv[:, :, :nope].reshape(kvl, heads * nope),
                            wkv[:, :, nope:].reshape(kvl, heads * MLA_V_DIM)], axis=1).astype(BF16)
    gd = cw // FOURIER_GROUPS
    jk = (np.arange(gd)[:, None] * np.arange(gd)[None, :]) % gd
    ang = 2.0 * np.pi * jk / gd
    eye = np.eye(FOURIER_GROUPS)
    bc = np.concatenate([np.kron(eye, np.cos(ang)), np.kron(eye, -np.sin(ang))], axis=1) / np.sqrt(gd)
    return {
        "w_in": w_in_ext, "qa_w": qa_w.reshape(1, ql), "kva_w": kva_w.reshape(1, kvl),
        "kp_w": _pad128(kn_w[nope:]), "kps_w": _pad128(kn_w[nope:][perm]),
        "w_q": w_q, "qn_w": qn_w[:nope].reshape(1, nope),
        "qp_w": _pad128(qn_w[nope:]), "qps_w": _pad128(qn_w[nope:][perm]),
        "w_kv": w_kv, "kn_w": kn_w[:nope].reshape(1, nope),
        "cos": cos_t, "sin": sin_t, "bc": jnp.asarray(bc, F32).astype(BF16),
    }


def _rope_tables(t_lat, ctx_len):
    rows = t_lat // GRID_W
    row = jnp.repeat(jnp.arange(rows), GRID_W)
    col = jnp.tile(jnp.arange(GRID_W), rows)
    r_axis = MLA_ROPE // 2
    inv_freq = ROPE_THETA ** (-jnp.arange(0, r_axis, 2, dtype=F32) / r_axis)
    ang = jnp.stack([row, col], axis=-1).astype(F32)[:, :, None] * inv_freq
    cos, sin = jnp.cos(ang), jnp.sin(ang)
    cos64 = jnp.concatenate([cos[:, 0], cos[:, 0], cos[:, 1], cos[:, 1]], axis=-1)
    sin64 = jnp.concatenate([-sin[:, 0], sin[:, 0], -sin[:, 1], sin[:, 1]], axis=-1)
    cos64 = jnp.concatenate([jnp.ones((ctx_len, MLA_ROPE), F32), cos64], axis=0)
    sin64 = jnp.concatenate([jnp.zeros((ctx_len, MLA_ROPE), F32), sin64], axis=0)
    pad = ((0, 0), (0, 128 - MLA_ROPE))
    return jnp.pad(cos64, pad), jnp.pad(sin64, pad)


def _cos_sin(phase, period, scale):
    ang = (phase % period).astype(F32) * (2.0 * np.pi / period)
    return jnp.cos(ang) * scale, jnp.sin(ang) * scale


def _dft_dense_tables(t):
    idx = jnp.arange(t, dtype=jnp.int32)
    c, s = _cos_sin(idx[:, None] * idx[None, :], t, 1.0 / np.sqrt(t))
    return c.astype(BF16), s.astype(BF16)


def _dft_two_stage_tables(t):
    n1, n2 = _dft_split(t)
    u1 = jnp.arange(n1, dtype=jnp.int32)
    pos = jnp.arange(t, dtype=jnp.int32).reshape(n1, n2)
    c, s = _cos_sin(pos.T[:, None, :] * u1[None, :, None], t, 1.0 / np.sqrt(n1))
    g = jnp.concatenate([c, -s], axis=1).astype(BF16)
    i2 = jnp.arange(n2, dtype=jnp.int32)
    c2, s2 = _cos_sin(i2[:, None] * i2[None, :], n2, 1.0 / np.sqrt(n2))
    return g, jnp.concatenate([c2, s2], axis=1).astype(BF16)


def kernel(x, c, ctx, c_ctx, ada_w, ada_b, norm_mix_w, norm_ffn_w, ev_w_in, ev_lb_logits, ev_onorm_w, ev_vnorm_w, ev_ws, ev_bs, ev_w_out, od_w_in, od_qa_norm_w, od_w_qb, od_kva_norm_w, od_w_kvb, od_q_norm_w, od_k_norm_w, od_w_out, ffn_w_up, ffn_conv_w, ffn_conv_b, ffn_w_down):
    bsz, t_lat, d = x.shape
    ctx_len = ctx.shape[1]
    depth = ada_w.shape[0]
    assert ctx_len == TM and t_lat % ATTN_KC == 0 and t_lat % GRID_W == 0 and bsz % NB == 0
    a_width = ev_lb_logits.shape[-1]
    ql, kvl = od_qa_norm_w.shape[-1], od_kva_norm_w.shape[-1]
    cw = od_w_in.shape[-1] - ql - kvl - MLA_ROPE
    heads = od_w_qb.shape[-1] // (MLA_NOPE + MLA_ROPE)
    d_ff = ffn_w_down.shape[1]

    pad_rows = (-(bsz + 1)) % 8
    cvec = jnp.concatenate([c, c_ctx[None, :], jnp.zeros((pad_rows, d), F32)], axis=0)
    mods = _ada_tables(cvec, ada_w, ada_b)
    mod_lat = mods[:, :bsz].reshape(depth, 1, bsz, 6, d)
    mod_ctx = jnp.broadcast_to(mods[:, bsz].reshape(depth, 1, 1, 6, d), (depth, 1, bsz, 6, d))
    mods = jnp.concatenate([mod_ctx, mod_lat], axis=1)

    lb_p = jax.nn.softmax(ev_lb_logits.astype(F32), axis=0)
    lbs = jnp.cumsum(lb_p, axis=0) - lb_p[0]
    cos_t, sin_t = _rope_tables(t_lat, ctx_len)
    dft_lat = dft_ctx = None

    xs = (ctx, x)
    for l in range(depth):
        last = l == depth - 1
        mod = mods[l]
        nmw = norm_mix_w[l].reshape(1, d)
        if l % 2 == 0:
            e = l // 2
            pz, pb = _even_in(xs, ctx_len + t_lat, mod, nmw, ev_w_in[e].astype(BF16), a_width)
            o_fwd, o_bwd = _hgrn_scan(pz, pb, lbs[e], ctx_len, a_width)
            bw = d - a_width
            gd = bw // ev_ws.shape[1]
            bsb = jnp.repeat(ev_bs[e].T, gd, axis=1)
            xs = _even_out(o_fwd, o_bwd, ev_onorm_w[e], pb, xs, mod, ev_vnorm_w[e].reshape(1, bw),
                           ev_ws[e].astype(BF16), bsb, ev_w_out[e].astype(BF16), a_width)
        else:
            o = l // 2
            if dft_lat is None:
                dft_lat, dft_ctx = _dft_two_stage_tables(t_lat), _dft_dense_tables(ctx_len)
            wts = _odd_weights(od_w_in[o], od_qa_norm_w[o], od_w_qb[o], od_kva_norm_w[o], od_w_kvb[o],
                               od_q_norm_w[o], od_k_norm_w[o], heads, cw, ql, kvl, cos_t, sin_t)
            xc_ctx, xc_lat, qt, k, vt = _odd_in(xs, mod, nmw, wts, heads, cw, ql, kvl)
            attn = _attention(qt, k, vt, ctx_len)
            fm_ctx = _dft_dense(xc_ctx, *dft_ctx)
            fm_lat = _dft_two_stage(xc_lat, dft_lat)
            xs = _odd_out(fm_ctx, fm_lat, attn, xs, mod, od_w_out[o].astype(BF16))
        xs = _conv_ffn(xs, mod, norm_ffn_w[l].reshape(1, d), ffn_w_up[l].astype(BF16), ffn_conv_w[l],
                       ffn_conv_b[l].reshape(1, d_ff), ffn_w_down[l].astype(BF16), skip_ctx=last)
    return xs
```

```python
import functools

import numpy as np
import jax
import jax.numpy as jnp
from jax import lax
from jax.experimental import pallas as pl
from jax.experimental.pallas import tpu as pltpu

F32 = jnp.float32
BF16 = jnp.bfloat16

EPS = 1e-6
F_MIN = 1e-6
LOG2_E = float(np.log2(np.e))
GRID_W = 64
ROPE_THETA = 10000.0
HGRN_HEAD_DIM = 128
HGRN_CHUNK = 64
HGRN_STEP_CHUNKS = 4
MLP_CHUNK = 128
MLA_NOPE = 128
MLA_ROPE = 64
MLA_V_DIM = 128
V_PAD = 16
FOURIER_GROUPS = 4
CONV_W = 3

TM = 256
NB = 2
HALO = 8
ATTN_KC = 512
ATTN_HEADS_PER_STEP = 3
V7X_VMEM_LIMIT = 56 * 1024 * 1024


def _cparams(sem, vmem=None):
    return pltpu.CompilerParams(dimension_semantics=sem, vmem_limit_bytes=vmem)


def _dot(a, b):
    return jnp.dot(a, b, preferred_element_type=F32)


def _dot_nt(a, b):
    return lax.dot_general(a, b, (((1,), (1,)), ((), ())), preferred_element_type=F32)


def _dot_tn(a, b):
    return lax.dot_general(a, b, (((0,), (0,)), ((), ())), preferred_element_type=F32)


def _silu(x):
    return x * jax.nn.sigmoid(x)


def _gelu_tanh(x):
    c = 0.7978845608028654
    return x * (0.5 + 0.5 * jnp.tanh(x * (c + (c * 0.044715) * (x * x))))


def _rms(x, n=None):
    n = x.shape[-1] if n is None else n
    return x * lax.rsqrt(jnp.sum(x * x, axis=-1, keepdims=True) * (1.0 / n) + EPS)


def _modulate(x, nw, shift, scale):
    return (_rms(x) * nw) * (1.0 + scale) + shift


def _mod_spec(d, nb):
    return pl.BlockSpec((None, nb, 6, d), lambda b, i: (jnp.minimum(i, 1), b, 0, 0))


def _rows_spec(cols, colblk=0):
    return pl.BlockSpec((NB, TM, cols), lambda b, i: (b, i, colblk))


def _const_spec(shape):
    nd = len(shape)
    return pl.BlockSpec(shape, lambda *_: (0,) * nd, pipeline_mode=pl.Buffered(1))


def _ada_kernel(c_ref, w_ref, b_ref, o_ref):
    s = _silu(c_ref[...]).astype(BF16)
    o_ref[...] = _dot(s, w_ref[...].astype(BF16)) + b_ref[...]


def _ada_tables(cvec, ada_w, ada_b):
    depth, d, n = ada_w.shape
    rows = cvec.shape[0]
    tn = 1536
    return pl.pallas_call(
        _ada_kernel,
        grid=(depth, n // tn),
        in_specs=[
            pl.BlockSpec((rows, d), lambda l, j: (0, 0)),
            pl.BlockSpec((None, d, tn), lambda l, j: (l, 0, j)),
            pl.BlockSpec((None, 1, tn), lambda l, j: (l, 0, j)),
        ],
        out_specs=pl.BlockSpec((None, rows, tn), lambda l, j: (l, 0, j)),
        out_shape=jax.ShapeDtypeStruct((depth, rows, n), F32),
        compiler_params=_cparams(("parallel", "parallel"), V7X_VMEM_LIMIT),
        name="ada_tables",
    )(cvec, ada_w, ada_b.reshape(depth, 1, n))


def _stream_specs(xs, d):
    if isinstance(xs, tuple):
        return ([pl.BlockSpec((NB, TM, d), lambda b, i: (b, 0, 0)),
                 pl.BlockSpec((NB, TM, d), lambda b, i: (b, jnp.maximum(i - 1, 0), 0))], list(xs))
    return [_rows_spec(d)], [xs]


def _stream_rows(x_refs, s):
    if len(x_refs) == 1:
        return x_refs[0][s]
    return jnp.where(pl.program_id(1) == 0, x_refs[0][s], x_refs[1][s])


def _even_in_kernel(*refs, n_x, a_width):
    x_refs = refs[:n_x]
    mod_ref, nw_ref, w_ref, oz_ref, ob_ref = refs[n_x:]
    nw = nw_ref[...]
    h = jnp.concatenate([_modulate(_stream_rows(x_refs, s), nw, mod_ref[s, 0:1, :], mod_ref[s, 1:2, :]).astype(BF16)
                         for s in range(NB)], axis=0)
    p = _dot(h, w_ref[...])
    z0, z1 = a_width * 2, a_width * 4
    for s in range(NB):
        rows = slice(s * TM, (s + 1) * TM)
        oz_ref[s] = p[rows, z0:z1]
        ob_ref[s] = jnp.concatenate([p[rows, :z0], p[rows, z1:]], axis=1).astype(ob_ref.dtype)


def _even_in(xs, tt, mod, nw, w_in, a_width):
    d, n = w_in.shape
    x_specs, x_args = _stream_specs(xs, d)
    b = x_args[0].shape[0]
    return pl.pallas_call(
        functools.partial(_even_in_kernel, n_x=len(x_args), a_width=a_width),
        grid=(b // NB, tt // TM),
        in_specs=x_specs + [
            _mod_spec(d, NB),
            _const_spec((1, d)),
            _const_spec((d, n)),
        ],
        out_specs=[_rows_spec(2 * a_width), _rows_spec(n - 2 * a_width)],
        out_shape=[jax.ShapeDtypeStruct((b, tt, 2 * a_width), F32),
                   jax.ShapeDtypeStruct((b, tt, n - 2 * a_width), BF16)],
        compiler_params=_cparams(("parallel", "parallel"), V7X_VMEM_LIMIT),
        name="even_in",
    )(*x_args, mod, nw, w_in)


def _hgrn_consts(reverse):
    c = HGRN_CHUNK
    idx = np.arange(c)
    if not reverse:
        tri = (idx[None, :] <= idx[:, None]).astype(np.float32)
    else:
        tri = (idx[None, :] >= idx[:, None]).astype(np.float32)
    masks = []
    for w in (32, 16, 8):
        grp = idx // (2 * w)
        qrow = (idx % (2 * w) >= w) if not reverse else (idx % (2 * w) < w)
        masks.append((grp[:, None] == grp[None, :]) & qrow[:, None] & (~qrow)[None, :])
    blk = idx // 8
    causal = (idx[None, :] <= idx[:, None]) if not reverse else (idx[None, :] >= idx[:, None])
    masks.append((blk[:, None] == blk[None, :]) & causal)
    return np.kron(np.eye(HGRN_STEP_CHUNKS), tri).astype(np.float32), np.stack(masks).astype(np.float32)


def _hgrn_prepare(pq_ref, pv_ref, pz_ref, lb, tri_ref, width):
    f = jnp.maximum(lb + (1.0 - lb) * jax.nn.sigmoid(pz_ref[...]), F_MIN)
    g = jnp.log(f) * LOG2_E
    g1 = g.astype(BF16)
    r1 = g - g1.astype(F32)
    g2 = r1.astype(BF16)
    g3 = (r1 - g2.astype(F32)).astype(BF16)
    b3 = _dot(tri_ref[...], jnp.concatenate([g1, g2, g3], axis=1))
    b_all = b3[:, :width] + b3[:, width:2 * width] + b3[:, 2 * width:]
    return _silu(pq_ref[...].astype(F32)).astype(BF16), (1.0 - f).astype(BF16), pv_ref[...], b_all


def _hgrn_chunk(feats, ci, reverse, masks_ref, st, heads):
    c, hd = HGRN_CHUNK, HGRN_HEAD_DIM
    width = heads * hd
    q, kk, v, b_all = feats
    rs = slice(ci * c, (ci + 1) * c)
    last = c - 1 if not reverse else 0
    parts = []
    for h in range(heads):
        sl = slice(h * hd, (h + 1) * hd)
        qh, kh, b = q[rs, sl], kk[rs, sl], b_all[rs, sl]

        def ref_rows(rows, n):
            return jnp.concatenate([jnp.broadcast_to(b[r:r + 1, :], (n, hd)) for r in rows], axis=0)

        a = None
        for lv, w in enumerate((32, 16, 8)):
            mids = [gi * 2 * w + (w - 1 if not reverse else w) for gi in range(c // (2 * w))]
            ew = jnp.exp2(-jnp.abs(b - ref_rows(mids, 2 * w))).astype(BF16)
            t = _dot_nt(qh * ew, kh * ew) * masks_ref[lv]
            a = t if a is None else a + t
        anchor = ref_rows([8 * m + (3 if not reverse else 4) for m in range(c // 8)], 8)
        d8 = _dot_nt(qh * jnp.exp2(b - anchor).astype(BF16), kh * jnp.exp2(anchor - b).astype(BF16))
        a = a + jnp.where(masks_ref[3] > 0.5, d8, 0.0)
        vb = v[rs, sl]
        e_in = jnp.exp2(b)
        qhat = qh * e_in.astype(BF16)
        khat = kh * jnp.exp2(ref_rows([last], c) - b).astype(BF16)
        parts.append((a.astype(BF16), vb, qhat, _dot_tn(vb, khat), e_in[last:last + 1, :]))
    heads_out = []
    for h, (a, vb, qhat, upd, e_last) in enumerate(parts):
        heads_out.append(_dot(a, vb) + _dot_nt(qhat, st[h].astype(BF16)))
        st[h] = st[h] * e_last + upd
    return jnp.concatenate(heads_out, axis=1)


def _hgrn_kernel(fq_ref, fv_ref, fz_ref, bq_ref, bv_ref, bz_ref, lb_ref, ftri_ref, btri_ref, fmask_ref, bmask_ref,
                 of_ref, ob_ref, fst_ref, bst_ref, *, heads):
    nck = HGRN_STEP_CHUNKS
    width = heads * HGRN_HEAD_DIM

    @pl.when(pl.program_id(1) == 0)
    def _():
        fst_ref[...] = jnp.zeros_like(fst_ref)
        bst_ref[...] = jnp.zeros_like(bst_ref)

    ffeats = _hgrn_prepare(fq_ref, fv_ref, fz_ref, lb_ref[0:1, :], ftri_ref, width)
    bfeats = _hgrn_prepare(bq_ref, bv_ref, bz_ref, lb_ref[1:2, :], btri_ref, width)
    fst = [fst_ref[h] for h in range(heads)]
    bst = [bst_ref[h] for h in range(heads)]
    fouts, bouts = [None] * nck, [None] * nck
    for k in range(nck):
        fouts[k] = _hgrn_chunk(ffeats, k, False, fmask_ref, fst, heads)
        bouts[nck - 1 - k] = _hgrn_chunk(bfeats, nck - 1 - k, True, bmask_ref, bst, heads)
    for h in range(heads):
        fst_ref[h] = fst[h]
        bst_ref[h] = bst[h]
    of_ref[...] = jnp.concatenate(fouts, axis=0).astype(of_ref.dtype)
    ob_ref[...] = jnp.concatenate(bouts, axis=0).astype(ob_ref.dtype)


def _hgrn_scan(pz, pb, lbs, ctx_len, width):
    b, tt, _ = pz.shape
    rows = HGRN_CHUNK * HGRN_STEP_CHUNKS
    heads = width // HGRN_HEAD_DIM
    nb, nbc = tt // rows, ctx_len // rows
    ftri, fmask = _hgrn_consts(False)
    btri, bmask = _hgrn_consts(True)

    def bblk(j):
        return jnp.where(j < nbc, nbc - 1 - j, nb - 1 - (j - nbc))

    def fcol(k):
        return pl.BlockSpec((None, rows, width), lambda b, j: (b, j, k))

    def bcol(k):
        return pl.BlockSpec((None, rows, width), lambda b, j: (b, bblk(j), k))

    consts = [lbs, jnp.asarray(ftri, BF16), jnp.asarray(btri, BF16), jnp.asarray(fmask, F32), jnp.asarray(bmask, F32)]
    state = pltpu.VMEM((heads, HGRN_HEAD_DIM, HGRN_HEAD_DIM), F32)
    return pl.pallas_call(
        functools.partial(_hgrn_kernel, heads=heads),
        grid=(b, nb),
        in_specs=[fcol(0), fcol(1), fcol(0), bcol(0), bcol(1), bcol(1)] + [_const_spec(a.shape) for a in consts],
        out_specs=[fcol(0), bcol(0)],
        out_shape=[jax.ShapeDtypeStruct((b, tt, width), BF16)] * 2,
        scratch_shapes=[state, state],
        compiler_params=_cparams(("parallel", "arbitrary"), V7X_VMEM_LIMIT),
        name="hgrn_scan",
    )(pb, pb, pz, pb, pb, pz, *consts)


def _even_out_kernel(*refs, groups, n_x):
    of_ref, ob_ref, pg_ref, pu_ref, pv_ref = refs[:5]
    x_refs = refs[5:5 + n_x]
    mod_ref, onw_ref, vnw_ref, ws_ref, bsb_ref, wa_ref, wb_ref, o_ref = refs[5 + n_x:]
    gd = pu_ref.shape[2] // groups
    hd = HGRN_HEAD_DIM
    onw = onw_ref[...]
    ms, yas = [], []
    for s in range(NB):
        o = of_ref[s].astype(F32) + ob_ref[s].astype(F32)
        yn = jnp.concatenate([_rms(o[:, h * hd:(h + 1) * hd]) * onw for h in range(o.shape[1] // hd)], axis=1)
        yas.append((yn * _silu(pg_ref[s].astype(F32))).astype(BF16))
    y = _dot(jnp.concatenate(yas, axis=0), wa_ref[...])
    for s in range(NB):
        for n in range(TM // MLP_CHUNK):
            rows = slice(n * MLP_CHUNK, (n + 1) * MLP_CHUNK)
            u = _gelu_tanh(pu_ref[s, rows, :].astype(F32))
            v = _gelu_tanh(pv_ref[s, rows, :].astype(F32))
            parts = []
            for g in range(groups):
                cs = slice(g * gd, (g + 1) * gd)
                vg = (_rms(v[:, cs]) * vnw_ref[:, cs]).astype(BF16)
                sv = _dot(ws_ref[g], vg) + bsb_ref[:, cs]
                parts.append(u[:, cs] * sv)
            ms.append(jnp.concatenate(parts, axis=1).astype(BF16))
    y = y + _dot(jnp.concatenate(ms, axis=0), wb_ref[...])
    for s in range(NB):
        o_ref[s] = _stream_rows(x_refs, s) + mod_ref[s, 2:3, :] * y[s * TM:(s + 1) * TM]


def _even_out(o_fwd, o_bwd, onw, pb, xs, mod, vnw, ws, bsb, w_out, a_width):
    b, tt, _ = pb.shape
    d = w_out.shape[1]
    bw = d - a_width
    groups = ws.shape[0]
    ucol = 3 * a_width // bw
    x_specs, x_args = _stream_specs(xs, d)
    return pl.pallas_call(
        functools.partial(_even_out_kernel, groups=groups, n_x=len(x_args)),
        grid=(b // NB, tt // TM),
        in_specs=[
            _rows_spec(a_width),
            _rows_spec(a_width),
            _rows_spec(a_width, 2),
            _rows_spec(bw, ucol),
            _rows_spec(bw, ucol + 1)] + x_specs + [
            _mod_spec(d, NB),
            _const_spec((1, HGRN_HEAD_DIM)),
            _const_spec((1, bw)),
            _const_spec(ws.shape),
            _const_spec(bsb.shape),
            _const_spec((a_width, d)),
            _const_spec((bw, d)),
        ],
        out_specs=_rows_spec(d),
        out_shape=jax.ShapeDtypeStruct((b, tt, d), F32),
        compiler_params=_cparams(("parallel", "parallel"), V7X_VMEM_LIMIT),
        name="even_out",
    )(o_fwd, o_bwd, pb, pb, pb, *x_args, mod, onw.reshape(1, HGRN_HEAD_DIM), vnw, ws, bsb,
      w_out[:a_width], w_out[a_width:])


def _odd_in_kernel(x_ref, mod_ref, nw_ref, win_ref, qaw_ref, kvaw_ref, wq_ref, qnw_ref, wkv_ref, knw_ref,
                   cq_ref, sq_ref, ck_ref, sk_ref, bc_ref,
                   xc_ctx_ref, xc_lat_ref, qt_ref, k_ref, vt_ref, *, heads, cw, ql, kvl):
    cq, sq, ck, sk = cq_ref[...], sq_ref[...], ck_ref[...], sk_ref[...]
    rope = MLA_ROPE
    o = cw + ql + kvl
    hw = heads * 128
    ones_pad = (lax.broadcasted_iota(jnp.int32, (V_PAD, TM), 0) == 0).astype(F32)
    is_ctx = pl.program_id(1) == 0

    def project(s):
        h = _modulate(x_ref[s], nw_ref[...], mod_ref[s, 0:1, :], mod_ref[s, 1:2, :])
        return _dot(h.astype(BF16), win_ref[...])

    def latents(s, p):
        xc = _dot(p[:, :cw].astype(BF16), bc_ref[...])
        xc_lat_ref[s] = xc.astype(xc_lat_ref.dtype)

        @pl.when(is_ctx)
        def _():
            xc_ctx_ref[s] = xc.astype(xc_ctx_ref.dtype)
        q_lat = (_rms(p[:, cw:cw + ql]) * qaw_ref[...]).astype(BF16)
        kv_lat = (_rms(p[:, cw + ql:cw + ql + kvl]) * kvaw_ref[...]).astype(BF16)
        kp, kps = p[:, o:o + 128], p[:, o + 128:o + 256]
        kpr = lax.rsqrt(jnp.sum(kp * kp, axis=-1, keepdims=True) * (1.0 / rope) + EPS)
        k_pe = (kp * ck + kps * sk) * kpr
        qf = _dot(q_lat, wq_ref[...])
        kvf = _dot(kv_lat, wkv_ref[...])
        return qf, kvf, k_pe

    def heads_out(s, qf, kvf, k_pe):
        for hh in range(heads):
            cs = slice(hh * 128, (hh + 1) * 128)
            qn = _rms(qf[:, cs]) * qnw_ref[...]
            qp = qf[:, hw + hh * 128:hw + (hh + 1) * 128]
            qps = qf[:, 2 * hw + hh * 128:2 * hw + (hh + 1) * 128]
            qpr = lax.rsqrt(jnp.sum(qp * qp, axis=-1, keepdims=True) * (1.0 / rope) + EPS)
            q_pe = (qp * cq + qps * sq) * qpr
            qt_ref[s, hh] = jnp.concatenate([qn, q_pe], axis=1).T.astype(qt_ref.dtype)
            kn = _rms(kvf[:, cs]) * knw_ref[...]
            k_ref[s, hh] = jnp.concatenate([kn, k_pe], axis=1).astype(k_ref.dtype)
            vt_ref[s, hh] = jnp.concatenate([kvf[:, hw + hh * 128:hw + (hh + 1) * 128].T, ones_pad],
                                            axis=0).astype(vt_ref.dtype)

    mid = latents(0, project(0))
    for s in range(1, NB):
        p_next = project(s)
        heads_out(s - 1, *mid)
        mid = latents(s, p_next)
    heads_out(NB - 1, *mid)


def _odd_in(x, mod, nw, wts, heads, cw, ql, kvl):
    b, tt, d = x.shape
    consts = [wts[k] for k in ("w_in", "qa_w", "kva_w", "w_q", "qn_w", "w_kv", "kn_w")]
    tables = [wts[k] for k in ("cq", "sq", "ck", "sk")]
    row128 = pl.BlockSpec((TM, 128), lambda b, i: (i, 0))
    in_specs = ([_rows_spec(d), _mod_spec(d, NB), _const_spec((1, d))]
                + [_const_spec(a.shape) for a in consts] + [row128] * 4 + [_const_spec(wts["bc"].shape)])
    return pl.pallas_call(
        functools.partial(_odd_in_kernel, heads=heads, cw=cw, ql=ql, kvl=kvl),
        grid=(b // NB, tt // TM),
        in_specs=in_specs,
        out_specs=[
            pl.BlockSpec((NB, TM, 2 * cw), lambda b, i: (b, 0, 0)),
            pl.BlockSpec((NB, TM, 2 * cw), lambda b, i: (b, jnp.maximum(i - 1, 0), 0)),
            pl.BlockSpec((NB, heads, 256, TM), lambda b, i: (b, 0, 0, i)),
            pl.BlockSpec((NB, heads, TM, 256), lambda b, i: (b, 0, i, 0)),
            pl.BlockSpec((NB, heads, MLA_V_DIM + V_PAD, TM), lambda b, i: (b, 0, 0, i)),
        ],
        out_shape=[
            jax.ShapeDtypeStruct((b, TM, 2 * cw), BF16),
            jax.ShapeDtypeStruct((b, tt - TM, 2 * cw), F32),
            jax.ShapeDtypeStruct((b, heads, 256, tt), BF16),
            jax.ShapeDtypeStruct((b, heads, tt, 256), BF16),
            jax.ShapeDtypeStruct((b, heads, MLA_V_DIM + V_PAD, tt), BF16),
        ],
        compiler_params=_cparams(("parallel", "arbitrary"), V7X_VMEM_LIMIT),
        name="odd_in",
    )(x, mod, nw, *consts, *tables, wts["bc"])


def _attn_kernel(qt_ref, k_ref, vt_ref, o_ref, sa_ref, sb_ref, ma_ref, mb_ref, *, ctx_len, n_tiles):
    i = pl.program_id(2)
    n_all = k_ref.shape[1]
    nh = k_ref.shape[0]
    bufs = ((sa_ref, ma_ref), (sb_ref, mb_ref))

    def chunks(n_keys):
        return [(st, min(ATTN_KC, n_keys - st)) for st in range(0, n_keys, ATTN_KC)]

    def run(step_parity, keys1, keys2):
        s_w, m_w = bufs[step_parity]
        s_r, m_r = bufs[1 - step_parity]
        c1 = chunks(keys1) if keys1 else []
        c2 = chunks(keys2) if keys2 else []
        if keys1:
            qts = [qt_ref[hh] for hh in range(nh)]
        if keys2:
            m_prev = [m_r[hh] for hh in range(nh)]
        m, acc = [None] * nh, [None] * nh
        for idx in range(max(len(c1), len(c2))):
            for hh in range(nh):
                if idx < len(c1):
                    st, sz = c1[idx]
                    s = _dot(k_ref[hh, st:st + sz, :], qts[hh])
                    s_w[hh, st:st + sz, :] = s
                    cm = jnp.max(s, axis=0, keepdims=True)
                    m[hh] = cm if m[hh] is None else jnp.maximum(m[hh], cm)
                if idx < len(c2):
                    st, sz = c2[idx]
                    p = jnp.exp2(s_r[hh, st:st + sz, :] - m_prev[hh]).astype(BF16)
                    ca = _dot(vt_ref[hh, :, st:st + sz], p)
                    acc[hh] = ca if acc[hh] is None else acc[hh] + ca
        dv = o_ref.shape[1] // nh
        for hh in range(nh):
            if keys1:
                m_w[hh] = m[hh]
            if keys2:
                o_ref[:, hh * dv:(hh + 1) * dv] = (acc[hh][:dv] * (1.0 / acc[hh][dv:dv + 1])).T.astype(o_ref.dtype)

    @pl.when(i == 0)
    def _():
        run(0, ctx_len, None)

    @pl.when(i == 1)
    def _():
        run(1, n_all, ctx_len)

    for parity in (0, 1):
        @pl.when(jnp.logical_and(jnp.logical_and(i >= 2, i < n_tiles), i % 2 == parity))
        def _():
            run(parity, n_all, n_all)

    @pl.when(i == n_tiles)
    def _():
        run(n_tiles % 2, None, n_all)


def _attention(qt, k, vt, ctx_len):
    b, heads, tt, dq = k.shape
    dvp = vt.shape[2]
    dv = dvp - V_PAD
    n_tiles = tt // TM
    nh = ATTN_HEADS_PER_STEP
    assert heads % nh == 0
    return pl.pallas_call(
        functools.partial(_attn_kernel, ctx_len=ctx_len, n_tiles=n_tiles),
        grid=(b, heads // nh, n_tiles + 1),
        in_specs=[
            pl.BlockSpec((None, nh, dq, TM), lambda b, h, i: (b, h, 0, jnp.minimum(i, n_tiles - 1))),
            pl.BlockSpec((None, nh, tt, dq), lambda b, h, i: (b, h, 0, 0)),
            pl.BlockSpec((None, nh, dvp, tt), lambda b, h, i: (b, h, 0, 0)),
        ],
        out_specs=pl.BlockSpec((None, TM, nh * dv), lambda b, h, i: (b, jnp.maximum(i - 1, 0), h)),
        out_shape=jax.ShapeDtypeStruct((b, tt, heads * dv), BF16),
        scratch_shapes=[pltpu.VMEM((nh, tt, TM), F32), pltpu.VMEM((nh, tt, TM), F32),
                        pltpu.VMEM((nh, 1, TM), F32), pltpu.VMEM((nh, 1, TM), F32)],
        compiler_params=_cparams(("parallel", "parallel", "arbitrary"), V7X_VMEM_LIMIT),
        name="attention",
    )(qt, k, vt)


def _dft_kernel(c_ref, s_ref, x_ref, o_ref):
    half = x_ref.shape[1] // 2
    y = _dot(c_ref[...], x_ref[:, :half]) + _dot(s_ref[...], x_ref[:, half:])
    o_ref[...] = y.astype(o_ref.dtype)


def _dft_dense(xc, t_cos, t_sin):
    b, t, w2 = xc.shape
    return pl.pallas_call(
        _dft_kernel,
        grid=(b,),
        in_specs=[_const_spec((t, t)), _const_spec((t, t)), pl.BlockSpec((None, t, w2), lambda b: (b, 0, 0))],
        out_specs=pl.BlockSpec((None, t, w2 // 2), lambda b: (b, 0, 0)),
        out_shape=jax.ShapeDtypeStruct((b, t, w2 // 2), BF16),
        compiler_params=_cparams(("parallel",), V7X_VMEM_LIMIT),
        name="dft_dense",
    )(t_cos, t_sin, xc)


def _dft_split(t):
    n2 = 1 << ((t.bit_length()) // 2)
    return t // n2, n2


def _swap_major(x):
    return jnp.transpose(x, (1, 0, 2))


def _dft_two_stage_kernel(xr_ref, xi_ref, g_ref, cs2_ref, o_ref, zr_ref, zi_ref, y_ref, *, n1, n2):
    lw = xr_ref.shape[1]
    xr = _swap_major(xr_ref[...].reshape(n1, n2, lw))
    xi = _swap_major(xi_ref[...].reshape(n1, n2, lw))
    for t2 in range(n2):
        p = _dot(g_ref[t2], jnp.concatenate([xr[t2], xi[t2]], axis=1).astype(BF16))
        zr_ref[t2] = p[:n1, :lw] - p[n1:, lw:]
        zi_ref[t2] = p[:n1, lw:] + p[n1:, :lw]
    zr = _swap_major(zr_ref[...])
    zi = _swap_major(zi_ref[...])
    for u1 in range(n1):
        y_ref[u1] = _dot(cs2_ref[...], jnp.concatenate([zr[u1], zi[u1]], axis=0).astype(BF16))
    o_ref[...] = _swap_major(y_ref[...]).reshape(n1 * n2, lw).astype(o_ref.dtype)


def _dft_two_stage(xc, tables):
    g, cs2 = tables
    b, t, w2 = xc.shape
    cw = w2 // 2
    lw = 128
    n1, n2 = _dft_split(t)
    return pl.pallas_call(
        functools.partial(_dft_two_stage_kernel, n1=n1, n2=n2),
        grid=(b, cw // lw),
        in_specs=[pl.BlockSpec((None, t, lw), lambda b, j: (b, 0, j)),
                  pl.BlockSpec((None, t, lw), lambda b, j: (b, 0, cw // lw + j)),
                  _const_spec(g.shape), _const_spec(cs2.shape)],
        out_specs=pl.BlockSpec((None, t, lw), lambda b, j: (b, 0, j)),
        out_shape=jax.ShapeDtypeStruct((b, t, cw), BF16),
        scratch_shapes=[pltpu.VMEM((n2, n1, lw), F32), pltpu.VMEM((n2, n1, lw), F32), pltpu.VMEM((n1, n2, lw), F32)],
        compiler_params=_cparams(("parallel", "parallel"), V7X_VMEM_LIMIT),
        name="dft_two_stage",
    )(xc, xc, g, cs2)


def _odd_out_kernel(fc_ref, fl_ref, at_ref, x_ref, mod_ref, wf_ref, wa_ref, o_ref):
    is_ctx = pl.program_id(1) == 0
    fm = jnp.concatenate([jnp.where(is_ctx, fc_ref[s], fl_ref[s]) for s in range(NB)], axis=0)
    at = jnp.concatenate([at_ref[s] for s in range(NB)], axis=0)
    y = _dot(fm, wf_ref[...]) + _dot(at, wa_ref[...])
    for s in range(NB):
        o_ref[s] = x_ref[s] + mod_ref[s, 2:3, :] * y[s * TM:(s + 1) * TM]


def _odd_out(fm_ctx, fm_lat, attn, x, mod, w_out):
    b, tt, d = x.shape
    cw = fm_ctx.shape[-1]
    aw = attn.shape[-1]
    return pl.pallas_call(
        _odd_out_kernel,
        grid=(b // NB, tt // TM),
        in_specs=[
            pl.BlockSpec((NB, TM, cw), lambda b, i: (b, 0, 0)),
            pl.BlockSpec((NB, TM, cw), lambda b, i: (b, jnp.maximum(i - 1, 0), 0)),
            _rows_spec(aw),
            _rows_spec(d),
            _mod_spec(d, NB),
            _const_spec((cw, d)),
            _const_spec((aw, d)),
        ],
        out_specs=_rows_spec(d),
        out_shape=jax.ShapeDtypeStruct((b, tt, d), F32),
        compiler_params=_cparams(("parallel", "parallel"), V7X_VMEM_LIMIT),
        name="odd_out",
    )(fm_ctx, fm_lat, attn, x, mod, w_out[:cw], w_out[cw:])


def _ffn_kernel(xp_ref, x_ref, xn_ref, mod_ref, nw_ref, wg_ref, wv_ref, cw_ref, cb_ref, wd_ref, o_ref,
                *, first_tile, n_tiles):
    i = pl.program_id(1) + first_tile
    ext = TM + 2 * HALO
    row = lax.broadcasted_iota(jnp.int32, (ext, 1), 0)
    keep = jnp.logical_and(jnp.logical_or(row >= HALO, i > 1),
                           jnp.logical_or(row < HALO + TM, jnp.logical_and(i > 0, i < n_tiles - 1)))
    nw = nw_ref[...]
    hes = []
    for s in range(NB):
        xe = jnp.concatenate([xp_ref[s], x_ref[s], xn_ref[s]], axis=0)
        hes.append(jnp.where(keep, _modulate(xe, nw, mod_ref[s, 3:4, :], mod_ref[s, 4:5, :]), 0.0))
    ge = _dot(jnp.concatenate(hes, axis=0).astype(BF16), wg_ref[...])
    val = _dot(jnp.concatenate([h[HALO:HALO + TM] for h in hes], axis=0).astype(BF16), wv_ref[...])
    cw0, cw1, cw2 = cw_ref[0:1, :], cw_ref[1:2, :], cw_ref[2:3, :]
    gc = jnp.concatenate(
        [ge[s * ext + HALO - 1:s * ext + HALO - 1 + TM] * cw0 + ge[s * ext + HALO:s * ext + HALO + TM] * cw1
         + ge[s * ext + HALO + 1:s * ext + HALO + 1 + TM] * cw2 for s in range(NB)], axis=0) + cb_ref[...]
    y = _dot((_silu(gc) * val).astype(BF16), wd_ref[...])
    for s in range(NB):
        o_ref[s] = x_ref[s] + mod_ref[s, 5:6, :] * y[s * TM:(s + 1) * TM]


def _conv_ffn(x, mod, nw, w_up, conv_w, conv_b, wd, skip_ctx):
    b, tt, d = x.shape
    ff = wd.shape[0]
    n_tiles = tt // TM
    first = 1 if skip_ctx else 0
    r = TM // HALO
    nblk = tt // HALO
    return pl.pallas_call(
        functools.partial(_ffn_kernel, first_tile=first, n_tiles=n_tiles),
        grid=(b // NB, n_tiles - first),
        in_specs=[
            pl.BlockSpec((NB, HALO, d), lambda b, i: (b, jnp.maximum((i + first) * r - 1, 0), 0)),
            pl.BlockSpec((NB, TM, d), lambda b, i: (b, i + first, 0)),
            pl.BlockSpec((NB, HALO, d), lambda b, i: (b, jnp.minimum((i + first + 1) * r, nblk - 1), 0)),
            pl.BlockSpec((None, NB, 6, d), lambda b, i: (jnp.minimum(i + first, 1), b, 0, 0)),
            _const_spec((1, d)),
            pl.BlockSpec((d, ff), lambda b, i: (0, 0), pipeline_mode=pl.Buffered(1)),
            pl.BlockSpec((d, ff), lambda b, i: (0, 1), pipeline_mode=pl.Buffered(1)),
            _const_spec((CONV_W, ff)),
            _const_spec((1, ff)),
            _const_spec((ff, d)),
        ],
        out_specs=_rows_spec(d),
        out_shape=jax.ShapeDtypeStruct((b, tt - first * TM, d), F32),
        compiler_params=_cparams(("parallel", "parallel"), V7X_VMEM_LIMIT),
        name="conv_ffn",
    )(x, x, x, mod, nw, w_up, w_up, conv_w, conv_b, wd)


def _rope_swap_perm():
    q = MLA_ROPE // 4
    return np.concatenate([np.arange(q, 2 * q), np.arange(0, q), np.arange(3 * q, 4 * q), np.arange(2 * q, 3 * q)])


def _pad128(v):
    return jnp.pad(v, (0, 128 - v.shape[0])).reshape(1, 128)


def _odd_weights(w_in, qa_w, w_qb, kva_w, w_kvb, qn_w, kn_w, heads, cw, ql, kvl, cos_t, sin_t):
    d = w_in.shape[0]
    perm = _rope_swap_perm()
    nope, rope, qk = MLA_NOPE, MLA_ROPE, MLA_NOPE + MLA_ROPE
    o = cw + ql + kvl
    z = jnp.zeros((d, 128 - rope), w_in.dtype)
    kpe = w_in[:, o:o + rope]
    w_in_ext = jnp.concatenate([w_in[:, :o], kpe, z, kpe[:, perm], z], axis=1).astype(BF16)
    wq = w_qb.reshape(ql, heads, qk)
    zq = jnp.zeros((ql, heads, 128 - rope), w_qb.dtype)
    wq_rope = wq[:, :, nope:]
    w_q = jnp.concatenate([
        wq[:, :, :nope].reshape(ql, heads * 128),
        jnp.concatenate([wq_rope, zq], axis=2).reshape(ql, heads * 128),
        jnp.concatenate([wq_rope[:, :, perm], zq], axis=2).reshape(ql, heads * 128),
    ], axis=1).astype(BF16)
    wkv = w_kvb.reshape(kvl, heads, nope + MLA_V_DIM)
    w_kv = jnp.concatenate([wkv[:, :, :nope].reshape(kvl, heads * nope),
                            wkv[:, :, nope:].reshape(kvl, heads * MLA_V_DIM)], axis=1).astype(BF16)
    gd = cw // FOURIER_GROUPS
    jk = (np.arange(gd)[:, None] * np.arange(gd)[None, :]) % gd
    ang = 2.0 * np.pi * jk / gd
    eye = np.eye(FOURIER_GROUPS)
    bc = np.concatenate([np.kron(eye, np.cos(ang)), np.kron(eye, -np.sin(ang))], axis=1) / np.sqrt(gd)
    scale = float(qk) ** -0.5 * LOG2_E
    return {
        "w_in": w_in_ext, "qa_w": qa_w.reshape(1, ql), "kva_w": kva_w.reshape(1, kvl),
        "w_q": w_q, "qn_w": (qn_w[:nope] * scale).reshape(1, nope),
        "w_kv": w_kv, "kn_w": kn_w[:nope].reshape(1, nope),
        "cq": cos_t * (_pad128(qn_w[nope:]) * scale), "sq": sin_t * (_pad128(qn_w[nope:][perm]) * scale),
        "ck": cos_t * _pad128(kn_w[nope:]), "sk": sin_t * _pad128(kn_w[nope:][perm]),
        "bc": jnp.asarray(bc, F32).astype(BF16),
    }


def _rope_tables(t_lat, ctx_len):
    rows = t_lat // GRID_W
    row = jnp.repeat(jnp.arange(rows), GRID_W)
    col = jnp.tile(jnp.arange(GRID_W), rows)
    r_axis = MLA_ROPE // 2
    inv_freq = ROPE_THETA ** (-jnp.arange(0, r_axis, 2, dtype=F32) / r_axis)
    ang = jnp.stack([row, col], axis=-1).astype(F32)[:, :, None] * inv_freq
    cos, sin = jnp.cos(ang), jnp.sin(ang)
    cos64 = jnp.concatenate([cos[:, 0], cos[:, 0], cos[:, 1], cos[:, 1]], axis=-1)
    sin64 = jnp.concatenate([-sin[:, 0], sin[:, 0], -sin[:, 1], sin[:, 1]], axis=-1)
    cos64 = jnp.concatenate([jnp.ones((ctx_len, MLA_ROPE), F32), cos64], axis=0)
    sin64 = jnp.concatenate([jnp.zeros((ctx_len, MLA_ROPE), F32), sin64], axis=0)
    pad = ((0, 0), (0, 128 - MLA_ROPE))
    return jnp.pad(cos64, pad), jnp.pad(sin64, pad)


def _cos_sin(phase, period, scale):
    ang = (phase % period).astype(F32) * (2.0 * np.pi / period)
    return jnp.cos(ang) * scale, jnp.sin(ang) * scale


def _dft_dense_tables(t):
    idx = jnp.arange(t, dtype=jnp.int32)
    c, s = _cos_sin(idx[:, None] * idx[None, :], t, 1.0 / np.sqrt(t))
    return c.astype(BF16), s.astype(BF16)


def _dft_two_stage_tables(t):
    n1, n2 = _dft_split(t)
    u1 = jnp.arange(n1, dtype=jnp.int32)
    pos = jnp.arange(t, dtype=jnp.int32).reshape(n1, n2)
    c, s = _cos_sin(pos.T[:, None, :] * u1[None, :, None], t, 1.0 / np.sqrt(n1))
    g = jnp.concatenate([c, -s], axis=1).astype(BF16)
    i2 = jnp.arange(n2, dtype=jnp.int32)
    c2, s2 = _cos_sin(i2[:, None] * i2[None, :], n2, 1.0 / np.sqrt(n2))
    return g, jnp.concatenate([c2, s2], axis=1).astype(BF16)


def kernel(x, c, ctx, c_ctx, ada_w, ada_b, norm_mix_w, norm_ffn_w, ev_w_in, ev_lb_logits, ev_onorm_w, ev_vnorm_w, ev_ws, ev_bs, ev_w_out, od_w_in, od_qa_norm_w, od_w_qb, od_kva_norm_w, od_w_kvb, od_q_norm_w, od_k_norm_w, od_w_out, ffn_w_up, ffn_conv_w, ffn_conv_b, ffn_w_down):
    bsz, t_lat, d = x.shape
    ctx_len = ctx.shape[1]
    depth = ada_w.shape[0]
    assert ctx_len == TM and t_lat % ATTN_KC == 0 and t_lat % GRID_W == 0 and bsz % NB == 0
    a_width = ev_lb_logits.shape[-1]
    ql, kvl = od_qa_norm_w.shape[-1], od_kva_norm_w.shape[-1]
    cw = od_w_in.shape[-1] - ql - kvl - MLA_ROPE
    heads = od_w_qb.shape[-1] // (MLA_NOPE + MLA_ROPE)
    d_ff = ffn_w_down.shape[1]

    pad_rows = (-(bsz + 1)) % 8
    cvec = jnp.concatenate([c, c_ctx[None, :], jnp.zeros((pad_rows, d), F32)], axis=0)
    mods = _ada_tables(cvec, ada_w, ada_b)
    mod_lat = mods[:, :bsz].reshape(depth, 1, bsz, 6, d)
    mod_ctx = jnp.broadcast_to(mods[:, bsz].reshape(depth, 1, 1, 6, d), (depth, 1, bsz, 6, d))
    mods = jnp.concatenate([mod_ctx, mod_lat], axis=1)

    lb_p = jax.nn.softmax(ev_lb_logits.astype(F32), axis=0)
    lbs = jnp.cumsum(lb_p, axis=0) - lb_p[0]
    cos_t, sin_t = _rope_tables(t_lat, ctx_len)
    dft_lat = dft_ctx = None

    xs = (ctx, x)
    for l in range(depth):
        last = l == depth - 1
        mod = mods[l]
        nmw = norm_mix_w[l].reshape(1, d)
        if l % 2 == 0:
            e = l // 2
            pz, pb = _even_in(xs, ctx_len + t_lat, mod, nmw, ev_w_in[e].astype(BF16), a_width)
            o_fwd, o_bwd = _hgrn_scan(pz, pb, lbs[e], ctx_len, a_width)
            bw = d - a_width
            gd = bw // ev_ws.shape[1]
            bsb = jnp.repeat(ev_bs[e].T, gd, axis=1)
            xs = _even_out(o_fwd, o_bwd, ev_onorm_w[e], pb, xs, mod, ev_vnorm_w[e].reshape(1, bw),
                           ev_ws[e].astype(BF16), bsb, ev_w_out[e].astype(BF16), a_width)
        else:
            o = l // 2
            if dft_lat is None:
                dft_lat, dft_ctx = _dft_two_stage_tables(t_lat), _dft_dense_tables(ctx_len)
            wts = _odd_weights(od_w_in[o], od_qa_norm_w[o], od_w_qb[o], od_kva_norm_w[o], od_w_kvb[o],
                               od_q_norm_w[o], od_k_norm_w[o], heads, cw, ql, kvl, cos_t, sin_t)
            xc_ctx, xc_lat, qt, k, vt = _odd_in(xs, mod, nmw, wts, heads, cw, ql, kvl)
            attn = _attention(qt, k, vt, ctx_len)
            fm_ctx = _dft_dense(xc_ctx, *dft_ctx)
            fm_lat = _dft_two_stage(xc_lat, dft_lat)
            xs = _odd_out(fm_ctx, fm_lat, attn, xs, mod, od_w_out[o].astype(BF16))
        xs = _conv_ffn(xs, mod, norm_ffn_w[l].reshape(1, d), ffn_w_up[l].astype(BF16), ffn_conv_w[l],
                       ffn_conv_b[l].reshape(1, d_ff), ffn_w_down[l].astype(BF16), skip_ctx=last)
    return xs
```

```python
import functools

import numpy as np
import jax
import jax.numpy as jnp
from jax import lax
from jax.experimental import pallas as pl
from jax.experimental.pallas import tpu as pltpu

F32 = jnp.float32
BF16 = jnp.bfloat16

EPS = 1e-6
F_MIN = 1e-6
LOG2_E = float(np.log2(np.e))
GRID_W = 64
ROPE_THETA = 10000.0
HGRN_HEAD_DIM = 128
HGRN_CHUNK = 64
HGRN_STEP_CHUNKS = 4
MLP_CHUNK = 128
MLA_NOPE = 128
MLA_ROPE = 64
MLA_V_DIM = 128
V_PAD = 16
FOURIER_GROUPS = 4
CONV_W = 3

TM = 256
NB = 2
HALO = 8
ATTN_KC = 512
ATTN_HEADS_PER_STEP = 3
V7X_VMEM_LIMIT = 56 * 1024 * 1024


def _cparams(sem, vmem=None):
    return pltpu.CompilerParams(dimension_semantics=sem, vmem_limit_bytes=vmem)


def _dot(a, b):
    return jnp.dot(a, b, preferred_element_type=F32)


def _dot_nt(a, b):
    return lax.dot_general(a, b, (((1,), (1,)), ((), ())), preferred_element_type=F32)


def _dot_tn(a, b):
    return lax.dot_general(a, b, (((0,), (0,)), ((), ())), preferred_element_type=F32)


def _silu(x):
    return x * jax.nn.sigmoid(x)


def _gelu_tanh(x):
    c = 0.7978845608028654
    return x * (0.5 + 0.5 * jnp.tanh(x * (c + (c * 0.044715) * (x * x))))


def _rms(x, n=None):
    n = x.shape[-1] if n is None else n
    return x * lax.rsqrt(jnp.sum(x * x, axis=-1, keepdims=True) * (1.0 / n) + EPS)


def _modulate(x, nw, shift, scale):
    return (_rms(x) * nw) * (1.0 + scale) + shift


def _mod_spec(d, nb):
    return pl.BlockSpec((None, nb, 6, d), lambda b, i: (jnp.minimum(i, 1), b, 0, 0))


def _rows_spec(cols, colblk=0):
    return pl.BlockSpec((NB, TM, cols), lambda b, i: (b, i, colblk))


def _const_spec(shape):
    nd = len(shape)
    return pl.BlockSpec(shape, lambda *_: (0,) * nd, pipeline_mode=pl.Buffered(1))


def _ada_kernel(c_ref, w_ref, b_ref, o_ref):
    s = _silu(c_ref[...]).astype(BF16)
    o_ref[...] = _dot(s, w_ref[...].astype(BF16)) + b_ref[...]


def _ada_tables(cvec, ada_w, ada_b):
    depth, d, n = ada_w.shape
    rows = cvec.shape[0]
    tn = 1536
    return pl.pallas_call(
        _ada_kernel,
        grid=(depth, n // tn),
        in_specs=[
            pl.BlockSpec((rows, d), lambda l, j: (0, 0)),
            pl.BlockSpec((None, d, tn), lambda l, j: (l, 0, j)),
            pl.BlockSpec((None, 1, tn), lambda l, j: (l, 0, j)),
        ],
        out_specs=pl.BlockSpec((None, rows, tn), lambda l, j: (l, 0, j)),
        out_shape=jax.ShapeDtypeStruct((depth, rows, n), F32),
        compiler_params=_cparams(("parallel", "parallel"), V7X_VMEM_LIMIT),
        name="ada_tables",
    )(cvec, ada_w, ada_b.reshape(depth, 1, n))


def _stream_specs(xs, d):
    if isinstance(xs, tuple):
        return ([pl.BlockSpec((NB, TM, d), lambda b, i: (b, 0, 0)),
                 pl.BlockSpec((NB, TM, d), lambda b, i: (b, jnp.maximum(i - 1, 0), 0))], list(xs))
    return [_rows_spec(d)], [xs]


def _stream_rows(x_refs, s):
    if len(x_refs) == 1:
        return x_refs[0][s]
    return jnp.where(pl.program_id(1) == 0, x_refs[0][s], x_refs[1][s])


def _even_in_kernel(*refs, n_x, a_width):
    x_refs = refs[:n_x]
    mod_ref, nw_ref, w_ref, oz_ref, ob_ref = refs[n_x:]
    nw = nw_ref[...]
    h = jnp.concatenate([_modulate(_stream_rows(x_refs, s), nw, mod_ref[s, 0:1, :], mod_ref[s, 1:2, :]).astype(BF16)
                         for s in range(NB)], axis=0)
    p = _dot(h, w_ref[...])
    z0, z1 = a_width * 2, a_width * 4
    for s in range(NB):
        rows = slice(s * TM, (s + 1) * TM)
        oz_ref[s] = p[rows, z0:z1]
        ob_ref[s] = jnp.concatenate([p[rows, :z0], p[rows, z1:]], axis=1).astype(ob_ref.dtype)


def _even_in(xs, tt, mod, nw, w_in, a_width):
    d, n = w_in.shape
    x_specs, x_args = _stream_specs(xs, d)
    b = x_args[0].shape[0]
    return pl.pallas_call(
        functools.partial(_even_in_kernel, n_x=len(x_args), a_width=a_width),
        grid=(b // NB, tt // TM),
        in_specs=x_specs + [
            _mod_spec(d, NB),
            _const_spec((1, d)),
            _const_spec((d, n)),
        ],
        out_specs=[_rows_spec(2 * a_width), _rows_spec(n - 2 * a_width)],
        out_shape=[jax.ShapeDtypeStruct((b, tt, 2 * a_width), F32),
                   jax.ShapeDtypeStruct((b, tt, n - 2 * a_width), BF16)],
        compiler_params=_cparams(("parallel", "parallel"), V7X_VMEM_LIMIT),
        name="even_in",
    )(*x_args, mod, nw, w_in)


def _hgrn_consts(reverse):
    c = HGRN_CHUNK
    idx = np.arange(c)
    if not reverse:
        tri = (idx[None, :] <= idx[:, None]).astype(np.float32)
    else:
        tri = (idx[None, :] >= idx[:, None]).astype(np.float32)
    masks = []
    for w in (32, 16, 8):
        grp = idx // (2 * w)
        qrow = (idx % (2 * w) >= w) if not reverse else (idx % (2 * w) < w)
        masks.append((grp[:, None] == grp[None, :]) & qrow[:, None] & (~qrow)[None, :])
    blk = idx // 8
    causal = (idx[None, :] <= idx[:, None]) if not reverse else (idx[None, :] >= idx[:, None])
    masks.append((blk[:, None] == blk[None, :]) & causal)
    return np.kron(np.eye(HGRN_STEP_CHUNKS), tri).astype(np.float32), np.stack(masks).astype(np.float32)


def _hgrn_prepare(pq_ref, pv_ref, pz_ref, lb, tri_ref, width):
    f = jnp.maximum(lb + (1.0 - lb) * jax.nn.sigmoid(pz_ref[...]), F_MIN)
    g = jnp.log(f) * LOG2_E
    g1 = g.astype(BF16)
    r1 = g - g1.astype(F32)
    g2 = r1.astype(BF16)
    b3 = _dot(tri_ref[...], jnp.concatenate([g1, g2], axis=1))
    b_all = b3[:, :width] + b3[:, width:]
    return _silu(pq_ref[...].astype(F32)).astype(BF16), (1.0 - f).astype(BF16), pv_ref[...], b_all


def _hgrn_chunk(feats, ci, reverse, masks_ref, st, heads):
    c, hd = HGRN_CHUNK, HGRN_HEAD_DIM
    width = heads * hd
    q, kk, v, b_all = feats
    rs = slice(ci * c, (ci + 1) * c)
    last = c - 1 if not reverse else 0
    parts = []
    for h in range(heads):
        sl = slice(h * hd, (h + 1) * hd)
        qh, kh, b = q[rs, sl], kk[rs, sl], b_all[rs, sl]

        def ref_rows(rows, n):
            return jnp.concatenate([jnp.broadcast_to(b[r:r + 1, :], (n, hd)) for r in rows], axis=0)

        a = None
        for lv, w in enumerate((32, 16, 8)):
            mids = [gi * 2 * w + (w - 1 if not reverse else w) for gi in range(c // (2 * w))]
            ew = jnp.exp2(-jnp.abs(b - ref_rows(mids, 2 * w))).astype(BF16)
            t = _dot_nt(qh * ew, kh * ew) * masks_ref[lv]
            a = t if a is None else a + t
        anchor = ref_rows([8 * m + (3 if not reverse else 4) for m in range(c // 8)], 8)
        d8 = _dot_nt(qh * jnp.exp2(b - anchor).astype(BF16), kh * jnp.exp2(anchor - b).astype(BF16))
        a = a + jnp.where(masks_ref[3] > 0.5, d8, 0.0)
        vb = v[rs, sl]
        e_in = jnp.exp2(b)
        qhat = qh * e_in.astype(BF16)
        khat = kh * jnp.exp2(ref_rows([last], c) - b).astype(BF16)
        parts.append((a.astype(BF16), vb, qhat, _dot_tn(vb, khat), e_in[last:last + 1, :]))
    heads_out = []
    for h, (a, vb, qhat, upd, e_last) in enumerate(parts):
        heads_out.append(_dot(a, vb) + _dot_nt(qhat, st[h].astype(BF16)))
        st[h] = st[h] * e_last + upd
    return jnp.concatenate(heads_out, axis=1)


def _hgrn_kernel(fq_ref, fv_ref, fz_ref, bq_ref, bv_ref, bz_ref, lb_ref, ftri_ref, btri_ref, fmask_ref, bmask_ref,
                 of_ref, ob_ref, fst_ref, bst_ref, *, heads):
    nck = HGRN_STEP_CHUNKS
    width = heads * HGRN_HEAD_DIM

    @pl.when(pl.program_id(1) == 0)
    def _():
        fst_ref[...] = jnp.zeros_like(fst_ref)
        bst_ref[...] = jnp.zeros_like(bst_ref)

    ffeats = _hgrn_prepare(fq_ref, fv_ref, fz_ref, lb_ref[0:1, :], ftri_ref, width)
    bfeats = _hgrn_prepare(bq_ref, bv_ref, bz_ref, lb_ref[1:2, :], btri_ref, width)
    fst = [fst_ref[h] for h in range(heads)]
    bst = [bst_ref[h] for h in range(heads)]
    fouts, bouts = [None] * nck, [None] * nck
    for k in range(nck):
        fouts[k] = _hgrn_chunk(ffeats, k, False, fmask_ref, fst, heads)
        bouts[nck - 1 - k] = _hgrn_chunk(bfeats, nck - 1 - k, True, bmask_ref, bst, heads)
    for h in range(heads):
        fst_ref[h] = fst[h]
        bst_ref[h] = bst[h]
    of_ref[...] = jnp.concatenate(fouts, axis=0).astype(of_ref.dtype)
    ob_ref[...] = jnp.concatenate(bouts, axis=0).astype(ob_ref.dtype)


def _hgrn_scan(pz, pb, lbs, ctx_len, width):
    b, tt, _ = pz.shape
    rows = HGRN_CHUNK * HGRN_STEP_CHUNKS
    heads = width // HGRN_HEAD_DIM
    nb, nbc = tt // rows, ctx_len // rows
    ftri, fmask = _hgrn_consts(False)
    btri, bmask = _hgrn_consts(True)

    def bblk(j):
        return jnp.where(j < nbc, nbc - 1 - j, nb - 1 - (j - nbc))

    def fcol(k):
        return pl.BlockSpec((None, rows, width), lambda b, j: (b, j, k))

    def bcol(k):
        return pl.BlockSpec((None, rows, width), lambda b, j: (b, bblk(j), k))

    consts = [lbs, jnp.asarray(ftri, BF16), jnp.asarray(btri, BF16), jnp.asarray(fmask, F32), jnp.asarray(bmask, F32)]
    state = pltpu.VMEM((heads, HGRN_HEAD_DIM, HGRN_HEAD_DIM), F32)
    return pl.pallas_call(
        functools.partial(_hgrn_kernel, heads=heads),
        grid=(b, nb),
        in_specs=[fcol(0), fcol(1), fcol(0), bcol(0), bcol(1), bcol(1)] + [_const_spec(a.shape) for a in consts],
        out_specs=[fcol(0), bcol(0)],
        out_shape=[jax.ShapeDtypeStruct((b, tt, width), BF16)] * 2,
        scratch_shapes=[state, state],
        compiler_params=_cparams(("parallel", "arbitrary"), V7X_VMEM_LIMIT),
        name="hgrn_scan",
    )(pb, pb, pz, pb, pb, pz, *consts)


def _even_out_kernel(*refs, groups, n_x):
    of_ref, ob_ref, pg_ref, pu_ref, pv_ref = refs[:5]
    x_refs = refs[5:5 + n_x]
    mod_ref, onw_ref, vnw_ref, ws_ref, bsb_ref, wa_ref, wb_ref, o_ref = refs[5 + n_x:]
    gd = pu_ref.shape[2] // groups
    hd = HGRN_HEAD_DIM
    onw = onw_ref[...]
    ms, yas = [], []
    for s in range(NB):
        o = of_ref[s].astype(F32) + ob_ref[s].astype(F32)
        yn = jnp.concatenate([_rms(o[:, h * hd:(h + 1) * hd]) * onw for h in range(o.shape[1] // hd)], axis=1)
        yas.append((yn * _silu(pg_ref[s].astype(F32))).astype(BF16))
    y = _dot(jnp.concatenate(yas, axis=0), wa_ref[...])
    for s in range(NB):
        for n in range(TM // MLP_CHUNK):
            rows = slice(n * MLP_CHUNK, (n + 1) * MLP_CHUNK)
            u = _gelu_tanh(pu_ref[s, rows, :].astype(F32))
            v = _gelu_tanh(pv_ref[s, rows, :].astype(F32))
            parts = []
            for g in range(groups):
                cs = slice(g * gd, (g + 1) * gd)
                vg = (_rms(v[:, cs]) * vnw_ref[:, cs]).astype(BF16)
                sv = _dot(ws_ref[g], vg) + bsb_ref[:, cs]
                parts.append(u[:, cs] * sv)
            ms.append(jnp.concatenate(parts, axis=1).astype(BF16))
    y = y + _dot(jnp.concatenate(ms, axis=0), wb_ref[...])
    for s in range(NB):
        o_ref[s] = _stream_rows(x_refs, s) + mod_ref[s, 2:3, :] * y[s * TM:(s + 1) * TM]


def _even_out(o_fwd, o_bwd, onw, pb, xs, mod, vnw, ws, bsb, w_out, a_width):
    b, tt, _ = pb.shape
    d = w_out.shape[1]
    bw = d - a_width
    groups = ws.shape[0]
    ucol = 3 * a_width // bw
    x_specs, x_args = _stream_specs(xs, d)
    return pl.pallas_call(
        functools.partial(_even_out_kernel, groups=groups, n_x=len(x_args)),
        grid=(b // NB, tt // TM),
        in_specs=[
            _rows_spec(a_width),
            _rows_spec(a_width),
            _rows_spec(a_width, 2),
            _rows_spec(bw, ucol),
            _rows_spec(bw, ucol + 1)] + x_specs + [
            _mod_spec(d, NB),
            _const_spec((1, HGRN_HEAD_DIM)),
            _const_spec((1, bw)),
            _const_spec(ws.shape),
            _const_spec(bsb.shape),
            _const_spec((a_width, d)),
            _const_spec((bw, d)),
        ],
        out_specs=_rows_spec(d),
        out_shape=jax.ShapeDtypeStruct((b, tt, d), F32),
        compiler_params=_cparams(("parallel", "parallel"), V7X_VMEM_LIMIT),
        name="even_out",
    )(o_fwd, o_bwd, pb, pb, pb, *x_args, mod, onw.reshape(1, HGRN_HEAD_DIM), vnw, ws, bsb,
      w_out[:a_width], w_out[a_width:])


def _odd_in_kernel(x_ref, mod_ref, nw_ref, win_ref, qaw_ref, kvaw_ref, wq_ref, qnw_ref, wkv_ref, knw_ref,
                   cq_ref, sq_ref, ck_ref, sk_ref, bc_ref,
                   xc_ctx_ref, xc_lat_ref, qt_ref, k_ref, vt_ref, *, heads, cw, ql, kvl):
    cq, sq, ck, sk = cq_ref[...], sq_ref[...], ck_ref[...], sk_ref[...]
    rope = MLA_ROPE
    o = cw + ql + kvl
    hw = heads * 128
    ones_pad = (lax.broadcasted_iota(jnp.int32, (V_PAD, TM), 0) == 0).astype(F32)
    is_ctx = pl.program_id(1) == 0

    def project(s):
        h = _modulate(x_ref[s], nw_ref[...], mod_ref[s, 0:1, :], mod_ref[s, 1:2, :])
        return _dot(h.astype(BF16), win_ref[...])

    def latents(s, p):
        xc = _dot(p[:, :cw].astype(BF16), bc_ref[...])
        xc_lat_ref[s] = xc.astype(xc_lat_ref.dtype)

        @pl.when(is_ctx)
        def _():
            xc_ctx_ref[s] = xc.astype(xc_ctx_ref.dtype)
        q_lat = (_rms(p[:, cw:cw + ql]) * qaw_ref[...]).astype(BF16)
        kv_lat = (_rms(p[:, cw + ql:cw + ql + kvl]) * kvaw_ref[...]).astype(BF16)
        kp, kps = p[:, o:o + 128], p[:, o + 128:o + 256]
        kpr = lax.rsqrt(jnp.sum(kp * kp, axis=-1, keepdims=True) * (1.0 / rope) + EPS)
        k_pe = (kp * ck + kps * sk) * kpr
        qf = _dot(q_lat, wq_ref[...])
        kvf = _dot(kv_lat, wkv_ref[...])
        return qf, kvf, k_pe

    def heads_out(s, qf, kvf, k_pe):
        for hh in range(heads):
            cs = slice(hh * 128, (hh + 1) * 128)
            qn = _rms(qf[:, cs]) * qnw_ref[...]
            qp = qf[:, hw + hh * 128:hw + (hh + 1) * 128]
            qps = qf[:, 2 * hw + hh * 128:2 * hw + (hh + 1) * 128]
            qpr = lax.rsqrt(jnp.sum(qp * qp, axis=-1, keepdims=True) * (1.0 / rope) + EPS)
            q_pe = (qp * cq + qps * sq) * qpr
            qt_ref[s, hh] = jnp.concatenate([qn, q_pe], axis=1).T.astype(qt_ref.dtype)
            kn = _rms(kvf[:, cs]) * knw_ref[...]
            k_ref[s, hh] = jnp.concatenate([kn, k_pe], axis=1).astype(k_ref.dtype)
            vt_ref[s, hh] = jnp.concatenate([kvf[:, hw + hh * 128:hw + (hh + 1) * 128].T, ones_pad],
                                            axis=0).astype(vt_ref.dtype)

    mid = latents(0, project(0))
    for s in range(1, NB):
        p_next = project(s)
        heads_out(s - 1, *mid)
        mid = latents(s, p_next)
    heads_out(NB - 1, *mid)


def _odd_in(x, mod, nw, wts, heads, cw, ql, kvl):
    b, tt, d = x.shape
    consts = [wts[k] for k in ("w_in", "qa_w", "kva_w", "w_q", "qn_w", "w_kv", "kn_w")]
    tables = [wts[k] for k in ("cq", "sq", "ck", "sk")]
    row128 = pl.BlockSpec((TM, 128), lambda b, i: (i, 0))
    in_specs = ([_rows_spec(d), _mod_spec(d, NB), _const_spec((1, d))]
                + [_const_spec(a.shape) for a in consts] + [row128] * 4 + [_const_spec(wts["bc"].shape)])
    return pl.pallas_call(
        functools.partial(_odd_in_kernel, heads=heads, cw=cw, ql=ql, kvl=kvl),
        grid=(b // NB, tt // TM),
        in_specs=in_specs,
        out_specs=[
            pl.BlockSpec((NB, TM, 2 * cw), lambda b, i: (b, 0, 0)),
            pl.BlockSpec((NB, TM, 2 * cw), lambda b, i: (b, jnp.maximum(i - 1, 0), 0)),
            pl.BlockSpec((NB, heads, 256, TM), lambda b, i: (b, 0, 0, i)),
            pl.BlockSpec((NB, heads, TM, 256), lambda b, i: (b, 0, i, 0)),
            pl.BlockSpec((NB, heads, MLA_V_DIM + V_PAD, TM), lambda b, i: (b, 0, 0, i)),
        ],
        out_shape=[
            jax.ShapeDtypeStruct((b, TM, 2 * cw), BF16),
            jax.ShapeDtypeStruct((b, tt - TM, 2 * cw), F32),
            jax.ShapeDtypeStruct((b, heads, 256, tt), BF16),
            jax.ShapeDtypeStruct((b, heads, tt, 256), BF16),
            jax.ShapeDtypeStruct((b, heads, MLA_V_DIM + V_PAD, tt), BF16),
        ],
        compiler_params=_cparams(("parallel", "arbitrary"), V7X_VMEM_LIMIT),
        name="odd_in",
    )(x, mod, nw, *consts, *tables, wts["bc"])


def _attn_kernel(qt_ref, k_ref, vt_ref, o_ref, sa_ref, sb_ref, ma_ref, mb_ref, *, ctx_len, n_tiles):
    i = pl.program_id(2)
    n_all = k_ref.shape[1]
    nh = k_ref.shape[0]
    bufs = ((sa_ref, ma_ref), (sb_ref, mb_ref))

    def chunks(n_keys):
        return [(st, min(ATTN_KC, n_keys - st)) for st in range(0, n_keys, ATTN_KC)]

    def run(step_parity, keys1, keys2):
        s_w, m_w = bufs[step_parity]
        s_r, m_r = bufs[1 - step_parity]
        c1 = chunks(keys1) if keys1 else []
        c2 = chunks(keys2) if keys2 else []
        if keys1:
            qts = [qt_ref[hh] for hh in range(nh)]
        if keys2:
            m_prev = [m_r[hh] for hh in range(nh)]
        m, acc = [None] * nh, [None] * nh
        for idx in range(max(len(c1), len(c2))):
            for hh in range(nh):
                if idx < len(c1):
                    st, sz = c1[idx]
                    s = _dot(k_ref[hh, st:st + sz, :], qts[hh])
                    s_w[hh, st:st + sz, :] = s
                    cm = jnp.max(s, axis=0, keepdims=True)
                    m[hh] = cm if m[hh] is None else jnp.maximum(m[hh], cm)
                if idx < len(c2):
                    st, sz = c2[idx]
                    p = jnp.exp2(s_r[hh, st:st + sz, :] - m_prev[hh]).astype(BF16)
                    ca = _dot(vt_ref[hh, :, st:st + sz], p)
                    acc[hh] = ca if acc[hh] is None else acc[hh] + ca
        dv = o_ref.shape[1] // nh
        for hh in range(nh):
            if keys1:
                m_w[hh] = m[hh]
            if keys2:
                o_ref[:, hh * dv:(hh + 1) * dv] = (acc[hh][:dv] * (1.0 / acc[hh][dv:dv + 1])).T.astype(o_ref.dtype)

    @pl.when(i == 0)
    def _():
        run(0, ctx_len, None)

    @pl.when(i == 1)
    def _():
        run(1, n_all, ctx_len)

    for parity in (0, 1):
        @pl.when(jnp.logical_and(jnp.logical_and(i >= 2, i < n_tiles), i % 2 == parity))
        def _():
            run(parity, n_all, n_all)

    @pl.when(i == n_tiles)
    def _():
        run(n_tiles % 2, None, n_all)


def _attention(qt, k, vt, ctx_len):
    b, heads, tt, dq = k.shape
    dvp = vt.shape[2]
    dv = dvp - V_PAD
    n_tiles = tt // TM
    nh = ATTN_HEADS_PER_STEP
    assert heads % nh == 0
    return pl.pallas_call(
        functools.partial(_attn_kernel, ctx_len=ctx_len, n_tiles=n_tiles),
        grid=(b, heads // nh, n_tiles + 1),
        in_specs=[
            pl.BlockSpec((None, nh, dq, TM), lambda b, h, i: (b, h, 0, jnp.minimum(i, n_tiles - 1))),
            pl.BlockSpec((None, nh, tt, dq), lambda b, h, i: (b, h, 0, 0)),
            pl.BlockSpec((None, nh, dvp, tt), lambda b, h, i: (b, h, 0, 0)),
        ],
        out_specs=pl.BlockSpec((None, TM, nh * dv), lambda b, h, i: (b, jnp.maximum(i - 1, 0), h)),
        out_shape=jax.ShapeDtypeStruct((b, tt, heads * dv), BF16),
        scratch_shapes=[pltpu.VMEM((nh, tt, TM), F32), pltpu.VMEM((nh, tt, TM), F32),
                        pltpu.VMEM((nh, 1, TM), F32), pltpu.VMEM((nh, 1, TM), F32)],
        compiler_params=_cparams(("parallel", "parallel", "arbitrary"), V7X_VMEM_LIMIT),
        name="attention",
    )(qt, k, vt)


def _dft_kernel(c_ref, s_ref, x_ref, o_ref):
    half = x_ref.shape[1] // 2
    y = _dot(c_ref[...], x_ref[:, :half]) + _dot(s_ref[...], x_ref[:, half:])
    o_ref[...] = y.astype(o_ref.dtype)


def _dft_dense(xc, t_cos, t_sin):
    b, t, w2 = xc.shape
    return pl.pallas_call(
        _dft_kernel,
        grid=(b,),
        in_specs=[_const_spec((t, t)), _const_spec((t, t)), pl.BlockSpec((None, t, w2), lambda b: (b, 0, 0))],
        out_specs=pl.BlockSpec((None, t, w2 // 2), lambda b: (b, 0, 0)),
        out_shape=jax.ShapeDtypeStruct((b, t, w2 // 2), BF16),
        compiler_params=_cparams(("parallel",), V7X_VMEM_LIMIT),
        name="dft_dense",
    )(t_cos, t_sin, xc)


def _dft_split(t):
    n2 = 1 << ((t.bit_length()) // 2)
    return t // n2, n2


def _swap_major(x):
    return jnp.transpose(x, (1, 0, 2))


def _dft_two_stage_kernel(xr_ref, xi_ref, g_ref, cs2_ref, o_ref, zr_ref, zi_ref, y_ref, *, n1, n2):
    lw = xr_ref.shape[1]
    xr = _swap_major(xr_ref[...].reshape(n1, n2, lw))
    xi = _swap_major(xi_ref[...].reshape(n1, n2, lw))
    for t2 in range(n2):
        p = _dot(g_ref[t2], jnp.concatenate([xr[t2], xi[t2]], axis=1).astype(BF16))
        zr_ref[t2] = p[:n1, :lw] - p[n1:, lw:]
        zi_ref[t2] = p[:n1, lw:] + p[n1:, :lw]
    zr = _swap_major(zr_ref[...])
    zi = _swap_major(zi_ref[...])
    for u1 in range(n1):
        y_ref[u1] = _dot(cs2_ref[...], jnp.concatenate([zr[u1], zi[u1]], axis=0).astype(BF16))
    o_ref[...] = _swap_major(y_ref[...]).reshape(n1 * n2, lw).astype(o_ref.dtype)


def _dft_two_stage(xc, tables):
    g, cs2 = tables
    b, t, w2 = xc.shape
    cw = w2 // 2
    lw = 128
    n1, n2 = _dft_split(t)
    return pl.pallas_call(
        functools.partial(_dft_two_stage_kernel, n1=n1, n2=n2),
        grid=(b, cw // lw),
        in_specs=[pl.BlockSpec((None, t, lw), lambda b, j: (b, 0, j)),
                  pl.BlockSpec((None, t, lw), lambda b, j: (b, 0, cw // lw + j)),
                  _const_spec(g.shape), _const_spec(cs2.shape)],
        out_specs=pl.BlockSpec((None, t, lw), lambda b, j: (b, 0, j)),
        out_shape=jax.ShapeDtypeStruct((b, t, cw), BF16),
        scratch_shapes=[pltpu.VMEM((n2, n1, lw), F32), pltpu.VMEM((n2, n1, lw), F32), pltpu.VMEM((n1, n2, lw), F32)],
        compiler_params=_cparams(("parallel", "parallel"), V7X_VMEM_LIMIT),
        name="dft_two_stage",
    )(xc, xc, g, cs2)


def _odd_out_kernel(fc_ref, fl_ref, at_ref, x_ref, mod_ref, wf_ref, wa_ref, o_ref):
    is_ctx = pl.program_id(1) == 0
    fm = jnp.concatenate([jnp.where(is_ctx, fc_ref[s], fl_ref[s]) for s in range(NB)], axis=0)
    at = jnp.concatenate([at_ref[s] for s in range(NB)], axis=0)
    y = _dot(fm, wf_ref[...]) + _dot(at, wa_ref[...])
    for s in range(NB):
        o_ref[s] = x_ref[s] + mod_ref[s, 2:3, :] * y[s * TM:(s + 1) * TM]


def _odd_out(fm_ctx, fm_lat, attn, x, mod, w_out):
    b, tt, d = x.shape
    cw = fm_ctx.shape[-1]
    aw = attn.shape[-1]
    return pl.pallas_call(
        _odd_out_kernel,
        grid=(b // NB, tt // TM),
        in_specs=[
            pl.BlockSpec((NB, TM, cw), lambda b, i: (b, 0, 0)),
            pl.BlockSpec((NB, TM, cw), lambda b, i: (b, jnp.maximum(i - 1, 0), 0)),
            _rows_spec(aw),
            _rows_spec(d),
            _mod_spec(d, NB),
            _const_spec((cw, d)),
            _const_spec((aw, d)),
        ],
        out_specs=_rows_spec(d),
        out_shape=jax.ShapeDtypeStruct((b, tt, d), F32),
        compiler_params=_cparams(("parallel", "parallel"), V7X_VMEM_LIMIT),
        name="odd_out",
    )(fm_ctx, fm_lat, attn, x, mod, w_out[:cw], w_out[cw:])


def _ffn_kernel(xp_ref, x_ref, xn_ref, mod_ref, nw_ref, wg_ref, wv_ref, cw_ref, cb_ref, wd_ref, o_ref,
                *, first_tile, n_tiles):
    i = pl.program_id(1) + first_tile
    ext = TM + 2 * HALO
    row = lax.broadcasted_iota(jnp.int32, (ext, 1), 0)
    keep = jnp.logical_and(jnp.logical_or(row >= HALO, i > 1),
                           jnp.logical_or(row < HALO + TM, jnp.logical_and(i > 0, i < n_tiles - 1)))
    nw = nw_ref[...]
    hes = []
    for s in range(NB):
        xe = jnp.concatenate([xp_ref[s], x_ref[s], xn_ref[s]], axis=0)
        hes.append(jnp.where(keep, _modulate(xe, nw, mod_ref[s, 3:4, :], mod_ref[s, 4:5, :]), 0.0))
    ge = _dot(jnp.concatenate(hes, axis=0).astype(BF16), wg_ref[...])
    val = _dot(jnp.concatenate([h[HALO:HALO + TM] for h in hes], axis=0).astype(BF16), wv_ref[...])
    cw0, cw1, cw2 = cw_ref[0:1, :], cw_ref[1:2, :], cw_ref[2:3, :]
    gc = jnp.concatenate(
        [ge[s * ext + HALO - 1:s * ext + HALO - 1 + TM] * cw0 + ge[s * ext + HALO:s * ext + HALO + TM] * cw1
         + ge[s * ext + HALO + 1:s * ext + HALO + 1 + TM] * cw2 for s in range(NB)], axis=0) + cb_ref[...]
    y = _dot((_silu(gc) * val).astype(BF16), wd_ref[...])
    for s in range(NB):
        o_ref[s] = x_ref[s] + mod_ref[s, 5:6, :] * y[s * TM:(s + 1) * TM]


def _conv_ffn(x, mod, nw, w_up, conv_w, conv_b, wd, skip_ctx):
    b, tt, d = x.shape
    ff = wd.shape[0]
    n_tiles = tt // TM
    first = 1 if skip_ctx else 0
    r = TM // HALO
    nblk = tt // HALO
    return pl.pallas_call(
        functools.partial(_ffn_kernel, first_tile=first, n_tiles=n_tiles),
        grid=(b // NB, n_tiles - first),
        in_specs=[
            pl.BlockSpec((NB, HALO, d), lambda b, i: (b, jnp.maximum((i + first) * r - 1, 0), 0)),
            pl.BlockSpec((NB, TM, d), lambda b, i: (b, i + first, 0)),
            pl.BlockSpec((NB, HALO, d), lambda b, i: (b, jnp.minimum((i + first + 1) * r, nblk - 1), 0)),
            pl.BlockSpec((None, NB, 6, d), lambda b, i: (jnp.minimum(i + first, 1), b, 0, 0)),
            _const_spec((1, d)),
            pl.BlockSpec((d, ff), lambda b, i: (0, 0), pipeline_mode=pl.Buffered(1)),
            pl.BlockSpec((d, ff), lambda b, i: (0, 1), pipeline_mode=pl.Buffered(1)),
            _const_spec((CONV_W, ff)),
            _const_spec((1, ff)),
            _const_spec((ff, d)),
        ],
        out_specs=_rows_spec(d),
        out_shape=jax.ShapeDtypeStruct((b, tt - first * TM, d), F32),
        compiler_params=_cparams(("parallel", "parallel"), V7X_VMEM_LIMIT),
        name="conv_ffn",
    )(x, x, x, mod, nw, w_up, w_up, conv_w, conv_b, wd)


def _rope_swap_perm():
    q = MLA_ROPE // 4
    return np.concatenate([np.arange(q, 2 * q), np.arange(0, q), np.arange(3 * q, 4 * q), np.arange(2 * q, 3 * q)])


def _pad128(v):
    return jnp.pad(v, (0, 128 - v.shape[0])).reshape(1, 128)


def _odd_weights(w_in, qa_w, w_qb, kva_w, w_kvb, qn_w, kn_w, heads, cw, ql, kvl, cos_t, sin_t):
    d = w_in.shape[0]
    perm = _rope_swap_perm()
    nope, rope, qk = MLA_NOPE, MLA_ROPE, MLA_NOPE + MLA_ROPE
    o = cw + ql + kvl
    z = jnp.zeros((d, 128 - rope), w_in.dtype)
    kpe = w_in[:, o:o + rope]
    w_in_ext = jnp.concatenate([w_in[:, :o], kpe, z, kpe[:, perm], z], axis=1).astype(BF16)
    wq = w_qb.reshape(ql, heads, qk)
    zq = jnp.zeros((ql, heads, 128 - rope), w_qb.dtype)
    wq_rope = wq[:, :, nope:]
    w_q = jnp.concatenate([
        wq[:, :, :nope].reshape(ql, heads * 128),
        jnp.concatenate([wq_rope, zq], axis=2).reshape(ql, heads * 128),
        jnp.concatenate([wq_rope[:, :, perm], zq], axis=2).reshape(ql, heads * 128),
    ], axis=1).astype(BF16)
    wkv = w_kvb.reshape(kvl, heads, nope + MLA_V_DIM)
    w_kv = jnp.concatenate([wkv[:, :, :nope].reshape(kvl, heads * nope),
                            wkv[:, :, nope:].reshape(kvl, heads * MLA_V_DIM)], axis=1).astype(BF16)
    gd = cw // FOURIER_GROUPS
    jk = (np.arange(gd)[:, None] * np.arange(gd)[None, :]) % gd
    ang = 2.0 * np.pi * jk / gd
    eye = np.eye(FOURIER_GROUPS)
    bc = np.concatenate([np.kron(eye, np.cos(ang)), np.kron(eye, -np.sin(ang))], axis=1) / np.sqrt(gd)
    scale = float(qk) ** -0.5 * LOG2_E
    return {
        "w_in": w_in_ext, "qa_w": qa_w.reshape(1, ql), "kva_w": kva_w.reshape(1, kvl),
        "w_q": w_q, "qn_w": (qn_w[:nope] * scale).reshape(1, nope),
        "w_kv": w_kv, "kn_w": kn_w[:nope].reshape(1, nope),
        "cq": cos_t * (_pad128(qn_w[nope:]) * scale), "sq": sin_t * (_pad128(qn_w[nope:][perm]) * scale),
        "ck": cos_t * _pad128(kn_w[nope:]), "sk": sin_t * _pad128(kn_w[nope:][perm]),
        "bc": jnp.asarray(bc, F32).astype(BF16),
    }


def _rope_tables(t_lat, ctx_len):
    rows = t_lat // GRID_W
    row = jnp.repeat(jnp.arange(rows), GRID_W)
    col = jnp.tile(jnp.arange(GRID_W), rows)
    r_axis = MLA_ROPE // 2
    inv_freq = ROPE_THETA ** (-jnp.arange(0, r_axis, 2, dtype=F32) / r_axis)
    ang = jnp.stack([row, col], axis=-1).astype(F32)[:, :, None] * inv_freq
    cos, sin = jnp.cos(ang), jnp.sin(ang)
    cos64 = jnp.concatenate([cos[:, 0], cos[:, 0], cos[:, 1], cos[:, 1]], axis=-1)
    sin64 = jnp.concatenate([-sin[:, 0], sin[:, 0], -sin[:, 1], sin[:, 1]], axis=-1)
    cos64 = jnp.concatenate([jnp.ones((ctx_len, MLA_ROPE), F32), cos64], axis=0)
    sin64 = jnp.concatenate([jnp.zeros((ctx_len, MLA_ROPE), F32), sin64], axis=0)
    pad = ((0, 0), (0, 128 - MLA_ROPE))
    return jnp.pad(cos64, pad), jnp.pad(sin64, pad)


def _cos_sin(phase, period, scale):
    ang = (phase % period).astype(F32) * (2.0 * np.pi / period)
    return jnp.cos(ang) * scale, jnp.sin(ang) * scale


def _dft_dense_tables(t):
    idx = jnp.arange(t, dtype=jnp.int32)
    c, s = _cos_sin(idx[:, None] * idx[None, :], t, 1.0 / np.sqrt(t))
    return c.astype(BF16), s.astype(BF16)


def _dft_two_stage_tables(t):
    n1, n2 = _dft_split(t)
    u1 = jnp.arange(n1, dtype=jnp.int32)
    pos = jnp.arange(t, dtype=jnp.int32).reshape(n1, n2)
    c, s = _cos_sin(pos.T[:, None, :] * u1[None, :, None], t, 1.0 / np.sqrt(n1))
    g = jnp.concatenate([c, -s], axis=1).astype(BF16)
    i2 = jnp.arange(n2, dtype=jnp.int32)
    c2, s2 = _cos_sin(i2[:, None] * i2[None, :], n2, 1.0 / np.sqrt(n2))
    return g, jnp.concatenate([c2, s2], axis=1).astype(BF16)


def kernel(x, c, ctx, c_ctx, ada_w, ada_b, norm_mix_w, norm_ffn_w, ev_w_in, ev_lb_logits, ev_onorm_w, ev_vnorm_w, ev_ws, ev_bs, ev_w_out, od_w_in, od_qa_norm_w, od_w_qb, od_kva_norm_w, od_w_kvb, od_q_norm_w, od_k_norm_w, od_w_out, ffn_w_up, ffn_conv_w, ffn_conv_b, ffn_w_down):
    bsz, t_lat, d = x.shape
    ctx_len = ctx.shape[1]
    depth = ada_w.shape[0]
    assert ctx_len == TM and t_lat % ATTN_KC == 0 and t_lat % GRID_W == 0 and bsz % NB == 0
    a_width = ev_lb_logits.shape[-1]
    ql, kvl = od_qa_norm_w.shape[-1], od_kva_norm_w.shape[-1]
    cw = od_w_in.shape[-1] - ql - kvl - MLA_ROPE
    heads = od_w_qb.shape[-1] // (MLA_NOPE + MLA_ROPE)
    d_ff = ffn_w_down.shape[1]

    pad_rows = (-(bsz + 1)) % 8
    cvec = jnp.concatenate([c, c_ctx[None, :], jnp.zeros((pad_rows, d), F32)], axis=0)
    mods = _ada_tables(cvec, ada_w, ada_b)
    mod_lat = mods[:, :bsz].reshape(depth, 1, bsz, 6, d)
    mod_ctx = jnp.broadcast_to(mods[:, bsz].reshape(depth, 1, 1, 6, d), (depth, 1, bsz, 6, d))
    mods = jnp.concatenate([mod_ctx, mod_lat], axis=1)

    lb_p = jax.nn.softmax(ev_lb_logits.astype(F32), axis=0)
    lbs = jnp.cumsum(lb_p, axis=0) - lb_p[0]
    cos_t, sin_t = _rope_tables(t_lat, ctx_len)
    dft_lat = dft_ctx = None

    xs = (ctx, x)
    for l in range(depth):
        last = l == depth - 1
        mod = mods[l]
        nmw = norm_mix_w[l].reshape(1, d)
        if l % 2 == 0:
            e = l // 2
            pz, pb = _even_in(xs, ctx_len + t_lat, mod, nmw, ev_w_in[e].astype(BF16), a_width)
            o_fwd, o_bwd = _hgrn_scan(pz, pb, lbs[e], ctx_len, a_width)
            bw = d - a_width
            gd = bw // ev_ws.shape[1]
            bsb = jnp.repeat(ev_bs[e].T, gd, axis=1)
            xs = _even_out(o_fwd, o_bwd, ev_onorm_w[e], pb, xs, mod, ev_vnorm_w[e].reshape(1, bw),
                           ev_ws[e].astype(BF16), bsb, ev_w_out[e].astype(BF16), a_width)
        else:
            o = l // 2
            if dft_lat is None:
                dft_lat, dft_ctx = _dft_two_stage_tables(t_lat), _dft_dense_tables(ctx_len)
            wts = _odd_weights(od_w_in[o], od_qa_norm_w[o], od_w_qb[o], od_kva_norm_w[o], od_w_kvb[o],
                               od_q_norm_w[o], od_k_norm_w[o], heads, cw, ql, kvl, cos_t, sin_t)
            xc_ctx, xc_lat, qt, k, vt = _odd_in(xs, mod, nmw, wts, heads, cw, ql, kvl)
            attn = _attention(qt, k, vt, ctx_len)
            fm_ctx = _dft_dense(xc_ctx, *dft_ctx)
            fm_lat = _dft_two_stage(xc_lat, dft_lat)
            xs = _odd_out(fm_ctx, fm_lat, attn, xs, mod, od_w_out[o].astype(BF16))
        xs = _conv_ffn(xs, mod, norm_ffn_w[l].reshape(1, d), ffn_w_up[l].astype(BF16), ffn_conv_w[l],
                       ffn_conv_b[l].reshape(1, d_ff), ffn_w_down[l].astype(BF16), skip_ctx=last)
    return xs
```

```python
import functools

import numpy as np
import jax
import jax.numpy as jnp
from jax import lax
from jax.experimental import pallas as pl
from jax.experimental.pallas import tpu as pltpu

F32 = jnp.float32
BF16 = jnp.bfloat16

EPS = 1e-6
F_MIN = 1e-6
LOG2_E = float(np.log2(np.e))
GRID_W = 64
ROPE_THETA = 10000.0
HGRN_HEAD_DIM = 128
HGRN_CHUNK = 64
HGRN_STEP_CHUNKS = 4
MLP_CHUNK = 128
MLA_NOPE = 128
MLA_ROPE = 64
MLA_V_DIM = 128
V_PAD = 16
FOURIER_GROUPS = 4
CONV_W = 3

TM = 256
NB = 2
HALO = 8
ATTN_KC = 512
ATTN_HEADS_PER_STEP = 3
ADA_TN = 1536
V7X_VMEM_LIMIT = 56 * 1024 * 1024


def _cparams(sem, vmem=None):
    return pltpu.CompilerParams(dimension_semantics=sem, vmem_limit_bytes=vmem)


def _dot(a, b):
    return jnp.dot(a, b, preferred_element_type=F32)


def _dot_nt(a, b):
    return lax.dot_general(a, b, (((1,), (1,)), ((), ())), preferred_element_type=F32)


def _dot_tn(a, b):
    return lax.dot_general(a, b, (((0,), (0,)), ((), ())), preferred_element_type=F32)


def _silu(x):
    return x * jax.nn.sigmoid(x)


def _gelu_tanh(x):
    c = 0.7978845608028654
    return x * (0.5 + 0.5 * jnp.tanh(x * (c + (c * 0.044715) * (x * x))))


def _rms(x):
    return x * lax.rsqrt(jnp.sum(x * x, axis=-1, keepdims=True) * (1.0 / x.shape[-1]) + EPS)


def _modulate(x, nw, shift, scale):
    return (_rms(x) * nw) * (1.0 + scale) + shift


def _mod_spec(d, nb):
    return pl.BlockSpec((None, nb, 6, d), lambda b, i: (jnp.minimum(i, 1), b, 0, 0))


def _rows_spec(cols, colblk=0):
    return pl.BlockSpec((NB, TM, cols), lambda b, i: (b, i, colblk))


def _const_spec(shape):
    nd = len(shape)
    return pl.BlockSpec(shape, lambda *_: (0,) * nd, pipeline_mode=pl.Buffered(1))


def _ada_kernel(c_ref, w_ref, b_ref, o_ref):
    s = _silu(c_ref[...]).astype(BF16)
    o_ref[...] = _dot(s, w_ref[...].astype(BF16)) + b_ref[...]


def _ada_tables(cvec, ada_w, ada_b):
    depth, d, n = ada_w.shape
    rows = cvec.shape[0]
    tn = ADA_TN
    return pl.pallas_call(
        _ada_kernel,
        grid=(depth, n // tn),
        in_specs=[
            pl.BlockSpec((rows, d), lambda l, j: (0, 0)),
            pl.BlockSpec((None, d, tn), lambda l, j: (l, 0, j)),
            pl.BlockSpec((None, 1, tn), lambda l, j: (l, 0, j)),
        ],
        out_specs=pl.BlockSpec((None, rows, tn), lambda l, j: (l, 0, j)),
        out_shape=jax.ShapeDtypeStruct((depth, rows, n), F32),
        compiler_params=_cparams(("parallel", "parallel"), V7X_VMEM_LIMIT),
        name="ada_tables",
    )(cvec, ada_w, ada_b.reshape(depth, 1, n))


def _stream_specs(xs, d):
    if isinstance(xs, tuple):
        return ([pl.BlockSpec((NB, TM, d), lambda b, i: (b, 0, 0)),
                 pl.BlockSpec((NB, TM, d), lambda b, i: (b, jnp.maximum(i - 1, 0), 0))], list(xs))
    return [_rows_spec(d)], [xs]


def _stream_rows(x_refs, s):
    if len(x_refs) == 1:
        return x_refs[0][s]
    return jnp.where(pl.program_id(1) == 0, x_refs[0][s], x_refs[1][s])


def _even_in_kernel(*refs, n_x, a_width):
    x_refs = refs[:n_x]
    mod_ref, nw_ref, w_ref, oz_ref, ob_ref = refs[n_x:]
    nw = nw_ref[...]
    h = jnp.concatenate([_modulate(_stream_rows(x_refs, s), nw, mod_ref[s, 0:1, :], mod_ref[s, 1:2, :]).astype(BF16)
                         for s in range(NB)], axis=0)
    p = _dot(h, w_ref[...])
    z0, z1 = a_width * 2, a_width * 4
    for s in range(NB):
        rows = slice(s * TM, (s + 1) * TM)
        oz_ref[s] = p[rows, z0:z1]
        ob_ref[s] = jnp.concatenate([p[rows, :z0], p[rows, z1:]], axis=1).astype(ob_ref.dtype)


def _even_in(xs, tt, mod, nw, w_in, a_width):
    d, n = w_in.shape
    x_specs, x_args = _stream_specs(xs, d)
    b = x_args[0].shape[0]
    return pl.pallas_call(
        functools.partial(_even_in_kernel, n_x=len(x_args), a_width=a_width),
        grid=(b // NB, tt // TM),
        in_specs=x_specs + [
            _mod_spec(d, NB),
            _const_spec((1, d)),
            _const_spec((d, n)),
        ],
        out_specs=[_rows_spec(2 * a_width), _rows_spec(n - 2 * a_width)],
        out_shape=[jax.ShapeDtypeStruct((b, tt, 2 * a_width), F32),
                   jax.ShapeDtypeStruct((b, tt, n - 2 * a_width), BF16)],
        compiler_params=_cparams(("parallel", "parallel"), V7X_VMEM_LIMIT),
        name="even_in",
    )(*x_args, mod, nw, w_in)


def _hgrn_consts(reverse):
    c = HGRN_CHUNK
    idx = np.arange(c)
    if not reverse:
        tri = (idx[None, :] <= idx[:, None]).astype(np.float32)
    else:
        tri = (idx[None, :] >= idx[:, None]).astype(np.float32)
    masks = []
    for w in (32, 16, 8):
        grp = idx // (2 * w)
        qrow = (idx % (2 * w) >= w) if not reverse else (idx % (2 * w) < w)
        masks.append((grp[:, None] == grp[None, :]) & qrow[:, None] & (~qrow)[None, :])
    blk = idx // 8
    causal = (idx[None, :] <= idx[:, None]) if not reverse else (idx[None, :] >= idx[:, None])
    masks.append((blk[:, None] == blk[None, :]) & causal)
    return np.kron(np.eye(HGRN_STEP_CHUNKS), tri).astype(np.float32), np.stack(masks).astype(np.float32)


def _hgrn_prepare(pq_ref, pv_ref, pz_ref, lb, tri_ref, width):
    f = jnp.maximum(lb + (1.0 - lb) * jax.nn.sigmoid(pz_ref[...]), F_MIN)
    g = jnp.log(f) * LOG2_E
    g1 = g.astype(BF16)
    r1 = g - g1.astype(F32)
    g2 = r1.astype(BF16)
    b3 = _dot(tri_ref[...], jnp.concatenate([g1, g2], axis=1))
    b_all = b3[:, :width] + b3[:, width:]
    return _silu(pq_ref[...].astype(F32)).astype(BF16), (1.0 - f).astype(BF16), pv_ref[...], b_all


def _hgrn_chunk(feats, ci, reverse, masks_ref, st, heads):
    c, hd = HGRN_CHUNK, HGRN_HEAD_DIM
    width = heads * hd
    q, kk, v, b_all = feats
    rs = slice(ci * c, (ci + 1) * c)
    last = c - 1 if not reverse else 0
    parts = []
    for h in range(heads):
        sl = slice(h * hd, (h + 1) * hd)
        qh, kh, b = q[rs, sl], kk[rs, sl], b_all[rs, sl]

        def ref_rows(rows, n):
            return jnp.concatenate([jnp.broadcast_to(b[r:r + 1, :], (n, hd)) for r in rows], axis=0)

        a = None
        for lv, w in enumerate((32, 16, 8)):
            mids = [gi * 2 * w + (w - 1 if not reverse else w) for gi in range(c // (2 * w))]
            ew = jnp.exp2(-jnp.abs(b - ref_rows(mids, 2 * w))).astype(BF16)
            t = _dot_nt(qh * ew, kh * ew) * masks_ref[lv]
            a = t if a is None else a + t
        anchor = ref_rows([8 * m + (3 if not reverse else 4) for m in range(c // 8)], 8)
        d8 = _dot_nt(qh * jnp.exp2(b - anchor).astype(BF16), kh * jnp.exp2(anchor - b).astype(BF16))
        a = a + jnp.where(masks_ref[3] > 0.5, d8, 0.0)
        vb = v[rs, sl]
        e_in = jnp.exp2(b)
        qhat = qh * e_in.astype(BF16)
        khat = kh * jnp.exp2(ref_rows([last], c) - b).astype(BF16)
        parts.append((a.astype(BF16), vb, qhat, _dot_tn(vb, khat), e_in[last:last + 1, :]))
    heads_out = []
    for h, (a, vb, qhat, upd, e_last) in enumerate(parts):
        heads_out.append(_dot(a, vb) + _dot_nt(qhat, st[h].astype(BF16)))
        st[h] = st[h] * e_last + upd
    return jnp.concatenate(heads_out, axis=1)


def _hgrn_kernel(fq_ref, fv_ref, fz_ref, bq_ref, bv_ref, bz_ref, lb_ref, ftri_ref, btri_ref, fmask_ref, bmask_ref,
                 of_ref, ob_ref, fst_ref, bst_ref, *, heads):
    nck = HGRN_STEP_CHUNKS
    width = heads * HGRN_HEAD_DIM

    @pl.when(pl.program_id(1) == 0)
    def _():
        fst_ref[...] = jnp.zeros_like(fst_ref)
        bst_ref[...] = jnp.zeros_like(bst_ref)

    ffeats = _hgrn_prepare(fq_ref, fv_ref, fz_ref, lb_ref[0:1, :], ftri_ref, width)
    bfeats = _hgrn_prepare(bq_ref, bv_ref, bz_ref, lb_ref[1:2, :], btri_ref, width)
    fst = [fst_ref[h] for h in range(heads)]
    bst = [bst_ref[h] for h in range(heads)]
    fouts, bouts = [None] * nck, [None] * nck
    for k in range(nck):
        fouts[k] = _hgrn_chunk(ffeats, k, False, fmask_ref, fst, heads)
        bouts[nck - 1 - k] = _hgrn_chunk(bfeats, nck - 1 - k, True, bmask_ref, bst, heads)
    for h in range(heads):
        fst_ref[h] = fst[h]
        bst_ref[h] = bst[h]
    of_ref[...] = jnp.concatenate(fouts, axis=0).astype(of_ref.dtype)
    ob_ref[...] = jnp.concatenate(bouts, axis=0).astype(ob_ref.dtype)


def _hgrn_scan(pz, pb, lbs, ctx_len, width):
    b, tt, _ = pz.shape
    rows = HGRN_CHUNK * HGRN_STEP_CHUNKS
    heads = width // HGRN_HEAD_DIM
    nb, nbc = tt // rows, ctx_len // rows
    ftri, fmask = _hgrn_consts(False)
    btri, bmask = _hgrn_consts(True)

    def bblk(j):
        return jnp.where(j < nbc, nbc - 1 - j, nb - 1 - (j - nbc))

    def fcol(k):
        return pl.BlockSpec((None, rows, width), lambda b, j: (b, j, k))

    def bcol(k):
        return pl.BlockSpec((None, rows, width), lambda b, j: (b, bblk(j), k))

    consts = [lbs, jnp.asarray(ftri, BF16), jnp.asarray(btri, BF16), jnp.asarray(fmask, F32), jnp.asarray(bmask, F32)]
    state = pltpu.VMEM((heads, HGRN_HEAD_DIM, HGRN_HEAD_DIM), F32)
    return pl.pallas_call(
        functools.partial(_hgrn_kernel, heads=heads),
        grid=(b, nb),
        in_specs=[fcol(0), fcol(1), fcol(0), bcol(0), bcol(1), bcol(1)] + [_const_spec(a.shape) for a in consts],
        out_specs=[fcol(0), bcol(0)],
        out_shape=[jax.ShapeDtypeStruct((b, tt, width), BF16)] * 2,
        scratch_shapes=[state, state],
        compiler_params=_cparams(("parallel", "arbitrary"), V7X_VMEM_LIMIT),
        name="hgrn_scan",
    )(pb, pb, pz, pb, pb, pz, *consts)


def _even_out_kernel(*refs, groups, n_x):
    of_ref, ob_ref, pg_ref, pu_ref, pv_ref = refs[:5]
    x_refs = refs[5:5 + n_x]
    mod_ref, onw_ref, vnw_ref, ws_ref, bsb_ref, wa_ref, wb_ref, o_ref = refs[5 + n_x:]
    gd = pu_ref.shape[2] // groups
    hd = HGRN_HEAD_DIM
    onw = onw_ref[...]
    ms, yas = [], []
    for s in range(NB):
        o = of_ref[s].astype(F32) + ob_ref[s].astype(F32)
        yn = jnp.concatenate([_rms(o[:, h * hd:(h + 1) * hd]) * onw for h in range(o.shape[1] // hd)], axis=1)
        yas.append((yn * _silu(pg_ref[s].astype(F32))).astype(BF16))
    y = _dot(jnp.concatenate(yas, axis=0), wa_ref[...])
    for s in range(NB):
        for n in range(TM // MLP_CHUNK):
            rows = slice(n * MLP_CHUNK, (n + 1) * MLP_CHUNK)
            u = _gelu_tanh(pu_ref[s, rows, :].astype(F32))
            v = _gelu_tanh(pv_ref[s, rows, :].astype(F32))
            parts = []
            for g in range(groups):
                cs = slice(g * gd, (g + 1) * gd)
                vg = (_rms(v[:, cs]) * vnw_ref[:, cs]).astype(BF16)
                sv = _dot(ws_ref[g], vg) + bsb_ref[:, cs]
                parts.append(u[:, cs] * sv)
            ms.append(jnp.concatenate(parts, axis=1).astype(BF16))
    y = y + _dot(jnp.concatenate(ms, axis=0), wb_ref[...])
    for s in range(NB):
        o_ref[s] = _stream_rows(x_refs, s) + mod_ref[s, 2:3, :] * y[s * TM:(s + 1) * TM]


def _even_out(o_fwd, o_bwd, onw, pb, xs, mod, vnw, ws, bsb, w_out, a_width):
    b, tt, _ = pb.shape
    d = w_out.shape[1]
    bw = d - a_width
    groups = ws.shape[0]
    ucol = 3 * a_width // bw
    x_specs, x_args = _stream_specs(xs, d)
    return pl.pallas_call(
        functools.partial(_even_out_kernel, groups=groups, n_x=len(x_args)),
        grid=(b // NB, tt // TM),
        in_specs=[
            _rows_spec(a_width),
            _rows_spec(a_width),
            _rows_spec(a_width, 2),
            _rows_spec(bw, ucol),
            _rows_spec(bw, ucol + 1)] + x_specs + [
            _mod_spec(d, NB),
            _const_spec((1, HGRN_HEAD_DIM)),
            _const_spec((1, bw)),
            _const_spec(ws.shape),
            _const_spec(bsb.shape),
            _const_spec((a_width, d)),
            _const_spec((bw, d)),
        ],
        out_specs=_rows_spec(d),
        out_shape=jax.ShapeDtypeStruct((b, tt, d), F32),
        compiler_params=_cparams(("parallel", "parallel"), V7X_VMEM_LIMIT),
        name="even_out",
    )(o_fwd, o_bwd, pb, pb, pb, *x_args, mod, onw.reshape(1, HGRN_HEAD_DIM), vnw, ws, bsb,
      w_out[:a_width], w_out[a_width:])


def _odd_in_kernel(x_ref, mod_ref, nw_ref, win_ref, qaw_ref, kvaw_ref, wq_ref, qnw_ref, wkv_ref, knw_ref,
                   cq_ref, sq_ref, ck_ref, sk_ref, bc_ref,
                   xc_ctx_ref, xc_lat_ref, qt_ref, k_ref, vt_ref, *, heads, cw, ql, kvl):
    cq, sq, ck, sk = cq_ref[...], sq_ref[...], ck_ref[...], sk_ref[...]
    rope = MLA_ROPE
    o = cw + ql + kvl
    hw = heads * 128
    ones_pad = (lax.broadcasted_iota(jnp.int32, (V_PAD, TM), 0) == 0).astype(F32)
    is_ctx = pl.program_id(1) == 0

    def project(s):
        h = _modulate(x_ref[s], nw_ref[...], mod_ref[s, 0:1, :], mod_ref[s, 1:2, :])
        return _dot(h.astype(BF16), win_ref[...])

    def latents(s, p):
        xc = _dot(p[:, :cw].astype(BF16), bc_ref[...])
        xc_lat_ref[s] = xc.astype(xc_lat_ref.dtype)

        @pl.when(is_ctx)
        def _():
            xc_ctx_ref[s] = xc.astype(xc_ctx_ref.dtype)
        q_lat = (_rms(p[:, cw:cw + ql]) * qaw_ref[...]).astype(BF16)
        kv_lat = (_rms(p[:, cw + ql:cw + ql + kvl]) * kvaw_ref[...]).astype(BF16)
        kp, kps = p[:, o:o + 128], p[:, o + 128:o + 256]
        kpr = lax.rsqrt(jnp.sum(kp * kp, axis=-1, keepdims=True) * (1.0 / rope) + EPS)
        k_pe = (kp * ck + kps * sk) * kpr
        qf = _dot(q_lat, wq_ref[...])
        kvf = _dot(kv_lat, wkv_ref[...])
        return qf, kvf, k_pe

    def heads_out(s, qf, kvf, k_pe):
        for hh in range(heads):
            cs = slice(hh * 128, (hh + 1) * 128)
            qn = _rms(qf[:, cs]) * qnw_ref[...]
            qp = qf[:, hw + hh * 128:hw + (hh + 1) * 128]
            qps = qf[:, 2 * hw + hh * 128:2 * hw + (hh + 1) * 128]
            qpr = lax.rsqrt(jnp.sum(qp * qp, axis=-1, keepdims=True) * (1.0 / rope) + EPS)
            q_pe = (qp * cq + qps * sq) * qpr
            qt_ref[s, hh] = jnp.concatenate([qn, q_pe], axis=1).T.astype(qt_ref.dtype)
            kn = _rms(kvf[:, cs]) * knw_ref[...]
            k_ref[s, hh] = jnp.concatenate([kn, k_pe], axis=1).astype(k_ref.dtype)
            vt_ref[s, hh] = jnp.concatenate([kvf[:, hw + hh * 128:hw + (hh + 1) * 128].T, ones_pad],
                                            axis=0).astype(vt_ref.dtype)

    mid = latents(0, project(0))
    for s in range(1, NB):
        p_next = project(s)
        heads_out(s - 1, *mid)
        mid = latents(s, p_next)
    heads_out(NB - 1, *mid)


def _odd_in(x, mod, nw, wts, heads, cw, ql, kvl):
    b, tt, d = x.shape
    consts = [wts[k] for k in ("w_in", "qa_w", "kva_w", "w_q", "qn_w", "w_kv", "kn_w")]
    tables = [wts[k] for k in ("cq", "sq", "ck", "sk")]
    row128 = pl.BlockSpec((TM, 128), lambda b, i: (i, 0))
    in_specs = ([_rows_spec(d), _mod_spec(d, NB), _const_spec((1, d))]
                + [_const_spec(a.shape) for a in consts] + [row128] * 4 + [_const_spec(wts["bc"].shape)])
    return pl.pallas_call(
        functools.partial(_odd_in_kernel, heads=heads, cw=cw, ql=ql, kvl=kvl),
        grid=(b // NB, tt // TM),
        in_specs=in_specs,
        out_specs=[
            pl.BlockSpec((NB, TM, 2 * cw), lambda b, i: (b, 0, 0)),
            pl.BlockSpec((NB, TM, 2 * cw), lambda b, i: (b, jnp.maximum(i - 1, 0), 0)),
            pl.BlockSpec((NB, heads, 256, TM), lambda b, i: (b, 0, 0, i)),
            pl.BlockSpec((NB, heads, TM, 256), lambda b, i: (b, 0, i, 0)),
            pl.BlockSpec((NB, heads, MLA_V_DIM + V_PAD, TM), lambda b, i: (b, 0, 0, i)),
        ],
        out_shape=[
            jax.ShapeDtypeStruct((b, TM, 2 * cw), BF16),
            jax.ShapeDtypeStruct((b, tt - TM, 2 * cw), F32),
            jax.ShapeDtypeStruct((b, heads, 256, tt), BF16),
            jax.ShapeDtypeStruct((b, heads, tt, 256), BF16),
            jax.ShapeDtypeStruct((b, heads, MLA_V_DIM + V_PAD, tt), BF16),
        ],
        compiler_params=_cparams(("parallel", "arbitrary"), V7X_VMEM_LIMIT),
        name="odd_in",
    )(x, mod, nw, *consts, *tables, wts["bc"])


def _attn_kernel(qt_ref, k_ref, vt_ref, o_ref, sa_ref, sb_ref, ma_ref, mb_ref, *, ctx_len, n_tiles):
    i = pl.program_id(2)
    n_all = k_ref.shape[1]
    nh = k_ref.shape[0]
    bufs = ((sa_ref, ma_ref), (sb_ref, mb_ref))

    def chunks(n_keys):
        return [(st, min(ATTN_KC, n_keys - st)) for st in range(0, n_keys, ATTN_KC)]

    def run(step_parity, keys1, keys2):
        s_w, m_w = bufs[step_parity]
        s_r, m_r = bufs[1 - step_parity]
        c1 = chunks(keys1) if keys1 else []
        c2 = chunks(keys2) if keys2 else []
        if keys1:
            qts = [qt_ref[hh] for hh in range(nh)]
        if keys2:
            m_prev = [m_r[hh] for hh in range(nh)]
        m, acc = [None] * nh, [None] * nh
        for idx in range(max(len(c1), len(c2))):
            for hh in range(nh):
                if idx < len(c1):
                    st, sz = c1[idx]
                    s = _dot(k_ref[hh, st:st + sz, :], qts[hh])
                    s_w[hh, st:st + sz, :] = s
                    cm = jnp.max(s, axis=0, keepdims=True)
                    m[hh] = cm if m[hh] is None else jnp.maximum(m[hh], cm)
                if idx < len(c2):
                    st, sz = c2[idx]
                    p = jnp.exp2(s_r[hh, st:st + sz, :] - m_prev[hh]).astype(BF16)
                    ca = _dot(vt_ref[hh, :, st:st + sz], p)
                    acc[hh] = ca if acc[hh] is None else acc[hh] + ca
        dv = o_ref.shape[1] // nh
        for hh in range(nh):
            if keys1:
                m_w[hh] = m[hh]
            if keys2:
                o_ref[:, hh * dv:(hh + 1) * dv] = (acc[hh][:dv] * (1.0 / acc[hh][dv:dv + 1])).T.astype(o_ref.dtype)

    @pl.when(i == 0)
    def _():
        run(0, ctx_len, None)

    @pl.when(i == 1)
    def _():
        run(1, n_all, ctx_len)

    for parity in (0, 1):
        @pl.when(jnp.logical_and(jnp.logical_and(i >= 2, i < n_tiles), i % 2 == parity))
        def _():
            run(parity, n_all, n_all)

    @pl.when(i == n_tiles)
    def _():
        run(n_tiles % 2, None, n_all)


def _attention(qt, k, vt, ctx_len):
    b, heads, tt, dq = k.shape
    dvp = vt.shape[2]
    dv = dvp - V_PAD
    n_tiles = tt // TM
    nh = ATTN_HEADS_PER_STEP
    assert heads % nh == 0
    return pl.pallas_call(
        functools.partial(_attn_kernel, ctx_len=ctx_len, n_tiles=n_tiles),
        grid=(b, heads // nh, n_tiles + 1),
        in_specs=[
            pl.BlockSpec((None, nh, dq, TM), lambda b, h, i: (b, h, 0, jnp.minimum(i, n_tiles - 1))),
            pl.BlockSpec((None, nh, tt, dq), lambda b, h, i: (b, h, 0, 0)),
            pl.BlockSpec((None, nh, dvp, tt), lambda b, h, i: (b, h, 0, 0)),
        ],
        out_specs=pl.BlockSpec((None, TM, nh * dv), lambda b, h, i: (b, jnp.maximum(i - 1, 0), h)),
        out_shape=jax.ShapeDtypeStruct((b, tt, heads * dv), BF16),
        scratch_shapes=[pltpu.VMEM((nh, tt, TM), F32), pltpu.VMEM((nh, tt, TM), F32),
                        pltpu.VMEM((nh, 1, TM), F32), pltpu.VMEM((nh, 1, TM), F32)],
        compiler_params=_cparams(("parallel", "parallel", "arbitrary"), V7X_VMEM_LIMIT),
        name="attention",
    )(qt, k, vt)


def _dft_kernel(c_ref, s_ref, x_ref, o_ref):
    half = x_ref.shape[1] // 2
    y = _dot(c_ref[...], x_ref[:, :half]) + _dot(s_ref[...], x_ref[:, half:])
    o_ref[...] = y.astype(o_ref.dtype)


def _dft_dense(xc, t_cos, t_sin):
    b, t, w2 = xc.shape
    return pl.pallas_call(
        _dft_kernel,
        grid=(b,),
        in_specs=[_const_spec((t, t)), _const_spec((t, t)), pl.BlockSpec((None, t, w2), lambda b: (b, 0, 0))],
        out_specs=pl.BlockSpec((None, t, w2 // 2), lambda b: (b, 0, 0)),
        out_shape=jax.ShapeDtypeStruct((b, t, w2 // 2), BF16),
        compiler_params=_cparams(("parallel",), V7X_VMEM_LIMIT),
        name="dft_dense",
    )(t_cos, t_sin, xc)


def _dft_split(t):
    n2 = 1 << ((t.bit_length()) // 2)
    return t // n2, n2


def _swap_major(x):
    return jnp.transpose(x, (1, 0, 2))


def _dft_two_stage_kernel(xr_ref, xi_ref, g_ref, cs2_ref, o_ref, zr_ref, zi_ref, y_ref, *, n1, n2):
    lw = xr_ref.shape[1]
    xr = _swap_major(xr_ref[...].reshape(n1, n2, lw))
    xi = _swap_major(xi_ref[...].reshape(n1, n2, lw))
    for t2 in range(n2):
        p = _dot(g_ref[t2], jnp.concatenate([xr[t2], xi[t2]], axis=1).astype(BF16))
        zr_ref[t2] = p[:n1, :lw] - p[n1:, lw:]
        zi_ref[t2] = p[:n1, lw:] + p[n1:, :lw]
    zr = _swap_major(zr_ref[...])
    zi = _swap_major(zi_ref[...])
    for u1 in range(n1):
        y_ref[u1] = _dot(cs2_ref[...], jnp.concatenate([zr[u1], zi[u1]], axis=0).astype(BF16))
    o_ref[...] = _swap_major(y_ref[...]).reshape(n1 * n2, lw).astype(o_ref.dtype)


def _dft_two_stage(xc, tables):
    g, cs2 = tables
    b, t, w2 = xc.shape
    cw = w2 // 2
    lw = 128
    n1, n2 = _dft_split(t)
    return pl.pallas_call(
        functools.partial(_dft_two_stage_kernel, n1=n1, n2=n2),
        grid=(b, cw // lw),
        in_specs=[pl.BlockSpec((None, t, lw), lambda b, j: (b, 0, j)),
                  pl.BlockSpec((None, t, lw), lambda b, j: (b, 0, cw // lw + j)),
                  _const_spec(g.shape), _const_spec(cs2.shape)],
        out_specs=pl.BlockSpec((None, t, lw), lambda b, j: (b, 0, j)),
        out_shape=jax.ShapeDtypeStruct((b, t, cw), BF16),
        scratch_shapes=[pltpu.VMEM((n2, n1, lw), F32), pltpu.VMEM((n2, n1, lw), F32), pltpu.VMEM((n1, n2, lw), F32)],
        compiler_params=_cparams(("parallel", "parallel"), V7X_VMEM_LIMIT),
        name="dft_two_stage",
    )(xc, xc, g, cs2)


def _odd_out_kernel(fc_ref, fl_ref, at_ref, x_ref, mod_ref, wf_ref, wa_ref, o_ref):
    is_ctx = pl.program_id(1) == 0
    fm = jnp.concatenate([jnp.where(is_ctx, fc_ref[s], fl_ref[s]) for s in range(NB)], axis=0)
    at = jnp.concatenate([at_ref[s] for s in range(NB)], axis=0)
    y = _dot(fm, wf_ref[...]) + _dot(at, wa_ref[...])
    for s in range(NB):
        o_ref[s] = x_ref[s] + mod_ref[s, 2:3, :] * y[s * TM:(s + 1) * TM]


def _odd_out(fm_ctx, fm_lat, attn, x, mod, w_out):
    b, tt, d = x.shape
    cw = fm_ctx.shape[-1]
    aw = attn.shape[-1]
    return pl.pallas_call(
        _odd_out_kernel,
        grid=(b // NB, tt // TM),
        in_specs=[
            pl.BlockSpec((NB, TM, cw), lambda b, i: (b, 0, 0)),
            pl.BlockSpec((NB, TM, cw), lambda b, i: (b, jnp.maximum(i - 1, 0), 0)),
            _rows_spec(aw),
            _rows_spec(d),
            _mod_spec(d, NB),
            _const_spec((cw, d)),
            _const_spec((aw, d)),
        ],
        out_specs=_rows_spec(d),
        out_shape=jax.ShapeDtypeStruct((b, tt, d), F32),
        compiler_params=_cparams(("parallel", "parallel"), V7X_VMEM_LIMIT),
        name="odd_out",
    )(fm_ctx, fm_lat, attn, x, mod, w_out[:cw], w_out[cw:])


def _ffn_kernel(xp_ref, x_ref, xn_ref, mod_ref, nw_ref, wg_ref, wv_ref, cw_ref, cb_ref, wd_ref, o_ref,
                *, first_tile, n_tiles):
    i = pl.program_id(1) + first_tile
    ext = TM + 2 * HALO
    row = lax.broadcasted_iota(jnp.int32, (ext, 1), 0)
    keep = jnp.logical_and(jnp.logical_or(row >= HALO, i > 1),
                           jnp.logical_or(row < HALO + TM, jnp.logical_and(i > 0, i < n_tiles - 1)))
    nw = nw_ref[...]
    hes = []
    for s in range(NB):
        xe = jnp.concatenate([xp_ref[s], x_ref[s], xn_ref[s]], axis=0)
        hes.append(jnp.where(keep, _modulate(xe, nw, mod_ref[s, 3:4, :], mod_ref[s, 4:5, :]), 0.0))
    ge = _dot(jnp.concatenate(hes, axis=0).astype(BF16), wg_ref[...])
    val = _dot(jnp.concatenate([h[HALO:HALO + TM] for h in hes], axis=0).astype(BF16), wv_ref[...])
    cw0, cw1, cw2 = cw_ref[0:1, :], cw_ref[1:2, :], cw_ref[2:3, :]
    gc = jnp.concatenate(
        [ge[s * ext + HALO - 1:s * ext + HALO - 1 + TM] * cw0 + ge[s * ext + HALO:s * ext + HALO + TM] * cw1
         + ge[s * ext + HALO + 1:s * ext + HALO + 1 + TM] * cw2 for s in range(NB)], axis=0) + cb_ref[...]
    y = _dot((_silu(gc) * val).astype(BF16), wd_ref[...])
    for s in range(NB):
        o_ref[s] = x_ref[s] + mod_ref[s, 5:6, :] * y[s * TM:(s + 1) * TM]


def _conv_ffn(x, mod, nw, w_up, conv_w, conv_b, wd, skip_ctx):
    b, tt, d = x.shape
    ff = wd.shape[0]
    n_tiles = tt // TM
    first = 1 if skip_ctx else 0
    r = TM // HALO
    nblk = tt // HALO
    return pl.pallas_call(
        functools.partial(_ffn_kernel, first_tile=first, n_tiles=n_tiles),
        grid=(b // NB, n_tiles - first),
        in_specs=[
            pl.BlockSpec((NB, HALO, d), lambda b, i: (b, jnp.maximum((i + first) * r - 1, 0), 0)),
            pl.BlockSpec((NB, TM, d), lambda b, i: (b, i + first, 0)),
            pl.BlockSpec((NB, HALO, d), lambda b, i: (b, jnp.minimum((i + first + 1) * r, nblk - 1), 0)),
            pl.BlockSpec((None, NB, 6, d), lambda b, i: (jnp.minimum(i + first, 1), b, 0, 0)),
            _const_spec((1, d)),
            pl.BlockSpec((d, ff), lambda b, i: (0, 0), pipeline_mode=pl.Buffered(1)),
            pl.BlockSpec((d, ff), lambda b, i: (0, 1), pipeline_mode=pl.Buffered(1)),
            _const_spec((CONV_W, ff)),
            _const_spec((1, ff)),
            _const_spec((ff, d)),
        ],
        out_specs=_rows_spec(d),
        out_shape=jax.ShapeDtypeStruct((b, tt - first * TM, d), F32),
        compiler_params=_cparams(("parallel", "parallel"), V7X_VMEM_LIMIT),
        name="conv_ffn",
    )(x, x, x, mod, nw, w_up, w_up, conv_w, conv_b, wd)


def _rope_swap_perm():
    q = MLA_ROPE // 4
    return np.concatenate([np.arange(q, 2 * q), np.arange(0, q), np.arange(3 * q, 4 * q), np.arange(2 * q, 3 * q)])


def _pad128(v):
    return jnp.pad(v, (0, 128 - v.shape[0])).reshape(1, 128)


def _odd_weights(w_in, qa_w, w_qb, kva_w, w_kvb, qn_w, kn_w, heads, cw, ql, kvl, cos_t, sin_t):
    d = w_in.shape[0]
    perm = _rope_swap_perm()
    nope, rope, qk = MLA_NOPE, MLA_ROPE, MLA_NOPE + MLA_ROPE
    o = cw + ql + kvl
    z = jnp.zeros((d, 128 - rope), w_in.dtype)
    kpe = w_in[:, o:o + rope]
    w_in_ext = jnp.concatenate([w_in[:, :o], kpe, z, kpe[:, perm], z], axis=1).astype(BF16)
    wq = w_qb.reshape(ql, heads, qk)
    zq = jnp.zeros((ql, heads, 128 - rope), w_qb.dtype)
    wq_rope = wq[:, :, nope:]
    w_q = jnp.concatenate([
        wq[:, :, :nope].reshape(ql, heads * 128),
        jnp.concatenate([wq_rope, zq], axis=2).reshape(ql, heads * 128),
        jnp.concatenate([wq_rope[:, :, perm], zq], axis=2).reshape(ql, heads * 128),
    ], axis=1).astype(BF16)
    wkv = w_kvb.reshape(kvl, heads, nope + MLA_V_DIM)
    w_kv = jnp.concatenate([wkv[:, :, :nope].reshape(kvl, heads * nope),
                            wkv[:, :, nope:].reshape(kvl, heads * MLA_V_DIM)], axis=1).astype(BF16)
    gd = cw // FOURIER_GROUPS
    jk = (np.arange(gd)[:, None] * np.arange(gd)[None, :]) % gd
    ang = 2.0 * np.pi * jk / gd
    eye = np.eye(FOURIER_GROUPS)
    bc = np.concatenate([np.kron(eye, np.cos(ang)), np.kron(eye, -np.sin(ang))], axis=1) / np.sqrt(gd)
    scale = float(qk) ** -0.5 * LOG2_E
    return {
        "w_in": w_in_ext, "qa_w": qa_w.reshape(1, ql), "kva_w": kva_w.reshape(1, kvl),
        "w_q": w_q, "qn_w": (qn_w[:nope] * scale).reshape(1, nope),
        "w_kv": w_kv, "kn_w": kn_w[:nope].reshape(1, nope),
        "cq": cos_t * (_pad128(qn_w[nope:]) * scale), "sq": sin_t * (_pad128(qn_w[nope:][perm]) * scale),
        "ck": cos_t * _pad128(kn_w[nope:]), "sk": sin_t * _pad128(kn_w[nope:][perm]),
        "bc": jnp.asarray(bc, F32).astype(BF16),
    }


def _rope_tables(t_lat, ctx_len):
    rows = t_lat // GRID_W
    row = jnp.repeat(jnp.arange(rows), GRID_W)
    col = jnp.tile(jnp.arange(GRID_W), rows)
    r_axis = MLA_ROPE // 2
    inv_freq = ROPE_THETA ** (-jnp.arange(0, r_axis, 2, dtype=F32) / r_axis)
    ang = jnp.stack([row, col], axis=-1).astype(F32)[:, :, None] * inv_freq
    cos, sin = jnp.cos(ang), jnp.sin(ang)
    cos64 = jnp.concatenate([cos[:, 0], cos[:, 0], cos[:, 1], cos[:, 1]], axis=-1)
    sin64 = jnp.concatenate([-sin[:, 0], sin[:, 0], -sin[:, 1], sin[:, 1]], axis=-1)
    cos64 = jnp.concatenate([jnp.ones((ctx_len, MLA_ROPE), F32), cos64], axis=0)
    sin64 = jnp.concatenate([jnp.zeros((ctx_len, MLA_ROPE), F32), sin64], axis=0)
    pad = ((0, 0), (0, 128 - MLA_ROPE))
    return jnp.pad(cos64, pad), jnp.pad(sin64, pad)


def _cos_sin(phase, period, scale):
    ang = (phase % period).astype(F32) * (2.0 * np.pi / period)
    return jnp.cos(ang) * scale, jnp.sin(ang) * scale


def _dft_dense_tables(t):
    idx = jnp.arange(t, dtype=jnp.int32)
    c, s = _cos_sin(idx[:, None] * idx[None, :], t, 1.0 / np.sqrt(t))
    return c.astype(BF16), s.astype(BF16)


def _dft_two_stage_tables(t):
    n1, n2 = _dft_split(t)
    u1 = jnp.arange(n1, dtype=jnp.int32)
    pos = jnp.arange(t, dtype=jnp.int32).reshape(n1, n2)
    c, s = _cos_sin(pos.T[:, None, :] * u1[None, :, None], t, 1.0 / np.sqrt(n1))
    g = jnp.concatenate([c, -s], axis=1).astype(BF16)
    i2 = jnp.arange(n2, dtype=jnp.int32)
    c2, s2 = _cos_sin(i2[:, None] * i2[None, :], n2, 1.0 / np.sqrt(n2))
    return g, jnp.concatenate([c2, s2], axis=1).astype(BF16)


def kernel(x, c, ctx, c_ctx, ada_w, ada_b, norm_mix_w, norm_ffn_w, ev_w_in, ev_lb_logits, ev_onorm_w, ev_vnorm_w, ev_ws, ev_bs, ev_w_out, od_w_in, od_qa_norm_w, od_w_qb, od_kva_norm_w, od_w_kvb, od_q_norm_w, od_k_norm_w, od_w_out, ffn_w_up, ffn_conv_w, ffn_conv_b, ffn_w_down):
    bsz, t_lat, d = x.shape
    ctx_len = ctx.shape[1]
    depth = ada_w.shape[0]
    assert ctx_len == TM and t_lat % ATTN_KC == 0 and t_lat % GRID_W == 0 and bsz % NB == 0
    a_width = ev_lb_logits.shape[-1]
    ql, kvl = od_qa_norm_w.shape[-1], od_kva_norm_w.shape[-1]
    cw = od_w_in.shape[-1] - ql - kvl - MLA_ROPE
    heads = od_w_qb.shape[-1] // (MLA_NOPE + MLA_ROPE)
    d_ff = ffn_w_down.shape[1]

    pad_rows = (-(bsz + 1)) % 8
    cvec = jnp.concatenate([c, c_ctx[None, :], jnp.zeros((pad_rows, d), F32)], axis=0)
    mods = _ada_tables(cvec, ada_w, ada_b)
    mod_lat = mods[:, :bsz].reshape(depth, 1, bsz, 6, d)
    mod_ctx = jnp.broadcast_to(mods[:, bsz].reshape(depth, 1, 1, 6, d), (depth, 1, bsz, 6, d))
    mods = jnp.concatenate([mod_ctx, mod_lat], axis=1)

    lb_p = jax.nn.softmax(ev_lb_logits.astype(F32), axis=0)
    lbs = jnp.cumsum(lb_p, axis=0) - lb_p[0]
    cos_t, sin_t = _rope_tables(t_lat, ctx_len)
    dft_lat = dft_ctx = None

    xs = (ctx, x)
    for l in range(depth):
        last = l == depth - 1
        mod = mods[l]
        nmw = norm_mix_w[l].reshape(1, d)
        if l % 2 == 0:
            e = l // 2
            pz, pb = _even_in(xs, ctx_len + t_lat, mod, nmw, ev_w_in[e].astype(BF16), a_width)
            o_fwd, o_bwd = _hgrn_scan(pz, pb, lbs[e], ctx_len, a_width)
            bw = d - a_width
            gd = bw // ev_ws.shape[1]
            bsb = jnp.repeat(ev_bs[e].T, gd, axis=1)
            xs = _even_out(o_fwd, o_bwd, ev_onorm_w[e], pb, xs, mod, ev_vnorm_w[e].reshape(1, bw),
                           ev_ws[e].astype(BF16), bsb, ev_w_out[e].astype(BF16), a_width)
        else:
            o = l // 2
            if dft_lat is None:
                dft_lat, dft_ctx = _dft_two_stage_tables(t_lat), _dft_dense_tables(ctx_len)
            wts = _odd_weights(od_w_in[o], od_qa_norm_w[o], od_w_qb[o], od_kva_norm_w[o], od_w_kvb[o],
                               od_q_norm_w[o], od_k_norm_w[o], heads, cw, ql, kvl, cos_t, sin_t)
            xc_ctx, xc_lat, qt, k, vt = _odd_in(xs, mod, nmw, wts, heads, cw, ql, kvl)
            attn = _attention(qt, k, vt, ctx_len)
            fm_ctx = _dft_dense(xc_ctx, *dft_ctx)
            fm_lat = _dft_two_stage(xc_lat, dft_lat)
            xs = _odd_out(fm_ctx, fm_lat, attn, xs, mod, od_w_out[o].astype(BF16))
        xs = _conv_ffn(xs, mod, norm_ffn_w[l].reshape(1, d), ffn_w_up[l].astype(BF16), ffn_conv_w[l],
                       ffn_conv_b[l].reshape(1, d_ff), ffn_w_down[l].astype(BF16), skip_ctx=last)
    return xs
```

```python
import functools

import numpy as np
import jax
import jax.numpy as jnp
from jax import lax
from jax.experimental import pallas as pl
from jax.experimental.pallas import tpu as pltpu

F32 = jnp.float32
BF16 = jnp.bfloat16

EPS = 1e-6
F_MIN = 1e-6
LOG2_E = float(np.log2(np.e))
GRID_W = 64
ROPE_THETA = 10000.0
HGRN_HEAD_DIM = 128
HGRN_CHUNK = 64
HGRN_STEP_CHUNKS = 4
MLP_CHUNK = 128
MLA_NOPE = 128
MLA_ROPE = 64
MLA_V_DIM = 128
V_PAD = 16
FOURIER_GROUPS = 4
CONV_W = 3

TM = 256
NB = 2
HALO = 8
ATTN_KC = 512
ATTN_HEADS_PER_STEP = 3
ADA_TN = 1536
X_RING = 3
V7X_VMEM_LIMIT = 56 * 1024 * 1024


def _cparams(sem, vmem=None):
    return pltpu.CompilerParams(dimension_semantics=sem, vmem_limit_bytes=vmem)


def _dot(a, b):
    return jnp.dot(a, b, preferred_element_type=F32)


def _dot_nt(a, b):
    return lax.dot_general(a, b, (((1,), (1,)), ((), ())), preferred_element_type=F32)


def _dot_tn(a, b):
    return lax.dot_general(a, b, (((0,), (0,)), ((), ())), preferred_element_type=F32)


def _silu(x):
    return x * jax.nn.sigmoid(x)


def _gelu_tanh(x):
    c = 0.7978845608028654
    return x * (0.5 + 0.5 * jnp.tanh(x * (c + (c * 0.044715) * (x * x))))


def _rms(x):
    return x * lax.rsqrt(jnp.sum(x * x, axis=-1, keepdims=True) * (1.0 / x.shape[-1]) + EPS)


def _modulate(x, nw, shift, scale):
    return (_rms(x) * nw) * (1.0 + scale) + shift


def _mod_spec(d, nb):
    return pl.BlockSpec((None, nb, 6, d), lambda b, i: (jnp.minimum(i, 1), b, 0, 0))


def _rows_spec(cols, colblk=0):
    return pl.BlockSpec((NB, TM, cols), lambda b, i: (b, i, colblk))


def _const_spec(shape):
    nd = len(shape)
    return pl.BlockSpec(shape, lambda *_: (0,) * nd, pipeline_mode=pl.Buffered(1))


def _ada_kernel(c_ref, w_ref, b_ref, o_ref):
    s = _silu(c_ref[...]).astype(BF16)
    o_ref[...] = _dot(s, w_ref[...].astype(BF16)) + b_ref[...]


def _ada_tables(cvec, ada_w, ada_b):
    depth, d, n = ada_w.shape
    rows = cvec.shape[0]
    tn = ADA_TN
    return pl.pallas_call(
        _ada_kernel,
        grid=(depth, n // tn),
        in_specs=[
            pl.BlockSpec((rows, d), lambda l, j: (0, 0)),
            pl.BlockSpec((None, d, tn), lambda l, j: (l, 0, j)),
            pl.BlockSpec((None, 1, tn), lambda l, j: (l, 0, j)),
        ],
        out_specs=pl.BlockSpec((None, rows, tn), lambda l, j: (l, 0, j)),
        out_shape=jax.ShapeDtypeStruct((depth, rows, n), F32),
        compiler_params=_cparams(("parallel", "parallel"), V7X_VMEM_LIMIT),
        name="ada_tables",
    )(cvec, ada_w, ada_b.reshape(depth, 1, n))


def _stream_specs(xs, d):
    if isinstance(xs, tuple):
        return ([pl.BlockSpec((NB, TM, d), lambda b, i: (b, 0, 0)),
                 pl.BlockSpec((NB, TM, d), lambda b, i: (b, jnp.maximum(i - 1, 0), 0))], list(xs))
    return [_rows_spec(d)], [xs]


def _stream_rows(x_refs, s):
    if len(x_refs) == 1:
        return x_refs[0][s]
    return jnp.where(pl.program_id(1) == 0, x_refs[0][s], x_refs[1][s])


def _even_in_kernel(*refs, n_x, a_width):
    x_refs = refs[:n_x]
    mod_ref, nw_ref, w_ref, oz_ref, ob_ref = refs[n_x:]
    nw = nw_ref[...]
    h = jnp.concatenate([_modulate(_stream_rows(x_refs, s), nw, mod_ref[s, 0:1, :], mod_ref[s, 1:2, :]).astype(BF16)
                         for s in range(NB)], axis=0)
    p = _dot(h, w_ref[...])
    z0, z1 = a_width * 2, a_width * 4
    for s in range(NB):
        rows = slice(s * TM, (s + 1) * TM)
        oz_ref[s] = p[rows, z0:z1]
        ob_ref[s] = jnp.concatenate([p[rows, :z0], p[rows, z1:]], axis=1).astype(ob_ref.dtype)


def _even_in(xs, tt, mod, nw, w_in, a_width):
    d, n = w_in.shape
    x_specs, x_args = _stream_specs(xs, d)
    b = x_args[0].shape[0]
    return pl.pallas_call(
        functools.partial(_even_in_kernel, n_x=len(x_args), a_width=a_width),
        grid=(b // NB, tt // TM),
        in_specs=x_specs + [
            _mod_spec(d, NB),
            _const_spec((1, d)),
            _const_spec((d, n)),
        ],
        out_specs=[_rows_spec(2 * a_width), _rows_spec(n - 2 * a_width)],
        out_shape=[jax.ShapeDtypeStruct((b, tt, 2 * a_width), F32),
                   jax.ShapeDtypeStruct((b, tt, n - 2 * a_width), BF16)],
        compiler_params=_cparams(("parallel", "parallel"), V7X_VMEM_LIMIT),
        name="even_in",
    )(*x_args, mod, nw, w_in)


def _hgrn_consts(reverse):
    c = HGRN_CHUNK
    idx = np.arange(c)
    if not reverse:
        tri = (idx[None, :] <= idx[:, None]).astype(np.float32)
    else:
        tri = (idx[None, :] >= idx[:, None]).astype(np.float32)
    masks = []
    for w in (32, 16, 8):
        grp = idx // (2 * w)
        qrow = (idx % (2 * w) >= w) if not reverse else (idx % (2 * w) < w)
        masks.append((grp[:, None] == grp[None, :]) & qrow[:, None] & (~qrow)[None, :])
    blk = idx // 8
    causal = (idx[None, :] <= idx[:, None]) if not reverse else (idx[None, :] >= idx[:, None])
    masks.append((blk[:, None] == blk[None, :]) & causal)
    return np.kron(np.eye(HGRN_STEP_CHUNKS), tri).astype(np.float32), np.stack(masks).astype(np.float32)


def _hgrn_prepare(pq_ref, pv_ref, pz_ref, lb, tri_ref, width):
    f = jnp.maximum(lb + (1.0 - lb) * jax.nn.sigmoid(pz_ref[...]), F_MIN)
    g = jnp.log(f) * LOG2_E
    g1 = g.astype(BF16)
    r1 = g - g1.astype(F32)
    g2 = r1.astype(BF16)
    b3 = _dot(tri_ref[...], jnp.concatenate([g1, g2], axis=1))
    b_all = b3[:, :width] + b3[:, width:]
    return _silu(pq_ref[...].astype(F32)).astype(BF16), (1.0 - f).astype(BF16), pv_ref[...], b_all


def _hgrn_chunk(feats, ci, reverse, masks_ref, st, heads):
    c, hd = HGRN_CHUNK, HGRN_HEAD_DIM
    width = heads * hd
    q, kk, v, b_all = feats
    rs = slice(ci * c, (ci + 1) * c)
    last = c - 1 if not reverse else 0
    parts = []
    for h in range(heads):
        sl = slice(h * hd, (h + 1) * hd)
        qh, kh, b = q[rs, sl], kk[rs, sl], b_all[rs, sl]

        def ref_rows(rows, n):
            return jnp.concatenate([jnp.broadcast_to(b[r:r + 1, :], (n, hd)) for r in rows], axis=0)

        a = None
        for lv, w in enumerate((32, 16, 8)):
            mids = [gi * 2 * w + (w - 1 if not reverse else w) for gi in range(c // (2 * w))]
            ew = jnp.exp2(-jnp.abs(b - ref_rows(mids, 2 * w))).astype(BF16)
            t = _dot_nt(qh * ew, kh * ew) * masks_ref[lv]
            a = t if a is None else a + t
        anchor = ref_rows([8 * m + (3 if not reverse else 4) for m in range(c // 8)], 8)
        d8 = _dot_nt(qh * jnp.exp2(b - anchor).astype(BF16), kh * jnp.exp2(anchor - b).astype(BF16))
        a = a + jnp.where(masks_ref[3] > 0.5, d8, 0.0)
        vb = v[rs, sl]
        e_in = jnp.exp2(b)
        qhat = qh * e_in.astype(BF16)
        khat = kh * jnp.exp2(ref_rows([last], c) - b).astype(BF16)
        parts.append((a.astype(BF16), vb, qhat, _dot_tn(vb, khat), e_in[last:last + 1, :]))
    heads_out = []
    for h, (a, vb, qhat, upd, e_last) in enumerate(parts):
        heads_out.append(_dot(a, vb) + _dot_nt(qhat, st[h].astype(BF16)))
        st[h] = st[h] * e_last + upd
    return jnp.concatenate(heads_out, axis=1)


def _hgrn_kernel(fq_ref, fv_ref, fz_ref, bq_ref, bv_ref, bz_ref, lb_ref, ftri_ref, btri_ref, fmask_ref, bmask_ref,
                 of_ref, ob_ref, fst_ref, bst_ref, *, heads):
    nck = HGRN_STEP_CHUNKS
    width = heads * HGRN_HEAD_DIM

    @pl.when(pl.program_id(1) == 0)
    def _():
        fst_ref[...] = jnp.zeros_like(fst_ref)
        bst_ref[...] = jnp.zeros_like(bst_ref)

    ffeats = _hgrn_prepare(fq_ref, fv_ref, fz_ref, lb_ref[0:1, :], ftri_ref, width)
    bfeats = _hgrn_prepare(bq_ref, bv_ref, bz_ref, lb_ref[1:2, :], btri_ref, width)
    fst = [fst_ref[h] for h in range(heads)]
    bst = [bst_ref[h] for h in range(heads)]
    fouts, bouts = [None] * nck, [None] * nck
    for k in range(nck):
        fouts[k] = _hgrn_chunk(ffeats, k, False, fmask_ref, fst, heads)
        bouts[nck - 1 - k] = _hgrn_chunk(bfeats, nck - 1 - k, True, bmask_ref, bst, heads)
    for h in range(heads):
        fst_ref[h] = fst[h]
        bst_ref[h] = bst[h]
    of_ref[...] = jnp.concatenate(fouts, axis=0).astype(of_ref.dtype)
    ob_ref[...] = jnp.concatenate(bouts, axis=0).astype(ob_ref.dtype)


def _hgrn_scan(pz, pb, lbs, ctx_len, width):
    b, tt, _ = pz.shape
    rows = HGRN_CHUNK * HGRN_STEP_CHUNKS
    heads = width // HGRN_HEAD_DIM
    nb, nbc = tt // rows, ctx_len // rows
    ftri, fmask = _hgrn_consts(False)
    btri, bmask = _hgrn_consts(True)

    def bblk(j):
        return jnp.where(j < nbc, nbc - 1 - j, nb - 1 - (j - nbc))

    def fcol(k):
        return pl.BlockSpec((None, rows, width), lambda b, j: (b, j, k))

    def bcol(k):
        return pl.BlockSpec((None, rows, width), lambda b, j: (b, bblk(j), k))

    consts = [lbs, jnp.asarray(ftri, BF16), jnp.asarray(btri, BF16), jnp.asarray(fmask, F32), jnp.asarray(bmask, F32)]
    state = pltpu.VMEM((heads, HGRN_HEAD_DIM, HGRN_HEAD_DIM), F32)
    return pl.pallas_call(
        functools.partial(_hgrn_kernel, heads=heads),
        grid=(b, nb),
        in_specs=[fcol(0), fcol(1), fcol(0), bcol(0), bcol(1), bcol(1)] + [_const_spec(a.shape) for a in consts],
        out_specs=[fcol(0), bcol(0)],
        out_shape=[jax.ShapeDtypeStruct((b, tt, width), BF16)] * 2,
        scratch_shapes=[state, state],
        compiler_params=_cparams(("parallel", "arbitrary"), V7X_VMEM_LIMIT),
        name="hgrn_scan",
    )(pb, pb, pz, pb, pb, pz, *consts)


def _even_out_kernel(*refs, groups, n_x):
    of_ref, ob_ref, pg_ref, pu_ref, pv_ref = refs[:5]
    x_refs = refs[5:5 + n_x]
    mod_ref, onw_ref, vnw_ref, ws_ref, bsb_ref, wa_ref, wb_ref, o_ref = refs[5 + n_x:]
    gd = pu_ref.shape[2] // groups
    hd = HGRN_HEAD_DIM
    onw = onw_ref[...]
    ms, yas = [], []
    for s in range(NB):
        o = of_ref[s].astype(F32) + ob_ref[s].astype(F32)
        yn = jnp.concatenate([_rms(o[:, h * hd:(h + 1) * hd]) * onw for h in range(o.shape[1] // hd)], axis=1)
        yas.append((yn * _silu(pg_ref[s].astype(F32))).astype(BF16))
    y = _dot(jnp.concatenate(yas, axis=0), wa_ref[...])
    for s in range(NB):
        for n in range(TM // MLP_CHUNK):
            rows = slice(n * MLP_CHUNK, (n + 1) * MLP_CHUNK)
            u = _gelu_tanh(pu_ref[s, rows, :].astype(F32))
            v = _gelu_tanh(pv_ref[s, rows, :].astype(F32))
            parts = []
            for g in range(groups):
                cs = slice(g * gd, (g + 1) * gd)
                vg = (_rms(v[:, cs]) * vnw_ref[:, cs]).astype(BF16)
                sv = _dot(ws_ref[g], vg) + bsb_ref[:, cs]
                parts.append(u[:, cs] * sv)
            ms.append(jnp.concatenate(parts, axis=1).astype(BF16))
    y = y + _dot(jnp.concatenate(ms, axis=0), wb_ref[...])
    for s in range(NB):
        o_ref[s] = _stream_rows(x_refs, s) + mod_ref[s, 2:3, :] * y[s * TM:(s + 1) * TM]


def _even_out(o_fwd, o_bwd, onw, pb, xs, mod, vnw, ws, bsb, w_out, a_width):
    b, tt, _ = pb.shape
    d = w_out.shape[1]
    bw = d - a_width
    groups = ws.shape[0]
    ucol = 3 * a_width // bw
    x_specs, x_args = _stream_specs(xs, d)
    return pl.pallas_call(
        functools.partial(_even_out_kernel, groups=groups, n_x=len(x_args)),
        grid=(b // NB, tt // TM),
        in_specs=[
            _rows_spec(a_width),
            _rows_spec(a_width),
            _rows_spec(a_width, 2),
            _rows_spec(bw, ucol),
            _rows_spec(bw, ucol + 1)] + x_specs + [
            _mod_spec(d, NB),
            _const_spec((1, HGRN_HEAD_DIM)),
            _const_spec((1, bw)),
            _const_spec(ws.shape),
            _const_spec(bsb.shape),
            _const_spec((a_width, d)),
            _const_spec((bw, d)),
        ],
        out_specs=_rows_spec(d),
        out_shape=jax.ShapeDtypeStruct((b, tt, d), F32),
        compiler_params=_cparams(("parallel", "parallel"), V7X_VMEM_LIMIT),
        name="even_out",
    )(o_fwd, o_bwd, pb, pb, pb, *x_args, mod, onw.reshape(1, HGRN_HEAD_DIM), vnw, ws, bsb,
      w_out[:a_width], w_out[a_width:])


def _odd_in_kernel(x_ref, mod_ref, nw_ref, win_ref, qaw_ref, kvaw_ref, wq_ref, qnw_ref, wkv_ref, knw_ref,
                   cq_ref, sq_ref, ck_ref, sk_ref, bc_ref,
                   xc_ctx_ref, xc_lat_ref, qt_ref, k_ref, vt_ref, *, heads, cw, ql, kvl):
    cq, sq, ck, sk = cq_ref[...], sq_ref[...], ck_ref[...], sk_ref[...]
    rope = MLA_ROPE
    o = cw + ql + kvl
    hw = heads * 128
    ones_pad = (lax.broadcasted_iota(jnp.int32, (V_PAD, TM), 0) == 0).astype(F32)
    is_ctx = pl.program_id(1) == 0

    def project(s):
        h = _modulate(x_ref[s], nw_ref[...], mod_ref[s, 0:1, :], mod_ref[s, 1:2, :])
        return _dot(h.astype(BF16), win_ref[...])

    def latents(s, p):
        xc = _dot(p[:, :cw].astype(BF16), bc_ref[...])
        xc_lat_ref[s] = xc.astype(xc_lat_ref.dtype)

        @pl.when(is_ctx)
        def _():
            xc_ctx_ref[s] = xc.astype(xc_ctx_ref.dtype)
        q_lat = (_rms(p[:, cw:cw + ql]) * qaw_ref[...]).astype(BF16)
        kv_lat = (_rms(p[:, cw + ql:cw + ql + kvl]) * kvaw_ref[...]).astype(BF16)
        kp, kps = p[:, o:o + 128], p[:, o + 128:o + 256]
        kpr = lax.rsqrt(jnp.sum(kp * kp, axis=-1, keepdims=True) * (1.0 / rope) + EPS)
        k_pe = (kp * ck + kps * sk) * kpr
        qf = _dot(q_lat, wq_ref[...])
        kvf = _dot(kv_lat, wkv_ref[...])
        return qf, kvf, k_pe

    def heads_out(s, qf, kvf, k_pe):
        for hh in range(heads):
            cs = slice(hh * 128, (hh + 1) * 128)
            qn = _rms(qf[:, cs]) * qnw_ref[...]
            qp = qf[:, hw + hh * 128:hw + (hh + 1) * 128]
            qps = qf[:, 2 * hw + hh * 128:2 * hw + (hh + 1) * 128]
            qpr = lax.rsqrt(jnp.sum(qp * qp, axis=-1, keepdims=True) * (1.0 / rope) + EPS)
            q_pe = (qp * cq + qps * sq) * qpr
            qt_ref[s, hh] = jnp.concatenate([qn, q_pe], axis=1).T.astype(qt_ref.dtype)
            kn = _rms(kvf[:, cs]) * knw_ref[...]
            k_ref[s, hh] = jnp.concatenate([kn, k_pe], axis=1).astype(k_ref.dtype)
            vt_ref[s, hh] = jnp.concatenate([kvf[:, hw + hh * 128:hw + (hh + 1) * 128].T, ones_pad],
                                            axis=0).astype(vt_ref.dtype)

    mid = latents(0, project(0))
    for s in range(1, NB):
        p_next = project(s)
        heads_out(s - 1, *mid)
        mid = latents(s, p_next)
    heads_out(NB - 1, *mid)


def _odd_in(x, mod, nw, wts, heads, cw, ql, kvl):
    b, tt, d = x.shape
    consts = [wts[k] for k in ("w_in", "qa_w", "kva_w", "w_q", "qn_w", "w_kv", "kn_w")]
    tables = [wts[k] for k in ("cq", "sq", "ck", "sk")]
    row128 = pl.BlockSpec((TM, 128), lambda b, i: (i, 0))
    in_specs = ([_rows_spec(d), _mod_spec(d, NB), _const_spec((1, d))]
                + [_const_spec(a.shape) for a in consts] + [row128] * 4 + [_const_spec(wts["bc"].shape)])
    return pl.pallas_call(
        functools.partial(_odd_in_kernel, heads=heads, cw=cw, ql=ql, kvl=kvl),
        grid=(b // NB, tt // TM),
        in_specs=in_specs,
        out_specs=[
            pl.BlockSpec((NB, TM, 2 * cw), lambda b, i: (b, 0, 0)),
            pl.BlockSpec((NB, TM, 2 * cw), lambda b, i: (b, jnp.maximum(i - 1, 0), 0)),
            pl.BlockSpec((NB, heads, 256, TM), lambda b, i: (b, 0, 0, i)),
            pl.BlockSpec((NB, heads, TM, 256), lambda b, i: (b, 0, i, 0)),
            pl.BlockSpec((NB, heads, MLA_V_DIM + V_PAD, TM), lambda b, i: (b, 0, 0, i)),
        ],
        out_shape=[
            jax.ShapeDtypeStruct((b, TM, 2 * cw), BF16),
            jax.ShapeDtypeStruct((b, tt - TM, 2 * cw), F32),
            jax.ShapeDtypeStruct((b, heads, 256, tt), BF16),
            jax.ShapeDtypeStruct((b, heads, tt, 256), BF16),
            jax.ShapeDtypeStruct((b, heads, MLA_V_DIM + V_PAD, tt), BF16),
        ],
        compiler_params=_cparams(("parallel", "arbitrary"), V7X_VMEM_LIMIT),
        name="odd_in",
    )(x, mod, nw, *consts, *tables, wts["bc"])


def _attn_kernel(qt_ref, k_ref, vt_ref, o_ref, sa_ref, sb_ref, ma_ref, mb_ref, *, ctx_len, n_tiles):
    i = pl.program_id(2)
    n_all = k_ref.shape[1]
    nh = k_ref.shape[0]
    bufs = ((sa_ref, ma_ref), (sb_ref, mb_ref))

    def chunks(n_keys):
        return [(st, min(ATTN_KC, n_keys - st)) for st in range(0, n_keys, ATTN_KC)]

    def run(step_parity, keys1, keys2):
        s_w, m_w = bufs[step_parity]
        s_r, m_r = bufs[1 - step_parity]
        c1 = chunks(keys1) if keys1 else []
        c2 = chunks(keys2) if keys2 else []
        if keys1:
            qts = [qt_ref[hh] for hh in range(nh)]
        if keys2:
            m_prev = [m_r[hh] for hh in range(nh)]
        m, acc = [None] * nh, [None] * nh
        for idx in range(max(len(c1), len(c2))):
            for hh in range(nh):
                if idx < len(c1):
                    st, sz = c1[idx]
                    s = _dot(k_ref[hh, st:st + sz, :], qts[hh])
                    s_w[hh, st:st + sz, :] = s
                    cm = jnp.max(s, axis=0, keepdims=True)
                    m[hh] = cm if m[hh] is None else jnp.maximum(m[hh], cm)
                if idx < len(c2):
                    st, sz = c2[idx]
                    p = jnp.exp2(s_r[hh, st:st + sz, :] - m_prev[hh]).astype(BF16)
                    ca = _dot(vt_ref[hh, :, st:st + sz], p)
                    acc[hh] = ca if acc[hh] is None else acc[hh] + ca
        dv = o_ref.shape[1] // nh
        for hh in range(nh):
            if keys1:
                m_w[hh] = m[hh]
            if keys2:
                o_ref[:, hh * dv:(hh + 1) * dv] = (acc[hh][:dv] * (1.0 / acc[hh][dv:dv + 1])).T.astype(o_ref.dtype)

    @pl.when(i == 0)
    def _():
        run(0, ctx_len, None)

    @pl.when(i == 1)
    def _():
        run(1, n_all, ctx_len)

    for parity in (0, 1):
        @pl.when(jnp.logical_and(jnp.logical_and(i >= 2, i < n_tiles), i % 2 == parity))
        def _():
            run(parity, n_all, n_all)

    @pl.when(i == n_tiles)
    def _():
        run(n_tiles % 2, None, n_all)


def _attention(qt, k, vt, ctx_len):
    b, heads, tt, dq = k.shape
    dvp = vt.shape[2]
    dv = dvp - V_PAD
    n_tiles = tt // TM
    nh = ATTN_HEADS_PER_STEP
    assert heads % nh == 0
    return pl.pallas_call(
        functools.partial(_attn_kernel, ctx_len=ctx_len, n_tiles=n_tiles),
        grid=(b, heads // nh, n_tiles + 1),
        in_specs=[
            pl.BlockSpec((None, nh, dq, TM), lambda b, h, i: (b, h, 0, jnp.minimum(i, n_tiles - 1))),
            pl.BlockSpec((None, nh, tt, dq), lambda b, h, i: (b, h, 0, 0)),
            pl.BlockSpec((None, nh, dvp, tt), lambda b, h, i: (b, h, 0, 0)),
        ],
        out_specs=pl.BlockSpec((None, TM, nh * dv), lambda b, h, i: (b, jnp.maximum(i - 1, 0), h)),
        out_shape=jax.ShapeDtypeStruct((b, tt, heads * dv), BF16),
        scratch_shapes=[pltpu.VMEM((nh, tt, TM), F32), pltpu.VMEM((nh, tt, TM), F32),
                        pltpu.VMEM((nh, 1, TM), F32), pltpu.VMEM((nh, 1, TM), F32)],
        compiler_params=_cparams(("parallel", "parallel", "arbitrary"), V7X_VMEM_LIMIT),
        name="attention",
    )(qt, k, vt)


def _dft_kernel(c_ref, s_ref, x_ref, o_ref):
    half = x_ref.shape[1] // 2
    y = _dot(c_ref[...], x_ref[:, :half]) + _dot(s_ref[...], x_ref[:, half:])
    o_ref[...] = y.astype(o_ref.dtype)


def _dft_dense(xc, t_cos, t_sin):
    b, t, w2 = xc.shape
    return pl.pallas_call(
        _dft_kernel,
        grid=(b,),
        in_specs=[_const_spec((t, t)), _const_spec((t, t)), pl.BlockSpec((None, t, w2), lambda b: (b, 0, 0))],
        out_specs=pl.BlockSpec((None, t, w2 // 2), lambda b: (b, 0, 0)),
        out_shape=jax.ShapeDtypeStruct((b, t, w2 // 2), BF16),
        compiler_params=_cparams(("parallel",), V7X_VMEM_LIMIT),
        name="dft_dense",
    )(t_cos, t_sin, xc)


def _dft_split(t):
    n2 = 1 << ((t.bit_length()) // 2)
    return t // n2, n2


def _swap_major(x):
    return jnp.transpose(x, (1, 0, 2))


def _dft_two_stage_kernel(xr_ref, xi_ref, g_ref, cs2_ref, o_ref, zr_ref, zi_ref, y_ref, *, n1, n2):
    lw = xr_ref.shape[1]
    xr = _swap_major(xr_ref[...].reshape(n1, n2, lw))
    xi = _swap_major(xi_ref[...].reshape(n1, n2, lw))
    for t2 in range(n2):
        p = _dot(g_ref[t2], jnp.concatenate([xr[t2], xi[t2]], axis=1).astype(BF16))
        zr_ref[t2] = p[:n1, :lw] - p[n1:, lw:]
        zi_ref[t2] = p[:n1, lw:] + p[n1:, :lw]
    zr = _swap_major(zr_ref[...])
    zi = _swap_major(zi_ref[...])
    for u1 in range(n1):
        y_ref[u1] = _dot(cs2_ref[...], jnp.concatenate([zr[u1], zi[u1]], axis=0).astype(BF16))
    o_ref[...] = _swap_major(y_ref[...]).reshape(n1 * n2, lw).astype(o_ref.dtype)


def _dft_two_stage(xc, tables):
    g, cs2 = tables
    b, t, w2 = xc.shape
    cw = w2 // 2
    lw = 128
    n1, n2 = _dft_split(t)
    return pl.pallas_call(
        functools.partial(_dft_two_stage_kernel, n1=n1, n2=n2),
        grid=(b, cw // lw),
        in_specs=[pl.BlockSpec((None, t, lw), lambda b, j: (b, 0, j)),
                  pl.BlockSpec((None, t, lw), lambda b, j: (b, 0, cw // lw + j)),
                  _const_spec(g.shape), _const_spec(cs2.shape)],
        out_specs=pl.BlockSpec((None, t, lw), lambda b, j: (b, 0, j)),
        out_shape=jax.ShapeDtypeStruct((b, t, cw), BF16),
        scratch_shapes=[pltpu.VMEM((n2, n1, lw), F32), pltpu.VMEM((n2, n1, lw), F32), pltpu.VMEM((n1, n2, lw), F32)],
        compiler_params=_cparams(("parallel", "parallel"), V7X_VMEM_LIMIT),
        name="dft_two_stage",
    )(xc, xc, g, cs2)


def _odd_out_kernel(fc_ref, fl_ref, at_ref, x_hbm, mod_ref, wf_ref, wa_ref, o_ref, xbuf_ref, sem_ref):
    n_tiles = pl.num_programs(1)
    n_steps = pl.num_programs(0) * n_tiles
    step = pl.program_id(0) * n_tiles + pl.program_id(1)

    def x_copy(m):
        slot = m % X_RING
        return pltpu.make_async_copy(
            x_hbm.at[pl.ds((m // n_tiles) * NB, NB), pl.ds((m % n_tiles) * TM, TM), :],
            xbuf_ref.at[slot], sem_ref.at[slot])

    @pl.when(step == 0)
    def _():
        for m in range(X_RING - 1):
            x_copy(m).start()

    @pl.when(step + (X_RING - 1) < n_steps)
    def _():
        x_copy(step + (X_RING - 1)).start()

    is_ctx = pl.program_id(1) == 0
    fm = jnp.concatenate([jnp.where(is_ctx, fc_ref[s], fl_ref[s]) for s in range(NB)], axis=0)
    at = jnp.concatenate([at_ref[s] for s in range(NB)], axis=0)
    y = _dot(fm, wf_ref[...]) + _dot(at, wa_ref[...])
    x_copy(step).wait()
    slot = step % X_RING
    for s in range(NB):
        o_ref[s] = xbuf_ref[slot, s] + mod_ref[s, 2:3, :] * y[s * TM:(s + 1) * TM]


def _odd_out(fm_ctx, fm_lat, attn, x, mod, w_out):
    b, tt, d = x.shape
    cw = fm_ctx.shape[-1]
    aw = attn.shape[-1]
    return pl.pallas_call(
        _odd_out_kernel,
        grid=(b // NB, tt // TM),
        in_specs=[
            pl.BlockSpec((NB, TM, cw), lambda b, i: (b, 0, 0)),
            pl.BlockSpec((NB, TM, cw), lambda b, i: (b, jnp.maximum(i - 1, 0), 0)),
            _rows_spec(aw),
            pl.BlockSpec(memory_space=pl.ANY),
            _mod_spec(d, NB),
            _const_spec((cw, d)),
            _const_spec((aw, d)),
        ],
        out_specs=_rows_spec(d),
        out_shape=jax.ShapeDtypeStruct((b, tt, d), F32),
        scratch_shapes=[pltpu.VMEM((X_RING, NB, TM, d), F32), pltpu.SemaphoreType.DMA((X_RING,))],
        compiler_params=_cparams(("arbitrary", "arbitrary"), V7X_VMEM_LIMIT),
        name="odd_out",
    )(fm_ctx, fm_lat, attn, x, mod, w_out[:cw], w_out[cw:])


def _ffn_kernel(xp_ref, x_ref, xn_ref, mod_ref, nw_ref, wg_ref, wv_ref, cw_ref, cb_ref, wd_ref, o_ref,
                *, first_tile, n_tiles):
    i = pl.program_id(1) + first_tile
    ext = TM + 2 * HALO
    row = lax.broadcasted_iota(jnp.int32, (ext, 1), 0)
    keep = jnp.logical_and(jnp.logical_or(row >= HALO, i > 1),
                           jnp.logical_or(row < HALO + TM, jnp.logical_and(i > 0, i < n_tiles - 1)))
    nw = nw_ref[...]
    hes = []
    for s in range(NB):
        xe = jnp.concatenate([xp_ref[s], x_ref[s], xn_ref[s]], axis=0)
        hes.append(jnp.where(keep, _modulate(xe, nw, mod_ref[s, 3:4, :], mod_ref[s, 4:5, :]), 0.0))
    ge = _dot(jnp.concatenate(hes, axis=0).astype(BF16), wg_ref[...])
    val = _dot(jnp.concatenate([h[HALO:HALO + TM] for h in hes], axis=0).astype(BF16), wv_ref[...])
    cw0, cw1, cw2 = cw_ref[0:1, :], cw_ref[1:2, :], cw_ref[2:3, :]
    gc = jnp.concatenate(
        [ge[s * ext + HALO - 1:s * ext + HALO - 1 + TM] * cw0 + ge[s * ext + HALO:s * ext + HALO + TM] * cw1
         + ge[s * ext + HALO + 1:s * ext + HALO + 1 + TM] * cw2 for s in range(NB)], axis=0) + cb_ref[...]
    y = _dot((_silu(gc) * val).astype(BF16), wd_ref[...])
    for s in range(NB):
        o_ref[s] = x_ref[s] + mod_ref[s, 5:6, :] * y[s * TM:(s + 1) * TM]


def _conv_ffn(x, mod, nw, w_up, conv_w, conv_b, wd, skip_ctx):
    b, tt, d = x.shape
    ff = wd.shape[0]
    n_tiles = tt // TM
    first = 1 if skip_ctx else 0
    r = TM // HALO
    nblk = tt // HALO
    return pl.pallas_call(
        functools.partial(_ffn_kernel, first_tile=first, n_tiles=n_tiles),
        grid=(b // NB, n_tiles - first),
        in_specs=[
            pl.BlockSpec((NB, HALO, d), lambda b, i: (b, jnp.maximum((i + first) * r - 1, 0), 0)),
            pl.BlockSpec((NB, TM, d), lambda b, i: (b, i + first, 0)),
            pl.BlockSpec((NB, HALO, d), lambda b, i: (b, jnp.minimum((i + first + 1) * r, nblk - 1), 0)),
            pl.BlockSpec((None, NB, 6, d), lambda b, i: (jnp.minimum(i + first, 1), b, 0, 0)),
            _const_spec((1, d)),
            pl.BlockSpec((d, ff), lambda b, i: (0, 0), pipeline_mode=pl.Buffered(1)),
            pl.BlockSpec((d, ff), lambda b, i: (0, 1), pipeline_mode=pl.Buffered(1)),
            _const_spec((CONV_W, ff)),
            _const_spec((1, ff)),
            _const_spec((ff, d)),
        ],
        out_specs=_rows_spec(d),
        out_shape=jax.ShapeDtypeStruct((b, tt - first * TM, d), F32),
        compiler_params=_cparams(("parallel", "parallel"), V7X_VMEM_LIMIT),
        name="conv_ffn",
    )(x, x, x, mod, nw, w_up, w_up, conv_w, conv_b, wd)


def _rope_swap_perm():
    q = MLA_ROPE // 4
    return np.concatenate([np.arange(q, 2 * q), np.arange(0, q), np.arange(3 * q, 4 * q), np.arange(2 * q, 3 * q)])


def _pad128(v):
    return jnp.pad(v, (0, 128 - v.shape[0])).reshape(1, 128)


def _odd_weights(w_in, qa_w, w_qb, kva_w, w_kvb, qn_w, kn_w, heads, cw, ql, kvl, cos_t, sin_t):
    d = w_in.shape[0]
    perm = _rope_swap_perm()
    nope, rope, qk = MLA_NOPE, MLA_ROPE, MLA_NOPE + MLA_ROPE
    o = cw + ql + kvl
    z = jnp.zeros((d, 128 - rope), w_in.dtype)
    kpe = w_in[:, o:o + rope]
    w_in_ext = jnp.concatenate([w_in[:, :o], kpe, z, kpe[:, perm], z], axis=1).astype(BF16)
    wq = w_qb.reshape(ql, heads, qk)
    zq = jnp.zeros((ql, heads, 128 - rope), w_qb.dtype)
    wq_rope = wq[:, :, nope:]
    w_q = jnp.concatenate([
        wq[:, :, :nope].reshape(ql, heads * 128),
        jnp.concatenate([wq_rope, zq], axis=2).reshape(ql, heads * 128),
        jnp.concatenate([wq_rope[:, :, perm], zq], axis=2).reshape(ql, heads * 128),
    ], axis=1).astype(BF16)
    wkv = w_kvb.reshape(kvl, heads, nope + MLA_V_DIM)
    w_kv = jnp.concatenate([wkv[:, :, :nope].reshape(kvl, heads * nope),
                            wkv[:, :, nope:].reshape(kvl, heads * MLA_V_DIM)], axis=1).astype(BF16)
    gd = cw // FOURIER_GROUPS
    jk = (np.arange(gd)[:, None] * np.arange(gd)[None, :]) % gd
    ang = 2.0 * np.pi * jk / gd
    eye = np.eye(FOURIER_GROUPS)
    bc = np.concatenate([np.kron(eye, np.cos(ang)), np.kron(eye, -np.sin(ang))], axis=1) / np.sqrt(gd)
    scale = float(qk) ** -0.5 * LOG2_E
    return {
        "w_in": w_in_ext, "qa_w": qa_w.reshape(1, ql), "kva_w": kva_w.reshape(1, kvl),
        "w_q": w_q, "qn_w": (qn_w[:nope] * scale).reshape(1, nope),
        "w_kv": w_kv, "kn_w": kn_w[:nope].reshape(1, nope),
        "cq": cos_t * (_pad128(qn_w[nope:]) * scale), "sq": sin_t * (_pad128(qn_w[nope:][perm]) * scale),
        "ck": cos_t * _pad128(kn_w[nope:]), "sk": sin_t * _pad128(kn_w[nope:][perm]),
        "bc": jnp.asarray(bc, F32).astype(BF16),
    }


def _rope_tables(t_lat, ctx_len):
    rows = t_lat // GRID_W
    row = jnp.repeat(jnp.arange(rows), GRID_W)
    col = jnp.tile(jnp.arange(GRID_W), rows)
    r_axis = MLA_ROPE // 2
    inv_freq = ROPE_THETA ** (-jnp.arange(0, r_axis, 2, dtype=F32) / r_axis)
    ang = jnp.stack([row, col], axis=-1).astype(F32)[:, :, None] * inv_freq
    cos, sin = jnp.cos(ang), jnp.sin(ang)
    cos64 = jnp.concatenate([cos[:, 0], cos[:, 0], cos[:, 1], cos[:, 1]], axis=-1)
    sin64 = jnp.concatenate([-sin[:, 0], sin[:, 0], -sin[:, 1], sin[:, 1]], axis=-1)
    cos64 = jnp.concatenate([jnp.ones((ctx_len, MLA_ROPE), F32), cos64], axis=0)
    sin64 = jnp.concatenate([jnp.zeros((ctx_len, MLA_ROPE), F32), sin64], axis=0)
    pad = ((0, 0), (0, 128 - MLA_ROPE))
    return jnp.pad(cos64, pad), jnp.pad(sin64, pad)


def _cos_sin(phase, period, scale):
    ang = (phase % period).astype(F32) * (2.0 * np.pi / period)
    return jnp.cos(ang) * scale, jnp.sin(ang) * scale


def _dft_dense_tables(t):
    idx = jnp.arange(t, dtype=jnp.int32)
    c, s = _cos_sin(idx[:, None] * idx[None, :], t, 1.0 / np.sqrt(t))
    return c.astype(BF16), s.astype(BF16)


def _dft_two_stage_tables(t):
    n1, n2 = _dft_split(t)
    u1 = jnp.arange(n1, dtype=jnp.int32)
    pos = jnp.arange(t, dtype=jnp.int32).reshape(n1, n2)
    c, s = _cos_sin(pos.T[:, None, :] * u1[None, :, None], t, 1.0 / np.sqrt(n1))
    g = jnp.concatenate([c, -s], axis=1).astype(BF16)
    i2 = jnp.arange(n2, dtype=jnp.int32)
    c2, s2 = _cos_sin(i2[:, None] * i2[None, :], n2, 1.0 / np.sqrt(n2))
    return g, jnp.concatenate([c2, s2], axis=1).astype(BF16)


def kernel(x, c, ctx, c_ctx, ada_w, ada_b, norm_mix_w, norm_ffn_w, ev_w_in, ev_lb_logits, ev_onorm_w, ev_vnorm_w, ev_ws, ev_bs, ev_w_out, od_w_in, od_qa_norm_w, od_w_qb, od_kva_norm_w, od_w_kvb, od_q_norm_w, od_k_norm_w, od_w_out, ffn_w_up, ffn_conv_w, ffn_conv_b, ffn_w_down):
    bsz, t_lat, d = x.shape
    ctx_len = ctx.shape[1]
    depth = ada_w.shape[0]
    assert ctx_len == TM and t_lat % ATTN_KC == 0 and t_lat % GRID_W == 0 and bsz % NB == 0
    a_width = ev_lb_logits.shape[-1]
    ql, kvl = od_qa_norm_w.shape[-1], od_kva_norm_w.shape[-1]
    cw = od_w_in.shape[-1] - ql - kvl - MLA_ROPE
    heads = od_w_qb.shape[-1] // (MLA_NOPE + MLA_ROPE)
    d_ff = ffn_w_down.shape[1]

    pad_rows = (-(bsz + 1)) % 8
    cvec = jnp.concatenate([c, c_ctx[None, :], jnp.zeros((pad_rows, d), F32)], axis=0)
    mods = _ada_tables(cvec, ada_w, ada_b)
    mod_lat = mods[:, :bsz].reshape(depth, 1, bsz, 6, d)
    mod_ctx = jnp.broadcast_to(mods[:, bsz].reshape(depth, 1, 1, 6, d), (depth, 1, bsz, 6, d))
    mods = jnp.concatenate([mod_ctx, mod_lat], axis=1)

    lb_p = jax.nn.softmax(ev_lb_logits.astype(F32), axis=0)
    lbs = jnp.cumsum(lb_p, axis=0) - lb_p[0]
    cos_t, sin_t = _rope_tables(t_lat, ctx_len)
    dft_lat = dft_ctx = None

    xs = (ctx, x)
    for l in range(depth):
        last = l == depth - 1
        mod = mods[l]
        nmw = norm_mix_w[l].reshape(1, d)
        if l % 2 == 0:
            e = l // 2
            pz, pb = _even_in(xs, ctx_len + t_lat, mod, nmw, ev_w_in[e].astype(BF16), a_width)
            o_fwd, o_bwd = _hgrn_scan(pz, pb, lbs[e], ctx_len, a_width)
            bw = d - a_width
            gd = bw // ev_ws.shape[1]
            bsb = jnp.repeat(ev_bs[e].T, gd, axis=1)
            xs = _even_out(o_fwd, o_bwd, ev_onorm_w[e], pb, xs, mod, ev_vnorm_w[e].reshape(1, bw),
                           ev_ws[e].astype(BF16), bsb, ev_w_out[e].astype(BF16), a_width)
        else:
            o = l // 2
            if dft_lat is None:
                dft_lat, dft_ctx = _dft_two_stage_tables(t_lat), _dft_dense_tables(ctx_len)
            wts = _odd_weights(od_w_in[o], od_qa_norm_w[o], od_w_qb[o], od_kva_norm_w[o], od_w_kvb[o],
                               od_q_norm_w[o], od_k_norm_w[o], heads, cw, ql, kvl, cos_t, sin_t)
            xc_ctx, xc_lat, qt, k, vt = _odd_in(xs, mod, nmw, wts, heads, cw, ql, kvl)
            attn = _attention(qt, k, vt, ctx_len)
            fm_ctx = _dft_dense(xc_ctx, *dft_ctx)
            fm_lat = _dft_two_stage(xc_lat, dft_lat)
            xs = _odd_out(fm_ctx, fm_lat, attn, xs, mod, od_w_out[o].astype(BF16))
        xs = _conv_ffn(xs, mod, norm_ffn_w[l].reshape(1, d), ffn_w_up[l].astype(BF16), ffn_conv_w[l],
                       ffn_conv_b[l].reshape(1, d_ff), ffn_w_down[l].astype(BF16), skip_ctx=last)
    return xs
```
